```python
import math
import jax
import jax.numpy as jnp
from jax import lax
import numpy as np


D_MODEL = 1024
BATCH = 32
SEQ = 2048
DEPTH = 2

N_A = DEPTH // 2
N_B = DEPTH - N_A
CONV_W = 3
D_FF = 2816
N_HEADS = 16
N_KV_GROUPS = 4
HPG = N_HEADS // N_KV_GROUPS
HEAD_DIM = 64
CMP_LEN = 32
CMP_STRIDE = 16
CMP_HIDDEN = 128
SLC_LEN = 64
N_SEL = 16
WINDOW = 512
Q_BLOCK = 128
BAND = WINDOW + Q_BLOCK
N_BUCKETS = 32
MAX_DISTANCE = 128
N_KV_SLOTS = 6
EPS = 1e-6
NEG = -1e30

kernel_name = 'yoco_shortconv_nsa_hybrid'


def rmsnorm(x, g):
    xf = x.astype(jnp.float32)
    y = xf * lax.rsqrt(jnp.mean(xf * xf, axis=-1, keepdims=True) + EPS)
    return (y * g.astype(jnp.float32)).astype(x.dtype)


def causal_dwconv(u, w):
    s = u.shape[1]
    up = jnp.pad(u, ((0, 0), (CONV_W - 1, 0), (0, 0)))
    return sum(up[:, k:k + s] * w[k] for k in range(CONV_W))


def rel_bucket(dist):
    max_exact = N_BUCKETS // 2
    d = jnp.maximum(dist, 0)
    df = jnp.maximum(d, 1).astype(jnp.float32)
    large = max_exact + (jnp.log(df / max_exact) / math.log(MAX_DISTANCE / max_exact)
                         * (N_BUCKETS - max_exact)).astype(jnp.int32)
    return jnp.where(d < max_exact, d, jnp.minimum(large, N_BUCKETS - 1))


def short_conv_mixer(h, w_in, conv_w, w_out):
    bg, cg, v = jnp.split(h @ w_in, 3, axis=-1)
    return (bg * causal_dwconv(cg * v, conv_w)) @ w_out


def conv_ffn(h, w_up, conv_w, w_down):
    u = causal_dwconv(h @ w_up, conv_w)
    a, g = jnp.split(u, 2, axis=-1)
    return (jax.nn.silu(a) * g) @ w_down


def nsa_shared_kv(s, w_kv, cmp_pe, cmp_w1, cmp_b1, cmp_w2):
    b, n, _ = s.shape
    kv = (s @ w_kv).reshape(b, n, N_KV_SLOTS, N_KV_GROUPS, HEAD_DIM)
    n_c = (n - CMP_LEN) // CMP_STRIDE + 1
    n_s = n // SLC_LEN
    idx = (jnp.arange(n_c) * CMP_STRIDE)[:, None] + jnp.arange(CMP_LEN)[None, :]

    def compress(tok, j):
        blk = tok[:, idx] + cmp_pe[j][None, None, :, None, :]
        blk = blk.transpose(0, 1, 3, 2, 4).reshape(b, n_c, N_KV_GROUPS, CMP_LEN * HEAD_DIM)
        return jax.nn.gelu(blk @ cmp_w1[j] + cmp_b1[j]) @ cmp_w2[j]

    k_cmp = compress(kv[:, :, 0], 0)
    v_cmp = compress(kv[:, :, 1], 1)
    k_slc = kv[:, :, 2].reshape(b, n_s, SLC_LEN, N_KV_GROUPS, HEAD_DIM)
    v_slc = kv[:, :, 3].reshape(b, n_s, SLC_LEN, N_KV_GROUPS, HEAD_DIM)
    pad = ((0, 0), (WINDOW, 0), (0, 0), (0, 0))
    k_win = jnp.pad(kv[:, :, 4], pad)
    v_win = jnp.pad(kv[:, :, 5], pad)
    return (k_cmp, v_cmp, k_slc, v_slc, k_win, v_win)


def nsa_query_block(q, gate, qs, k_cmp, v_cmp, k_slc, v_slc, k_win, v_win, rel_bias, overlap):
    t_len = q.shape[0]
    n_c = k_cmp.shape[0]
    n_s = k_slc.shape[0]
    n_sel = min(N_SEL, n_s)
    garr = jnp.arange(N_KV_GROUPS)
    tb = rel_bias.astype(jnp.float32).reshape(N_BUCKETS, N_KV_GROUPS, HPG)
    qg = (q * HEAD_DIM ** -0.5).reshape(t_len, N_KV_GROUPS, HPG, HEAD_DIM).transpose(1, 2, 0, 3)
    t_pos = qs + jnp.arange(t_len)

    c_end = jnp.arange(n_c) * CMP_STRIDE + (CMP_LEN - 1)
    dist_c = t_pos[:, None] - c_end[None, :]
    mask_c = dist_c >= 0
    bias_c = tb[rel_bucket(dist_c)].transpose(2, 3, 0, 1)
    s_c = jnp.einsum('ghtd,cgd->ghtc', qg, k_cmp).astype(jnp.float32) + bias_c
    p_c = jax.nn.softmax(jnp.where(mask_c, s_c, NEG), axis=-1) * mask_c
    o_c = jnp.einsum('ghtc,cgd->ghtd', p_c.astype(v_cmp.dtype), v_cmp)

    imp = jnp.einsum('ghtc,cs->gts', p_c, overlap)
    cur = t_pos // SLC_LEN
    jb = jnp.arange(n_s)[None, :]
    forced = (jb == 0) | (jb == cur[:, None]) | (jb == cur[:, None] - 1)
    score = jnp.where(forced, jnp.inf, jnp.where(jb <= cur[:, None], imp, -jnp.inf))
    _, sel = lax.top_k(score, n_sel)
    key_pos = sel[..., None] * SLC_LEN + jnp.arange(SLC_LEN)
    dist_s = t_pos[None, :, None, None] - key_pos
    mask_s = dist_s >= 0
    ks = k_slc.transpose(2, 0, 1, 3)[garr[:, None, None], sel]
    vs = v_slc.transpose(2, 0, 1, 3)[garr[:, None, None], sel]
    bias_s = jnp.moveaxis(tb[rel_bucket(dist_s), garr[:, None, None, None]], -1, 1)
    s_s = jnp.einsum('ghtd,gtnld->ghtnl', qg, ks).astype(jnp.float32) + bias_s
    s_s = jnp.where(mask_s[:, None], s_s, NEG)
    p_s = jax.nn.softmax(s_s.reshape(N_KV_GROUPS, HPG, t_len, -1), axis=-1).reshape(s_s.shape)
    o_s = jnp.einsum('ghtnl,gtnld->ghtd', p_s.astype(vs.dtype), vs)

    kw = lax.dynamic_slice_in_dim(k_win, qs, BAND, axis=0)
    vw = lax.dynamic_slice_in_dim(v_win, qs, BAND, axis=0)
    key_pos_w = qs - WINDOW + jnp.arange(BAND)
    dist_w = t_pos[:, None] - key_pos_w[None, :]
    mask_w = (dist_w >= 0) & (dist_w < WINDOW) & (key_pos_w[None, :] >= 0)
    bias_w = tb[rel_bucket(dist_w)].transpose(2, 3, 0, 1)
    s_w = jnp.einsum('ghtd,kgd->ghtk', qg, kw).astype(jnp.float32) + bias_w
    p_w = jax.nn.softmax(jnp.where(mask_w, s_w, NEG), axis=-1)
    o_w = jnp.einsum('ghtk,kgd->ghtd', p_w.astype(vw.dtype), vw)

    g = gate.reshape(t_len, N_KV_GROUPS, HPG, 3).transpose(1, 2, 0, 3)
    o = g[..., 0:1] * o_c + g[..., 1:2] * o_s + g[..., 2:3] * o_w
    return o.transpose(2, 0, 1, 3).reshape(t_len, N_HEADS * HEAD_DIM)


def nsa_mixer(h, w_qg, w_o, kv, rel_bias):
    b, n, _ = h.shape
    k_cmp, v_cmp, k_slc, v_slc, k_win, v_win = kv
    n_c = k_cmp.shape[1]
    n_s = k_slc.shape[1]
    nq = n // Q_BLOCK
    c_start = jnp.arange(n_c)[:, None] * CMP_STRIDE
    s_start = jnp.arange(n_s)[None, :] * SLC_LEN
    overlap = ((c_start < s_start + SLC_LEN) & (c_start + CMP_LEN > s_start)).astype(jnp.float32)
    hd = N_HEADS * HEAD_DIM
    qg = h @ w_qg
    q = qg[..., :hd].reshape(b * nq, Q_BLOCK, N_HEADS, HEAD_DIM)
    gate = jax.nn.sigmoid(qg[..., hd:]).reshape(b * nq, Q_BLOCK, N_HEADS, 3)
    b_idx = jnp.repeat(jnp.arange(b), nq)
    q_start = jnp.tile(jnp.arange(nq) * Q_BLOCK, b)

    def step(args):
        qb, gb, bi, qs = args
        return nsa_query_block(qb, gb, qs, k_cmp[bi], v_cmp[bi], k_slc[bi], v_slc[bi],
                               k_win[bi], v_win[bi], rel_bias, overlap)

    o = lax.map(step, (q, gate, b_idx, q_start))
    return o.reshape(b, n, hd) @ w_o


def setup_inputs(seed: int = 0) -> dict:
    key = jax.random.key(seed)
    k = jax.random.split(key, 20)

    def nrm(kk, shape, scale):
        return jax.random.normal(kk, shape, jnp.float32) * scale

    def gain(kk, shape):
        return 1.0 + 0.02 * jax.random.normal(kk, shape, jnp.float32)

    d, f, hd = D_MODEL, D_FF, N_HEADS * HEAD_DIM
    return {
        'x': nrm(k[0], (BATCH, SEQ, d), 1.0),
        'mix_norm': gain(k[1], (DEPTH, d)),
        'a_w_in': nrm(k[2], (N_A, d, 3 * d), d ** -0.5),
        'a_conv': nrm(k[3], (N_A, CONV_W, d), CONV_W ** -0.5),
        'a_w_out': nrm(k[4], (N_A, d, d), d ** -0.5),
        'ffn_norm': gain(k[5], (DEPTH, d)),
        'ffn_up': nrm(k[6], (DEPTH, d, 2 * f), d ** -0.5),
        'ffn_conv': nrm(k[7], (DEPTH, CONV_W, 2 * f), CONV_W ** -0.5),
        'ffn_down': nrm(k[8], (DEPTH, f, d), f ** -0.5),
        'kv_norm': gain(k[9], (d,)),
        'w_kv': nrm(k[10], (d, N_KV_SLOTS * N_KV_GROUPS * HEAD_DIM), d ** -0.5),
        'cmp_pe': nrm(k[11], (2, CMP_LEN, HEAD_DIM), 0.1),
        'cmp_w1': nrm(k[12], (2, CMP_LEN * HEAD_DIM, CMP_HIDDEN), (CMP_LEN * HEAD_DIM) ** -0.5),
        'cmp_b1': nrm(k[13], (2, CMP_HIDDEN), 0.02),
        'cmp_w2': nrm(k[14], (2, CMP_HIDDEN, HEAD_DIM), CMP_HIDDEN ** -0.5),
        'b_w_qg': nrm(k[15], (N_B, d, hd + 3 * N_HEADS), d ** -0.5),
        'b_w_o': nrm(k[16], (N_B, hd, d), hd ** -0.5),
        'rel_bias': nrm(k[17], (N_BUCKETS, N_HEADS), 0.5),
        'final_norm': gain(k[18], (d,)),
    }


def reference(x, mix_norm, a_w_in, a_conv, a_w_out, ffn_norm, ffn_up, ffn_conv, ffn_down,
              kv_norm, w_kv, cmp_pe, cmp_w1, cmp_b1, cmp_w2, b_w_qg, b_w_o, rel_bias, final_norm):
    kv = None
    for layer in range(DEPTH):
        if layer < N_A:
            i = layer
            x = x + short_conv_mixer(rmsnorm(x, mix_norm[layer]), a_w_in[i], a_conv[i], a_w_out[i])
        else:
            i = layer - N_A
            if layer == N_A:
                kv = nsa_shared_kv(rmsnorm(x, kv_norm), w_kv, cmp_pe, cmp_w1, cmp_b1, cmp_w2)
            x = x + nsa_mixer(rmsnorm(x, mix_norm[layer]), b_w_qg[i], b_w_o[i], kv, rel_bias)
        x = x + conv_ffn(rmsnorm(x, ffn_norm[layer]), ffn_up[layer], ffn_conv[layer], ffn_down[layer])
    return rmsnorm(x, final_norm)
```

```python
import functools
import math

import jax
import jax.numpy as jnp
import numpy as np
from jax import lax
from jax.experimental import pallas as pl
from jax.experimental.pallas import tpu as pltpu

D_MODEL = 1024
CONV_W = 3
D_FF = 2816
N_HEADS = 16
N_GROUPS = 4
HPG = N_HEADS // N_GROUPS
HEAD_DIM = 64
CMP_LEN = 32
CMP_STRIDE = 16
CMP_HIDDEN = 128
SLC_LEN = 64
N_SEL = 16
WINDOW = 512
N_BUCKETS = 32
MAX_DISTANCE = 128
EPS = 1e-6
NEG = -1e30

LANES = 128
V7X_VMEM_BYTES = 64 * 2**20
VMEM_LIMIT = V7X_VMEM_BYTES - 8 * 2**20

TQ = 256
CARRY_ROWS = 8
F32 = jnp.float32
BF16 = jnp.bfloat16


def _dot(a, b):
    return jnp.dot(a, b, preferred_element_type=F32)


def _dot_nt(a, b, precision=None):
    return lax.dot_general(a, b, (((1,), (1,)), ((), ())),
                           preferred_element_type=F32, precision=precision)


def _rms_scale(x):
    return x * lax.rsqrt(jnp.mean(x * x, axis=-1, keepdims=True) + EPS)


def _conv3(buf_ref, cw, rows):
    c = CARRY_ROWS
    return (cw[0:1, :] * buf_ref[c - 2:c - 2 + rows, :]
            + cw[1:2, :] * buf_ref[c - 1:c - 1 + rows, :]
            + cw[2:3, :] * buf_ref[c:c + rows, :])


def _mixer_kernel(x_ref, g_ref, win_ref, cw_ref, wout_ref, o_ref, ubuf_ref, *, tm):
    d = D_MODEL

    @pl.when(pl.program_id(1) == 0)
    def _():
        ubuf_ref[0:CARRY_ROWS, :] = jnp.zeros((CARRY_ROWS, d), F32)

    x = x_ref[0]
    h = (_rms_scale(x) * g_ref[...]).astype(BF16)
    cg = _dot(h, win_ref[:, d:2 * d])
    v = _dot(h, win_ref[:, 2 * d:3 * d])
    ubuf_ref[CARRY_ROWS:CARRY_ROWS + tm, :] = cg * v
    conv = _conv3(ubuf_ref, cw_ref[...], tm)
    ubuf_ref[0:CARRY_ROWS, :] = ubuf_ref[tm:tm + CARRY_ROWS, :]
    bg = _dot(h, win_ref[:, 0:d])
    y = (bg * conv).astype(BF16)
    o_ref[0] = x + _dot(y, wout_ref[...])


def _mixer(x, g, w_in, conv_w, w_out, *, tm=512):
    b, s, d = x.shape
    const = lambda *_: (0, 0)
    one = pl.Buffered(1)
    return pl.pallas_call(
        functools.partial(_mixer_kernel, tm=tm),
        grid=(b, s // tm),
        in_specs=[
            pl.BlockSpec((1, tm, d), lambda i, j: (i, j, 0)),
            pl.BlockSpec((1, d), const),
            pl.BlockSpec((d, 3 * d), const, pipeline_mode=one),
            pl.BlockSpec((CONV_W, d), const),
            pl.BlockSpec((d, d), const, pipeline_mode=one),
        ],
        out_specs=pl.BlockSpec((1, tm, d), lambda i, j: (i, j, 0)),
        out_shape=jax.ShapeDtypeStruct((b, s, d), F32),
        scratch_shapes=[pltpu.VMEM((CARRY_ROWS + tm, d), F32)],
        compiler_params=pltpu.CompilerParams(
            dimension_semantics=("arbitrary", "arbitrary"), vmem_limit_bytes=VMEM_LIMIT),
        name="mixer",
    )(x, g.reshape(1, d), w_in.astype(BF16), conv_w, w_out.astype(BF16))


def _ffn_kernel(*refs, tm, tf, has_attn, final_norm):
    refs = list(refs)
    x_ref = refs.pop(0)
    a_ref = refs.pop(0) if has_attn else None
    wo_ref = refs.pop(0) if has_attn else None
    g_ref, wup_ref, cw_ref, wdn_ref = refs[:4]
    refs = refs[4:]
    gf_ref = refs.pop(0) if final_norm else None
    o_ref, h_ref, ua_ref, ug_ref, carry_ref, acc_ref = refs

    @pl.when(pl.program_id(1) == 0)
    def _():
        carry_ref[...] = jnp.zeros(carry_ref.shape, F32)

    x = x_ref[0]
    if has_attn:
        x = x + _dot(a_ref[0], wo_ref[...])
    h_ref[...] = (_rms_scale(x) * g_ref[...]).astype(BF16)

    def up_conv(buf_ref, c0):
        cols = slice(c0, c0 + tf)
        buf_ref[0:CARRY_ROWS, :] = carry_ref[:, cols]
        buf_ref[CARRY_ROWS:CARRY_ROWS + tm, :] = _dot(h_ref[...], wup_ref[:, cols])
        carry_ref[:, cols] = buf_ref[tm:tm + CARRY_ROWS, :]
        return _conv3(buf_ref, cw_ref[:, cols], tm)

    for c in range(D_FF // tf):
        a = up_conv(ua_ref, c * tf)
        gt = up_conv(ug_ref, D_FF + c * tf)
        act = (a * jax.nn.sigmoid(a) * gt).astype(BF16)
        part = _dot(act, wdn_ref[c * tf:(c + 1) * tf, :])
        if c == 0:
            acc_ref[...] = x + part
        else:
            acc_ref[...] += part
    y = acc_ref[...]
    if final_norm:
        y = _rms_scale(y) * gf_ref[...]
    o_ref[0] = y


def _ffn(x, g, w_up, conv_w, w_down, *, attn=None, w_o=None, final_g=None, tm=512, tf=1408):
    b, s, d = x.shape
    has_attn = attn is not None
    final_norm = final_g is not None
    const = lambda *_: (0, 0)
    row = lambda i, j: (i, j, 0)
    one = pl.Buffered(1)
    args = [x]
    in_specs = [pl.BlockSpec((1, tm, d), row)]
    if has_attn:
        ka = attn.shape[-1]
        args += [attn, w_o]
        in_specs += [pl.BlockSpec((1, tm, ka), row),
                     pl.BlockSpec((ka, d), const, pipeline_mode=one)]
    args += [g.reshape(1, d), w_up.astype(BF16), conv_w, w_down.astype(BF16)]
    in_specs += [pl.BlockSpec((1, d), const),
                 pl.BlockSpec((d, 2 * D_FF), const, pipeline_mode=one),
                 pl.BlockSpec((CONV_W, 2 * D_FF), const),
                 pl.BlockSpec((D_FF, d), const, pipeline_mode=one)]
    if final_norm:
        args.append(final_g.reshape(1, d))
        in_specs.append(pl.BlockSpec((1, d), const))
    return pl.pallas_call(
        functools.partial(_ffn_kernel, tm=tm, tf=tf, has_attn=has_attn, final_norm=final_norm),
        grid=(b, s // tm),
        in_specs=in_specs,
        out_specs=pl.BlockSpec((1, tm, d), row),
        out_shape=jax.ShapeDtypeStruct((b, s, d), F32),
        scratch_shapes=[
            pltpu.VMEM((tm, d), BF16),
            pltpu.VMEM((CARRY_ROWS + tm, tf), F32),
            pltpu.VMEM((CARRY_ROWS + tm, tf), F32),
            pltpu.VMEM((CARRY_ROWS, 2 * D_FF), F32),
            pltpu.VMEM((tm, d), F32),
        ],
        compiler_params=pltpu.CompilerParams(
            dimension_semantics=("arbitrary", "arbitrary"), vmem_limit_bytes=VMEM_LIMIT),
        name="ffn_attn" if has_attn else "ffn",
    )(*args)


def _proj_kernel(x_ref, gm_ref, gk_ref, wq_ref, wg_ref, wkc_ref, wkr_ref,
                 q_ref, gate_ref, kvc_ref, kvr_ref):
    xn = _rms_scale(x_ref[0])
    hq = (xn * gm_ref[...]).astype(BF16)
    hs = (xn * gk_ref[...]).astype(BF16)
    q_ref[0] = (_dot(hq, wq_ref[...]) * (HEAD_DIM ** -0.5)).astype(BF16)
    gate_ref[0] = jax.nn.sigmoid(_dot(hq, wg_ref[...]))
    kvc_ref[0] = _dot(hs, wkc_ref[...])
    kvr_ref[0] = _dot(hs, wkr_ref[...]).astype(BF16)


def _proj(x, g_mix, g_kv, w_qg, w_kv, *, tm=512):
    b, s, d = x.shape
    hd = N_HEADS * HEAD_DIM
    gd = N_GROUPS * HEAD_DIM
    w_q = w_qg[:, :hd].astype(BF16)
    w_g = jnp.pad(w_qg[:, hd:], ((0, 0), (0, LANES - 3 * N_HEADS))).astype(BF16)
    w_kc = w_kv[:, :2 * gd].astype(BF16)
    w_kr = (w_kv[:, 2 * gd:].reshape(d, 4, N_GROUPS, HEAD_DIM)
            .transpose(0, 2, 1, 3).reshape(d, 4 * gd).astype(BF16))
    const = lambda *_: (0, 0)
    row = lambda i, j: (i, j, 0)
    one = pl.Buffered(1)
    return pl.pallas_call(
        _proj_kernel,
        grid=(b, s // tm),
        in_specs=[
            pl.BlockSpec((1, tm, d), row),
            pl.BlockSpec((1, d), const),
            pl.BlockSpec((1, d), const),
            pl.BlockSpec((d, hd), const, pipeline_mode=one),
            pl.BlockSpec((d, LANES), const, pipeline_mode=one),
            pl.BlockSpec((d, 2 * gd), const, pipeline_mode=one),
            pl.BlockSpec((d, 4 * gd), const, pipeline_mode=one),
        ],
        out_specs=[
            pl.BlockSpec((1, tm, hd), row),
            pl.BlockSpec((1, tm, LANES), row),
            pl.BlockSpec((1, tm, 2 * gd), row),
            pl.BlockSpec((1, tm, 4 * gd), row),
        ],
        out_shape=[
            jax.ShapeDtypeStruct((b, s, hd), BF16),
            jax.ShapeDtypeStruct((b, s, LANES), F32),
            jax.ShapeDtypeStruct((b, s, 2 * gd), F32),
            jax.ShapeDtypeStruct((b, s, 4 * gd), BF16),
        ],
        compiler_params=pltpu.CompilerParams(
            dimension_semantics=("arbitrary", "arbitrary"), vmem_limit_bytes=VMEM_LIMIT),
        name="proj",
    )(x, g_mix.reshape(1, d), g_kv.reshape(1, d), w_q, w_g, w_kc, w_kr)


def _gelu_tanh(x):
    return 0.5 * x * (1.0 + jnp.tanh(math.sqrt(2.0 / math.pi) * (x + 0.044715 * (x * x * x))))


def _compress_kernel(z_ref, pe_ref, w1_ref, b1_ref, w2_ref, o_ref):
    half = (CMP_LEN // 2) * HEAD_DIM
    n_chunks = z_ref.shape[3]
    acc = None
    for j in range(2):
        z = z_ref[0, j, 0]
        za = (z + pe_ref[j, 0:1, :]).astype(BF16)
        zb = (z + pe_ref[j, 1:2, :]).astype(BF16)
        h1 = _dot(za, w1_ref[j, 0:half, :])
        h2 = _dot(zb, w1_ref[j, half:2 * half, :])
        pre = h1 + pltpu.roll(h2, n_chunks - 1, 0) + b1_ref[j]
        part = _dot(_gelu_tanh(pre).astype(BF16), w2_ref[j])
        acc = part if acc is None else acc + part
    o_ref[0, 0] = acc.astype(BF16)


def _compress(kvc, cmp_pe, cmp_w1, cmp_b1, cmp_w2):
    b, s, _ = kvc.shape
    n_chunks = s // CMP_STRIDE
    half = (CMP_LEN // 2) * HEAD_DIM
    z = (kvc.reshape(b, n_chunks, CMP_STRIDE, 2, N_GROUPS, HEAD_DIM)
         .transpose(0, 3, 4, 1, 2, 5).reshape(b, 2, N_GROUPS, n_chunks, half))
    pe = cmp_pe.reshape(2, 2, half)
    w1 = cmp_w1.astype(BF16)
    b1 = cmp_b1.reshape(2, 1, CMP_HIDDEN)
    w2 = jnp.stack([jnp.pad(cmp_w2[0], ((0, 0), (0, HEAD_DIM))),
                    jnp.pad(cmp_w2[1], ((0, 0), (HEAD_DIM, 0)))]).astype(BF16)
    return pl.pallas_call(
        _compress_kernel,
        grid=(b, N_GROUPS),
        in_specs=[
            pl.BlockSpec((1, 2, 1, n_chunks, half), lambda i, g: (i, 0, g, 0, 0)),
            pl.BlockSpec((2, 2, half), lambda *_: (0, 0, 0)),
            pl.BlockSpec((2, 2 * half, CMP_HIDDEN), lambda *_: (0, 0, 0)),
            pl.BlockSpec((2, 1, CMP_HIDDEN), lambda *_: (0, 0, 0)),
            pl.BlockSpec((2, CMP_HIDDEN, LANES), lambda *_: (0, 0, 0)),
        ],
        out_specs=pl.BlockSpec((1, 1, n_chunks, LANES), lambda i, g: (i, g, 0, 0)),
        out_shape=jax.ShapeDtypeStruct((b, N_GROUPS, n_chunks, LANES), BF16),
        compiler_params=pltpu.CompilerParams(
            dimension_semantics=("arbitrary", "arbitrary"), vmem_limit_bytes=VMEM_LIMIT),
        name="compress",
    )(z, pe, w1, b1, w2)


def _rel_bucket_np(dist):
    max_exact = N_BUCKETS // 2
    d = np.maximum(dist, 0)
    df = np.maximum(d, 1).astype(np.float32)
    large = max_exact + (np.log(df / max_exact) / np.float32(math.log(MAX_DISTANCE / max_exact))
                         * (N_BUCKETS - max_exact)).astype(np.int32)
    return np.where(d < max_exact, d, np.minimum(large, N_BUCKETS - 1))


def _attn_tables(rel_bias, s):
    nq = s // TQ
    relb = rel_bias.astype(F32) - rel_bias[N_BUCKETS - 1].astype(F32)
    ti = np.arange(TQ)[:, None]
    ki = np.arange(TQ)[None, :]

    def tile(dist, valid):
        t = jnp.take(relb, jnp.asarray(_rel_bucket_np(dist)), axis=0)
        t = jnp.where(jnp.asarray(valid)[..., None], t, NEG)
        return jnp.moveaxis(t, -1, 0)

    diag = tile(ti - ki, ki <= ti)
    prev = tile(ti - ki + TQ, np.ones((TQ, TQ), bool))
    prev2 = jnp.broadcast_to(jnp.asarray(np.where(ki > ti, 0.0, NEG), F32), diag.shape)
    b4 = jnp.stack([jnp.zeros_like(diag), prev2, prev, diag], axis=1)
    n_cmp = s // CMP_STRIDE
    c_end = np.arange(n_cmp) * CMP_STRIDE + (CMP_LEN - 1)
    c_end[-1] = 2 * s
    dist_c = (np.arange(nq)[:, None, None] * TQ + ti[None]) - c_end[None, None, :]
    bc = jnp.take(relb, jnp.asarray(_rel_bucket_np(dist_c)), axis=0)
    bc = jnp.where(jnp.asarray(dist_c >= 0)[..., None], bc, NEG)
    bc = jnp.moveaxis(bc, -1, 1)
    return b4, bc


def _attn_consts(s):
    n_slc = s // SLC_LEN
    n_cmp = s // CMP_STRIDE
    key = np.arange(s)[:, None]
    lane = np.arange(LANES)[None, :]
    ceneg = np.where((key // SLC_LEN == lane) & (lane < n_slc), NEG, 0.0)
    cones = np.zeros((TQ, LANES), np.float32)
    cones[:, 0] = 1.0
    j = np.arange(n_slc)[:, None]
    i = np.arange(n_cmp)[None, :]
    ov = ((i * CMP_STRIDE < j * SLC_LEN + SLC_LEN) & (i * CMP_STRIDE + CMP_LEN > j * SLC_LEN)
          & (i < n_cmp - 1))
    place = np.eye(n_slc, LANES)
    return (jnp.asarray(ceneg, BF16), jnp.asarray(cones, BF16),
            jnp.asarray(ov, F32), jnp.asarray(place, BF16))


def _attn_kernel(q_ref, gate_ref, kv_ref, kvc_ref, b4_ref, bc_ref, ceneg_ref, cones_ref,
                 ov_ref, place_ref, o_ref, sc_ref, macc_ref, oacc_ref, oc_ref):
    qb = pl.program_id(2)
    n_slc = ov_ref.shape[0]
    lane = lax.broadcasted_iota(jnp.int32, (TQ, LANES), 1)
    low = lane < HEAD_DIM

    q_ops = []
    for pair in range(HPG // 2):
        qp = q_ref[0, :, pair * LANES:(pair + 1) * LANES].astype(F32)
        q_ops.append(jnp.where(low, qp, 0.0).astype(BF16))
        q_ops.append(jnp.where(low, pltpu.roll(qp, HEAD_DIM, 1), 0.0).astype(BF16))

    cones = cones_ref[...]

    kvc = kvc_ref[0, 0]
    rhs_c = jnp.concatenate([kvc, cones[0:kvc.shape[0]]], axis=1)
    psum = jnp.zeros((TQ, LANES), F32)
    for hh in range(HPG):
        bias = bc_ref[0, hh]
        sc = _dot_nt(q_ops[hh], kvc) + bias
        m = jnp.max(sc, axis=1, keepdims=True)
        e = jnp.where(bias > 0.5 * NEG, jnp.exp(sc - m), 0.0)
        l = jnp.sum(e, axis=1, keepdims=True)
        psum = psum + e * (1.0 / jnp.where(l > 0.0, l, 1.0))
        oc_ref[hh] = _dot(e.astype(BF16), rhs_c)

    imp_t = _dot_nt(ov_ref[...], psum, precision=lax.Precision.HIGHEST)
    jrow = lax.broadcasted_iota(jnp.int32, (n_slc, TQ), 0)
    tpos = qb * TQ + lax.broadcasted_iota(jnp.int32, (n_slc, TQ), 1)
    cur = lax.shift_right_logical(tpos, int(math.log2(SLC_LEN)))
    forced = (jrow == 0) | (jrow == cur) | (jrow == cur - 1)
    score = jnp.where(forced, 3e38, jnp.where(jrow <= cur, imp_t, -1.0))
    rank = jnp.zeros((n_slc, TQ), F32)
    for k in range(n_slc):
        sk = score[k:k + 1, :]
        beats = (sk > score) | ((sk == score) & (jrow > k))
        rank = rank + beats.astype(F32)
    notsel_t = (rank >= float(N_SEL)).astype(BF16)
    q_mask = lax.dot_general(notsel_t, place_ref[...], (((0,), (0,)), ((), ())),
                             preferred_element_type=F32).astype(BF16)

    def branch(q_op, col0, masked, kt_lo, hh):
        macc_ref[...] = jnp.full((TQ, LANES), -3e38, F32)

        def scores(kt, carry):
            k0 = pl.multiple_of(kt * TQ, TQ)
            rhs = kv_ref[0, pl.ds(k0, TQ), col0:col0 + LANES]
            if masked:
                rhs = jnp.concatenate([rhs, ceneg_ref[pl.ds(k0, TQ), :]], axis=1)
                typ = jnp.where(kt >= qb - 1, kt - qb + 3, 0)
            else:
                typ = kt - qb + 3
            sv = _dot_nt(q_op, rhs) + b4_ref[hh, typ]
            sc_ref[kt] = sv
            macc_ref[...] = jnp.maximum(macc_ref[...],
                                        jnp.maximum(sv[:, :LANES], sv[:, LANES:]))
            return carry

        lax.fori_loop(kt_lo, qb + 1, scores, 0)
        m = jnp.max(macc_ref[...], axis=1, keepdims=True)
        oacc_ref[...] = jnp.zeros((TQ, 2 * LANES), F32)

        def values(kt, carry):
            k0 = pl.multiple_of(kt * TQ, TQ)
            p = jnp.exp(sc_ref[kt] - m).astype(BF16)
            rhs = jnp.concatenate([kv_ref[0, pl.ds(k0, TQ), col0:col0 + LANES], cones], axis=1)
            oacc_ref[...] += _dot(p, rhs)
            return carry

        lax.fori_loop(kt_lo, qb + 1, values, 0)
        return oacc_ref[...]

    def normalised(o, gate):
        l = o[:, LANES:LANES + 1]
        return o[:, :LANES] * (gate / jnp.where(l > 0.0, l, 1.0))

    for hh in range(HPG):
        q_sel = jnp.concatenate([q_ops[hh], q_mask], axis=1)
        o_s = branch(q_sel, 0, True, 0, hh)
        comb = normalised(o_s, gate_ref[0, 0, :, 3 * hh + 1:3 * hh + 2])
        o_w = branch(q_ops[hh], LANES, False, jnp.maximum(qb - 2, 0), hh)
        comb = comb + normalised(o_w, gate_ref[0, 0, :, 3 * hh + 2:3 * hh + 3])
        comb = comb + normalised(oc_ref[hh], gate_ref[0, 0, :, 3 * hh:3 * hh + 1])
        o_ref[0, :, hh * LANES:(hh + 1) * LANES] = jnp.where(low, 0.0, comb).astype(BF16)


def _attention(q, gates, kvr, kvcmp, rel_bias):
    b, s, hd = q.shape
    nq = s // TQ
    n_slc = s // SLC_LEN
    n_cmp = s // CMP_STRIDE
    b4, bc = _attn_tables(rel_bias, s)
    ceneg, cones, ov, place = _attn_consts(s)
    gw = 3 * HPG
    gates_g = (gates[:, :, :3 * N_HEADS].reshape(b, s, N_GROUPS, gw).transpose(0, 2, 1, 3))
    gl = HPG * HEAD_DIM
    return pl.pallas_call(
        _attn_kernel,
        grid=(b, N_GROUPS, nq),
        in_specs=[
            pl.BlockSpec((1, TQ, gl), lambda i, g, t: (i, t, g)),
            pl.BlockSpec((1, 1, TQ, gw), lambda i, g, t: (i, g, t, 0)),
            pl.BlockSpec((1, s, 4 * HEAD_DIM), lambda i, g, t: (i, 0, g)),
            pl.BlockSpec((1, 1, n_cmp, LANES), lambda i, g, t: (i, g, 0, 0)),
            pl.BlockSpec((HPG, 4, TQ, TQ), lambda i, g, t: (g, 0, 0, 0)),
            pl.BlockSpec((1, HPG, TQ, n_cmp), lambda i, g, t: (t, g, 0, 0)),
            pl.BlockSpec((s, LANES), lambda *_: (0, 0)),
            pl.BlockSpec((TQ, LANES), lambda *_: (0, 0)),
            pl.BlockSpec((n_slc, n_cmp), lambda *_: (0, 0)),
            pl.BlockSpec((n_slc, LANES), lambda *_: (0, 0)),
        ],
        out_specs=pl.BlockSpec((1, TQ, HPG * LANES), lambda i, g, t: (i, t, g)),
        out_shape=jax.ShapeDtypeStruct((b, s, N_HEADS * LANES), BF16),
        scratch_shapes=[
            pltpu.VMEM((nq, TQ, TQ), F32),
            pltpu.VMEM((TQ, LANES), F32),
            pltpu.VMEM((TQ, 2 * LANES), F32),
            pltpu.VMEM((HPG, TQ, 2 * LANES), F32),
        ],
        compiler_params=pltpu.CompilerParams(
            dimension_semantics=("arbitrary", "arbitrary", "arbitrary"),
            vmem_limit_bytes=VMEM_LIMIT),
        name="nsa_attn",
    )(q, gates_g, kvr, kvcmp, b4, bc, ceneg, cones, ov, place)


def _wo_padded(w_o):
    d = w_o.shape[1]
    w = w_o.reshape(N_HEADS, HEAD_DIM, d)
    return jnp.pad(w, ((0, 0), (HEAD_DIM, 0), (0, 0))).reshape(N_HEADS * LANES, d).astype(BF16)


def kernel(x, mix_norm, a_w_in, a_conv, a_w_out, ffn_norm, ffn_up, ffn_conv, ffn_down,
           kv_norm, w_kv, cmp_pe, cmp_w1, cmp_b1, cmp_w2, b_w_qg, b_w_o, rel_bias, final_norm):
    b, s, d = x.shape
    assert d == D_MODEL and s % 512 == 0 and mix_norm.shape[0] == 2
    x = _mixer(x, mix_norm[0], a_w_in[0], a_conv[0], a_w_out[0])
    x = _ffn(x, ffn_norm[0], ffn_up[0], ffn_conv[0], ffn_down[0])
    q, gates, kvc, kvr = _proj(x, mix_norm[1], kv_norm, b_w_qg[0], w_kv)
    kvcmp = _compress(kvc, cmp_pe, cmp_w1, cmp_b1, cmp_w2)
    attn = _attention(q, gates, kvr, kvcmp, rel_bias)
    return _ffn(x, ffn_norm[1], ffn_up[1], ffn_conv[1], ffn_down[1],
                attn=attn, w_o=_wo_padded(b_w_o[0]), final_g=final_norm)
```

```python
import functools
import math

import jax
import jax.numpy as jnp
import numpy as np
from jax import lax
from jax.experimental import pallas as pl
from jax.experimental.pallas import tpu as pltpu

D_MODEL = 1024
CONV_W = 3
D_FF = 2816
N_HEADS = 16
N_GROUPS = 4
HPG = N_HEADS // N_GROUPS
HEAD_DIM = 64
CMP_LEN = 32
CMP_STRIDE = 16
CMP_HIDDEN = 128
SLC_LEN = 64
N_SEL = 16
WINDOW = 512
N_BUCKETS = 32
MAX_DISTANCE = 128
EPS = 1e-6
NEG = -1e30

LANES = 128
V7X_VMEM_BYTES = 64 * 2**20
VMEM_LIMIT = V7X_VMEM_BYTES - 8 * 2**20

TQ = 256
CARRY_ROWS = 8
F32 = jnp.float32
BF16 = jnp.bfloat16


def _dot(a, b):
    return jnp.dot(a, b, preferred_element_type=F32)


def _dot_nt(a, b, precision=None):
    return lax.dot_general(a, b, (((1,), (1,)), ((), ())),
                           preferred_element_type=F32, precision=precision)


def _rms_scale(x):
    return x * lax.rsqrt(jnp.mean(x * x, axis=-1, keepdims=True) + EPS)


def _conv3(buf_ref, cw, rows):
    c = CARRY_ROWS
    return (cw[0:1, :] * buf_ref[c - 2:c - 2 + rows, :]
            + cw[1:2, :] * buf_ref[c - 1:c - 1 + rows, :]
            + cw[2:3, :] * buf_ref[c:c + rows, :])


def _mixer_kernel(x_ref, g_ref, win_ref, cw_ref, wout_ref, o_ref, ubuf_ref, *, tm):
    d = D_MODEL

    @pl.when(pl.program_id(1) == 0)
    def _():
        ubuf_ref[0:CARRY_ROWS, :] = jnp.zeros((CARRY_ROWS, d), F32)

    x = x_ref[0]
    h = (_rms_scale(x) * g_ref[...]).astype(BF16)
    cg = _dot(h, win_ref[:, d:2 * d])
    v = _dot(h, win_ref[:, 2 * d:3 * d])
    ubuf_ref[CARRY_ROWS:CARRY_ROWS + tm, :] = cg * v
    conv = _conv3(ubuf_ref, cw_ref[...], tm)
    ubuf_ref[0:CARRY_ROWS, :] = ubuf_ref[tm:tm + CARRY_ROWS, :]
    bg = _dot(h, win_ref[:, 0:d])
    y = (bg * conv).astype(BF16)
    o_ref[0] = x + _dot(y, wout_ref[...])


def _mixer(x, g, w_in, conv_w, w_out, *, tm=512):
    b, s, d = x.shape
    const = lambda *_: (0, 0)
    one = pl.Buffered(1)
    return pl.pallas_call(
        functools.partial(_mixer_kernel, tm=tm),
        grid=(b, s // tm),
        in_specs=[
            pl.BlockSpec((1, tm, d), lambda i, j: (i, j, 0)),
            pl.BlockSpec((1, d), const),
            pl.BlockSpec((d, 3 * d), const, pipeline_mode=one),
            pl.BlockSpec((CONV_W, d), const),
            pl.BlockSpec((d, d), const, pipeline_mode=one),
        ],
        out_specs=pl.BlockSpec((1, tm, d), lambda i, j: (i, j, 0)),
        out_shape=jax.ShapeDtypeStruct((b, s, d), F32),
        scratch_shapes=[pltpu.VMEM((CARRY_ROWS + tm, d), F32)],
        compiler_params=pltpu.CompilerParams(
            dimension_semantics=("arbitrary", "arbitrary"), vmem_limit_bytes=VMEM_LIMIT),
        name="mixer",
    )(x, g.reshape(1, d), w_in.astype(BF16), conv_w, w_out.astype(BF16))


def _ffn_kernel(*refs, tm, tf, has_attn, final_norm):
    refs = list(refs)
    x_ref = refs.pop(0)
    a_ref = refs.pop(0) if has_attn else None
    wo_ref = refs.pop(0) if has_attn else None
    g_ref, wup_ref, cw_ref, wdn_ref = refs[:4]
    refs = refs[4:]
    gf_ref = refs.pop(0) if final_norm else None
    o_ref, h_ref, ua_ref, ug_ref, carry_ref, acc_ref = refs

    @pl.when(pl.program_id(1) == 0)
    def _():
        carry_ref[...] = jnp.zeros(carry_ref.shape, F32)

    x = x_ref[0]
    if has_attn:
        x = x + _dot(a_ref[0], wo_ref[...])
    h_ref[...] = (_rms_scale(x) * g_ref[...]).astype(BF16)

    def up_conv(buf_ref, c0):
        cols = slice(c0, c0 + tf)
        buf_ref[0:CARRY_ROWS, :] = carry_ref[:, cols]
        buf_ref[CARRY_ROWS:CARRY_ROWS + tm, :] = _dot(h_ref[...], wup_ref[:, cols])
        carry_ref[:, cols] = buf_ref[tm:tm + CARRY_ROWS, :]
        return _conv3(buf_ref, cw_ref[:, cols], tm)

    for c in range(D_FF // tf):
        a = up_conv(ua_ref, c * tf)
        gt = up_conv(ug_ref, D_FF + c * tf)
        act = (a * jax.nn.sigmoid(a) * gt).astype(BF16)
        part = _dot(act, wdn_ref[c * tf:(c + 1) * tf, :])
        if c == 0:
            acc_ref[...] = x + part
        else:
            acc_ref[...] += part
    y = acc_ref[...]
    if final_norm:
        y = _rms_scale(y) * gf_ref[...]
    o_ref[0] = y


def _ffn(x, g, w_up, conv_w, w_down, *, attn=None, w_o=None, final_g=None, tm=512, tf=1408):
    b, s, d = x.shape
    has_attn = attn is not None
    final_norm = final_g is not None
    const = lambda *_: (0, 0)
    row = lambda i, j: (i, j, 0)
    one = pl.Buffered(1)
    args = [x]
    in_specs = [pl.BlockSpec((1, tm, d), row)]
    if has_attn:
        ka = attn.shape[-1]
        args += [attn, w_o]
        in_specs += [pl.BlockSpec((1, tm, ka), row),
                     pl.BlockSpec((ka, d), const, pipeline_mode=one)]
    args += [g.reshape(1, d), w_up.astype(BF16), conv_w, w_down.astype(BF16)]
    in_specs += [pl.BlockSpec((1, d), const),
                 pl.BlockSpec((d, 2 * D_FF), const, pipeline_mode=one),
                 pl.BlockSpec((CONV_W, 2 * D_FF), const),
                 pl.BlockSpec((D_FF, d), const, pipeline_mode=one)]
    if final_norm:
        args.append(final_g.reshape(1, d))
        in_specs.append(pl.BlockSpec((1, d), const))
    return pl.pallas_call(
        functools.partial(_ffn_kernel, tm=tm, tf=tf, has_attn=has_attn, final_norm=final_norm),
        grid=(b, s // tm),
        in_specs=in_specs,
        out_specs=pl.BlockSpec((1, tm, d), row),
        out_shape=jax.ShapeDtypeStruct((b, s, d), F32),
        scratch_shapes=[
            pltpu.VMEM((tm, d), BF16),
            pltpu.VMEM((CARRY_ROWS + tm, tf), F32),
            pltpu.VMEM((CARRY_ROWS + tm, tf), F32),
            pltpu.VMEM((CARRY_ROWS, 2 * D_FF), F32),
            pltpu.VMEM((tm, d), F32),
        ],
        compiler_params=pltpu.CompilerParams(
            dimension_semantics=("arbitrary", "arbitrary"), vmem_limit_bytes=VMEM_LIMIT),
        name="ffn_attn" if has_attn else "ffn",
    )(*args)


def _proj_kernel(x_ref, gm_ref, gk_ref, wq_ref, wg_ref, wkc_ref, wkr_ref,
                 q_ref, gate_ref, kvc_ref, kvr_ref):
    xn = _rms_scale(x_ref[0])
    hq = (xn * gm_ref[...]).astype(BF16)
    hs = (xn * gk_ref[...]).astype(BF16)
    q_ref[0] = (_dot(hq, wq_ref[...]) * (HEAD_DIM ** -0.5)).astype(BF16)
    gate_ref[0] = jax.nn.sigmoid(_dot(hq, wg_ref[...]))
    kvc_ref[0] = _dot(hs, wkc_ref[...])
    kvr_ref[0] = _dot(hs, wkr_ref[...]).astype(BF16)


def _proj(x, g_mix, g_kv, w_qg, w_kv, *, tm=512):
    b, s, d = x.shape
    hd = N_HEADS * HEAD_DIM
    gd = N_GROUPS * HEAD_DIM
    w_q = w_qg[:, :hd].astype(BF16)
    w_g = jnp.pad(w_qg[:, hd:], ((0, 0), (0, LANES - 3 * N_HEADS))).astype(BF16)
    w_kc = w_kv[:, :2 * gd].astype(BF16)
    w_kr = (w_kv[:, 2 * gd:].reshape(d, 4, N_GROUPS, HEAD_DIM)
            .transpose(0, 2, 1, 3).reshape(d, 4 * gd).astype(BF16))
    const = lambda *_: (0, 0)
    row = lambda i, j: (i, j, 0)
    one = pl.Buffered(1)
    return pl.pallas_call(
        _proj_kernel,
        grid=(b, s // tm),
        in_specs=[
            pl.BlockSpec((1, tm, d), row),
            pl.BlockSpec((1, d), const),
            pl.BlockSpec((1, d), const),
            pl.BlockSpec((d, hd), const, pipeline_mode=one),
            pl.BlockSpec((d, LANES), const, pipeline_mode=one),
            pl.BlockSpec((d, 2 * gd), const, pipeline_mode=one),
            pl.BlockSpec((d, 4 * gd), const, pipeline_mode=one),
        ],
        out_specs=[
            pl.BlockSpec((1, tm, hd), row),
            pl.BlockSpec((1, tm, LANES), row),
            pl.BlockSpec((1, tm, 2 * gd), row),
            pl.BlockSpec((1, tm, 4 * gd), row),
        ],
        out_shape=[
            jax.ShapeDtypeStruct((b, s, hd), BF16),
            jax.ShapeDtypeStruct((b, s, LANES), F32),
            jax.ShapeDtypeStruct((b, s, 2 * gd), F32),
            jax.ShapeDtypeStruct((b, s, 4 * gd), BF16),
        ],
        compiler_params=pltpu.CompilerParams(
            dimension_semantics=("arbitrary", "arbitrary"), vmem_limit_bytes=VMEM_LIMIT),
        name="proj",
    )(x, g_mix.reshape(1, d), g_kv.reshape(1, d), w_q, w_g, w_kc, w_kr)


def _gelu_tanh(x):
    return 0.5 * x * (1.0 + jnp.tanh(math.sqrt(2.0 / math.pi) * (x + 0.044715 * (x * x * x))))


def _compress_kernel(z_ref, pe_ref, w1_ref, b1_ref, w2_ref, o_ref):
    half = (CMP_LEN // 2) * HEAD_DIM
    n_chunks = z_ref.shape[3]
    acc = None
    for j in range(2):
        z = z_ref[0, j, 0]
        za = (z + pe_ref[j, 0:1, :]).astype(BF16)
        zb = (z + pe_ref[j, 1:2, :]).astype(BF16)
        h1 = _dot(za, w1_ref[j, 0:half, :])
        h2 = _dot(zb, w1_ref[j, half:2 * half, :])
        pre = h1 + pltpu.roll(h2, n_chunks - 1, 0) + b1_ref[j]
        part = _dot(_gelu_tanh(pre).astype(BF16), w2_ref[j])
        acc = part if acc is None else acc + part
    o_ref[0, 0] = acc.astype(BF16)


def _compress(kvc, cmp_pe, cmp_w1, cmp_b1, cmp_w2):
    b, s, _ = kvc.shape
    n_chunks = s // CMP_STRIDE
    half = (CMP_LEN // 2) * HEAD_DIM
    z = (kvc.reshape(b, n_chunks, CMP_STRIDE, 2, N_GROUPS, HEAD_DIM)
         .transpose(0, 3, 4, 1, 2, 5).reshape(b, 2, N_GROUPS, n_chunks, half))
    pe = cmp_pe.reshape(2, 2, half)
    w1 = cmp_w1.astype(BF16)
    b1 = cmp_b1.reshape(2, 1, CMP_HIDDEN)
    w2 = jnp.stack([jnp.pad(cmp_w2[0], ((0, 0), (0, HEAD_DIM))),
                    jnp.pad(cmp_w2[1], ((0, 0), (HEAD_DIM, 0)))]).astype(BF16)
    return pl.pallas_call(
        _compress_kernel,
        grid=(b, N_GROUPS),
        in_specs=[
            pl.BlockSpec((1, 2, 1, n_chunks, half), lambda i, g: (i, 0, g, 0, 0)),
            pl.BlockSpec((2, 2, half), lambda *_: (0, 0, 0)),
            pl.BlockSpec((2, 2 * half, CMP_HIDDEN), lambda *_: (0, 0, 0)),
            pl.BlockSpec((2, 1, CMP_HIDDEN), lambda *_: (0, 0, 0)),
            pl.BlockSpec((2, CMP_HIDDEN, LANES), lambda *_: (0, 0, 0)),
        ],
        out_specs=pl.BlockSpec((1, 1, n_chunks, LANES), lambda i, g: (i, g, 0, 0)),
        out_shape=jax.ShapeDtypeStruct((b, N_GROUPS, n_chunks, LANES), BF16),
        compiler_params=pltpu.CompilerParams(
            dimension_semantics=("arbitrary", "arbitrary"), vmem_limit_bytes=VMEM_LIMIT),
        name="compress",
    )(z, pe, w1, b1, w2)


TILE_PREV2, TILE_PREV, TILE_DIAG = 0, 1, 2
CMP_WIN = 32


def _rel_bucket_np(dist):
    max_exact = N_BUCKETS // 2
    d = np.maximum(dist, 0)
    df = np.maximum(d, 1).astype(np.float32)
    large = max_exact + (np.log(df / max_exact) / np.float32(math.log(MAX_DISTANCE / max_exact))
                         * (N_BUCKETS - max_exact)).astype(np.int32)
    return np.where(d < max_exact, d, np.minimum(large, N_BUCKETS - 1))


def _toeplitz(w):
    t = w.shape[-1] // 2
    lead = w.shape[:-1]
    a = jnp.broadcast_to(w[..., None, :], lead + (t, 2 * t)).reshape(lead + (2 * t * t,))
    return a[..., :t * (2 * t - 1)].reshape(lead + (t, 2 * t - 1))[..., :t]


def _attn_tables(rel_bias, s):
    nq = s // TQ
    assert _rel_bucket_np(np.arange(TQ // 2, 4 * TQ)).min() == N_BUCKETS - 1
    relb = (rel_bias.astype(F32) - rel_bias[N_BUCKETS - 1].astype(F32)).T
    f = jnp.take(relb, jnp.asarray(_rel_bucket_np(np.arange(2 * TQ))), axis=1)
    neg = jnp.full((N_HEADS, TQ), NEG, F32)
    w_diag = jnp.concatenate([f[:, 0:1], neg, jnp.flip(f[:, 1:TQ], axis=1)], axis=1)
    w_prev = jnp.concatenate([jnp.flip(f[:, 1:TQ + 1], axis=1), jnp.zeros((N_HEADS, 1), F32),
                              jnp.flip(f[:, TQ + 1:2 * TQ], axis=1)], axis=1)
    ti = np.arange(TQ)[:, None]
    ki = np.arange(TQ)[None, :]
    prev2 = jnp.broadcast_to(jnp.asarray(np.where(ki > ti, 0.0, NEG), F32), (N_HEADS, TQ, TQ))
    b3 = jnp.stack([prev2, _toeplitz(w_prev), _toeplitz(w_diag)], axis=1)
    b3 = (b3.reshape(N_GROUPS, HPG, 3, TQ, TQ).transpose(0, 2, 1, 3, 4)
          .reshape(N_GROUPS, 3, HPG * TQ, TQ))
    n_cmp = s // CMP_STRIDE
    lead = CMP_WIN // 4
    dist = ti - CMP_STRIDE * (np.arange(CMP_WIN)[None, :] - lead) - (CMP_LEN - 1)
    pw = jnp.take(relb, jnp.asarray(_rel_bucket_np(dist)), axis=1)
    pw = jnp.where(jnp.asarray(dist >= 0), pw, NEG)
    full = jnp.concatenate([jnp.zeros((N_HEADS, TQ, n_cmp - lead), F32), pw,
                            jnp.full((N_HEADS, TQ, n_cmp), NEG, F32)], axis=-1)
    per_tile = TQ // CMP_STRIDE
    bc = jnp.stack([full[..., n_cmp - per_tile * t:2 * n_cmp - per_tile * t] for t in range(nq)])
    bc = bc.reshape(nq, N_GROUPS, HPG * TQ, n_cmp)
    return b3, bc


def _attn_consts(s):
    n_slc = s // SLC_LEN
    n_cmp = s // CMP_STRIDE
    key = np.arange(s)[:, None]
    lane = np.arange(LANES)[None, :]
    ceneg = np.where((key // SLC_LEN == lane) & (lane < n_slc), NEG, 0.0)
    j = np.arange(n_slc)[:, None]
    i = np.arange(n_cmp)[None, :]
    ov = ((i * CMP_STRIDE < j * SLC_LEN + SLC_LEN) & (i * CMP_STRIDE + CMP_LEN > j * SLC_LEN)
          & (i < n_cmp - 1))
    place = np.eye(n_slc, LANES)
    return (jnp.asarray(ceneg, BF16), jnp.ones((TQ, LANES), BF16),
            jnp.asarray(ov, F32), jnp.asarray(place, BF16))


def _attn_kernel(q_ref, gate_ref, kv_ref, kvc_ref, b3_ref, bc_ref, ceneg_ref, ones_ref,
                 ov_ref, place_ref, o_ref, q4_ref, sc_ref, macc_ref, mb_ref, oacc_ref, oc_ref, os_ref):
    qb = pl.program_id(2)
    n_slc = ov_ref.shape[0]
    rows = HPG * TQ
    lane = lax.broadcasted_iota(jnp.int32, (TQ, LANES), 1)
    low = lane < HEAD_DIM

    for pair in range(HPG // 2):
        qp = q_ref[0, :, pair * LANES:(pair + 1) * LANES].astype(F32)
        q4_ref[(2 * pair) * TQ:(2 * pair + 1) * TQ, 0:LANES] = jnp.where(low, qp, 0.0).astype(BF16)
        q4_ref[(2 * pair + 1) * TQ:(2 * pair + 2) * TQ, 0:LANES] = (
            jnp.where(low, pltpu.roll(qp, HEAD_DIM, 1), 0.0).astype(BF16))
    q4 = q4_ref[:, 0:LANES]
    ones = ones_ref[...]

    kvc = kvc_ref[0, 0]
    bias_c = bc_ref[0, 0]
    s_c = _dot_nt(q4, kvc) + bias_c
    m_c = jnp.max(s_c, axis=1, keepdims=True)
    e_c = jnp.where(bias_c > 0.5 * NEG, jnp.exp(s_c - m_c), 0.0)
    e_cb = e_c.astype(BF16)
    oc_ref[...] = _dot(e_cb, jnp.concatenate([kvc, ones[0:kvc.shape[0]]], axis=1))
    l_c = jnp.sum(e_c, axis=1, keepdims=True)
    p_c = e_c * (1.0 / jnp.where(l_c > 0.0, l_c, 1.0))
    psum = p_c[0:TQ]
    for hh in range(1, HPG):
        psum = psum + p_c[hh * TQ:(hh + 1) * TQ]

    imp_t = _dot_nt(ov_ref[...], psum, precision=lax.Precision.HIGHEST)
    jrow = lax.broadcasted_iota(jnp.int32, (n_slc, TQ), 0)
    tpos = qb * TQ + lax.broadcasted_iota(jnp.int32, (n_slc, TQ), 1)
    cur = lax.shift_right_logical(tpos, int(math.log2(SLC_LEN)))
    forced = (jrow == 0) | (jrow == cur) | (jrow == cur - 1)
    score = jnp.where(forced, 3e38, jnp.where(jrow <= cur, imp_t, -1.0))
    rank = jnp.zeros((n_slc, TQ), F32)
    for k in range(n_slc):
        sk = score[k:k + 1, :]
        beats = (sk > score) | ((sk == score) & (jrow > k))
        rank = rank + beats.astype(F32)
    notsel_t = (rank >= float(N_SEL)).astype(BF16)
    q_mask = lax.dot_general(notsel_t, place_ref[...], (((0,), (0,)), ((), ())),
                             preferred_element_type=F32).astype(BF16)
    for hh in range(HPG):
        q4_ref[hh * TQ:(hh + 1) * TQ, LANES:2 * LANES] = q_mask

    def key_tile(kt, col0):
        k0 = pl.multiple_of(kt * TQ, TQ)
        return kv_ref[0, pl.ds(k0, TQ), col0:col0 + LANES]

    def branch(col0, masked, kt_lo):
        macc_ref[...] = jnp.full((rows, LANES), -3e38, F32)
        q_op = q4_ref[...] if masked else q4

        def score_tile(kt, kind):
            rhs = key_tile(kt, col0)
            if masked:
                k0 = pl.multiple_of(kt * TQ, TQ)
                rhs = jnp.concatenate([rhs, ceneg_ref[pl.ds(k0, TQ), :]], axis=1)
            sv = _dot_nt(q_op, rhs)
            if kind is not None:
                sv = sv + b3_ref[0, kind]
            sc_ref[kt] = sv
            macc_ref[...] = jnp.maximum(macc_ref[...], jnp.maximum(sv[:, :LANES], sv[:, LANES:]))

        if masked:
            def far(kt, carry):
                score_tile(kt, None)
                return carry
            lax.fori_loop(0, qb - 1, far, 0)
        else:
            @pl.when(qb >= 2)
            def _():
                score_tile(qb - 2, TILE_PREV2)

        @pl.when(qb >= 1)
        def _():
            score_tile(qb - 1, TILE_PREV)

        score_tile(qb, TILE_DIAG)

        m = jnp.max(macc_ref[...], axis=1, keepdims=True)
        mb_ref[...] = jnp.broadcast_to(m, (rows, LANES))
        oacc_ref[...] = jnp.zeros((rows, 2 * LANES), F32)

        def values(kt, carry):
            sv = sc_ref[kt]
            mb = mb_ref[...]
            p = jnp.concatenate([jnp.exp(sv[:, :LANES] - mb), jnp.exp(sv[:, LANES:] - mb)],
                                axis=1).astype(BF16)
            oacc_ref[...] += _dot(p, jnp.concatenate([key_tile(kt, col0), ones], axis=1))
            return carry

        lax.fori_loop(kt_lo, qb + 1, values, 0)

    branch(0, True, 0)
    os_ref[...] = oacc_ref[...]
    branch(LANES, False, jnp.maximum(qb - 2, 0))

    for hh in range(HPG):
        r = slice(hh * TQ, (hh + 1) * TQ)
        comb = None
        for br, o_br in enumerate((oc_ref, os_ref, oacc_ref)):
            l = o_br[r, LANES:2 * LANES]
            gate = gate_ref[0, 0, :, 3 * hh + br:3 * hh + br + 1]
            term = o_br[r, 0:LANES] * (gate / jnp.where(l > 0.0, l, 1.0))
            comb = term if comb is None else comb + term
        o_ref[0, :, hh * LANES:(hh + 1) * LANES] = jnp.where(low, 0.0, comb).astype(BF16)


def _attention(q, gates, kvr, kvcmp, rel_bias):
    b, s, hd = q.shape
    nq = s // TQ
    n_slc = s // SLC_LEN
    n_cmp = s // CMP_STRIDE
    rows = HPG * TQ
    b3, bc = _attn_tables(rel_bias, s)
    ceneg, ones, ov, place = _attn_consts(s)
    gw = 3 * HPG
    gates_g = (gates[:, :, :3 * N_HEADS].reshape(b, s, N_GROUPS, gw).transpose(0, 2, 1, 3))
    gl = HPG * HEAD_DIM
    return pl.pallas_call(
        _attn_kernel,
        grid=(b, N_GROUPS, nq),
        in_specs=[
            pl.BlockSpec((1, TQ, gl), lambda i, g, t: (i, t, g)),
            pl.BlockSpec((1, 1, TQ, gw), lambda i, g, t: (i, g, t, 0)),
            pl.BlockSpec((1, s, 4 * HEAD_DIM), lambda i, g, t: (i, 0, g)),
            pl.BlockSpec((1, 1, n_cmp, LANES), lambda i, g, t: (i, g, 0, 0)),
            pl.BlockSpec((1, 3, rows, TQ), lambda i, g, t: (g, 0, 0, 0)),
            pl.BlockSpec((1, 1, rows, n_cmp), lambda i, g, t: (t, g, 0, 0)),
            pl.BlockSpec((s, LANES), lambda *_: (0, 0)),
            pl.BlockSpec((TQ, LANES), lambda *_: (0, 0)),
            pl.BlockSpec((n_slc, n_cmp), lambda *_: (0, 0)),
            pl.BlockSpec((n_slc, LANES), lambda *_: (0, 0)),
        ],
        out_specs=pl.BlockSpec((1, TQ, HPG * LANES), lambda i, g, t: (i, t, g)),
        out_shape=jax.ShapeDtypeStruct((b, s, N_HEADS * LANES), BF16),
        scratch_shapes=[
            pltpu.VMEM((rows, 2 * LANES), BF16),
            pltpu.VMEM((nq, rows, TQ), F32),
            pltpu.VMEM((rows, LANES), F32),
            pltpu.VMEM((rows, LANES), F32),
            pltpu.VMEM((rows, 2 * LANES), F32),
            pltpu.VMEM((rows, 2 * LANES), F32),
            pltpu.VMEM((rows, 2 * LANES), F32),
        ],
        compiler_params=pltpu.CompilerParams(
            dimension_semantics=("arbitrary", "arbitrary", "arbitrary"),
            vmem_limit_bytes=VMEM_LIMIT),
        name="nsa_attn",
    )(q, gates_g, kvr, kvcmp, b3, bc, ceneg, ones, ov, place)


def _wo_padded(w_o):
    d = w_o.shape[1]
    w = w_o.reshape(N_HEADS, HEAD_DIM, d)
    return jnp.pad(w, ((0, 0), (HEAD_DIM, 0), (0, 0))).reshape(N_HEADS * LANES, d).astype(BF16)


def kernel(x, mix_norm, a_w_in, a_conv, a_w_out, ffn_norm, ffn_up, ffn_conv, ffn_down,
           kv_norm, w_kv, cmp_pe, cmp_w1, cmp_b1, cmp_w2, b_w_qg, b_w_o, rel_bias, final_norm):
    b, s, d = x.shape
    assert d == D_MODEL and s % 512 == 0 and mix_norm.shape[0] == 2
    x = _mixer(x, mix_norm[0], a_w_in[0], a_conv[0], a_w_out[0])
    x = _ffn(x, ffn_norm[0], ffn_up[0], ffn_conv[0], ffn_down[0])
    q, gates, kvc, kvr = _proj(x, mix_norm[1], kv_norm, b_w_qg[0], w_kv)
    kvcmp = _compress(kvc, cmp_pe, cmp_w1, cmp_b1, cmp_w2)
    attn = _attention(q, gates, kvr, kvcmp, rel_bias)
    return _ffn(x, ffn_norm[1], ffn_up[1], ffn_conv[1], ffn_down[1],
                attn=attn, w_o=_wo_padded(b_w_o[0]), final_g=final_norm)
```

```python
import functools
import math

import jax
import jax.numpy as jnp
import numpy as np
from jax import lax
from jax.experimental import pallas as pl
from jax.experimental.pallas import tpu as pltpu

D_MODEL = 1024
CONV_W = 3
D_FF = 2816
N_HEADS = 16
N_GROUPS = 4
HPG = N_HEADS // N_GROUPS
HEAD_DIM = 64
CMP_LEN = 32
CMP_STRIDE = 16
CMP_HIDDEN = 128
SLC_LEN = 64
N_SEL = 16
WINDOW = 512
N_BUCKETS = 32
MAX_DISTANCE = 128
EPS = 1e-6
NEG = -1e30

LANES = 128
V7X_VMEM_BYTES = 64 * 2**20
VMEM_LIMIT = V7X_VMEM_BYTES - 8 * 2**20

TQ = 256
CARRY_ROWS = 8
F32 = jnp.float32
BF16 = jnp.bfloat16


def _dot(a, b):
    return jnp.dot(a, b, preferred_element_type=F32)


def _dot_nt(a, b, precision=None):
    return lax.dot_general(a, b, (((1,), (1,)), ((), ())),
                           preferred_element_type=F32, precision=precision)


def _rms_scale(x):
    return x * lax.rsqrt(jnp.mean(x * x, axis=-1, keepdims=True) + EPS)


def _conv3(buf_ref, cw, rows):
    c = CARRY_ROWS
    return (cw[0:1, :] * buf_ref[c - 2:c - 2 + rows, :]
            + cw[1:2, :] * buf_ref[c - 1:c - 1 + rows, :]
            + cw[2:3, :] * buf_ref[c:c + rows, :])


def _mixer_kernel(x_ref, g_ref, win_ref, cw_ref, wout_ref, o_ref, ubuf_ref, *, tm):
    d = D_MODEL

    @pl.when(pl.program_id(1) == 0)
    def _():
        ubuf_ref[0:CARRY_ROWS, :] = jnp.zeros((CARRY_ROWS, d), F32)

    x = x_ref[0]
    h = (_rms_scale(x) * g_ref[...]).astype(BF16)
    cg = _dot(h, win_ref[:, d:2 * d])
    v = _dot(h, win_ref[:, 2 * d:3 * d])
    ubuf_ref[CARRY_ROWS:CARRY_ROWS + tm, :] = cg * v
    conv = _conv3(ubuf_ref, cw_ref[...], tm)
    ubuf_ref[0:CARRY_ROWS, :] = ubuf_ref[tm:tm + CARRY_ROWS, :]
    bg = _dot(h, win_ref[:, 0:d])
    y = (bg * conv).astype(BF16)
    o_ref[0] = x + _dot(y, wout_ref[...])


def _mixer(x, g, w_in, conv_w, w_out, *, tm=512):
    b, s, d = x.shape
    const = lambda *_: (0, 0)
    one = pl.Buffered(1)
    return pl.pallas_call(
        functools.partial(_mixer_kernel, tm=tm),
        grid=(b, s // tm),
        in_specs=[
            pl.BlockSpec((1, tm, d), lambda i, j: (i, j, 0)),
            pl.BlockSpec((1, d), const),
            pl.BlockSpec((d, 3 * d), const, pipeline_mode=one),
            pl.BlockSpec((CONV_W, d), const),
            pl.BlockSpec((d, d), const, pipeline_mode=one),
        ],
        out_specs=pl.BlockSpec((1, tm, d), lambda i, j: (i, j, 0)),
        out_shape=jax.ShapeDtypeStruct((b, s, d), F32),
        scratch_shapes=[pltpu.VMEM((CARRY_ROWS + tm, d), F32)],
        compiler_params=pltpu.CompilerParams(
            dimension_semantics=("arbitrary", "arbitrary"), vmem_limit_bytes=VMEM_LIMIT),
        name="mixer",
    )(x, g.reshape(1, d), w_in.astype(BF16), conv_w, w_out.astype(BF16))


def _ffn_kernel(*refs, tm, tf, has_attn, final_norm):
    refs = list(refs)
    x_ref = refs.pop(0)
    a_ref = refs.pop(0) if has_attn else None
    wo_ref = refs.pop(0) if has_attn else None
    g_ref, wup_ref, cw_ref, wdn_ref = refs[:4]
    refs = refs[4:]
    gf_ref = refs.pop(0) if final_norm else None
    o_ref, h_ref, ua_ref, ug_ref, carry_ref, acc_ref = refs

    @pl.when(pl.program_id(1) == 0)
    def _():
        carry_ref[...] = jnp.zeros(carry_ref.shape, F32)

    x = x_ref[0]
    if has_attn:
        x = x + _dot(a_ref[0], wo_ref[...])
    h_ref[...] = (_rms_scale(x) * g_ref[...]).astype(BF16)

    def up_conv(buf_ref, c0):
        cols = slice(c0, c0 + tf)
        buf_ref[0:CARRY_ROWS, :] = carry_ref[:, cols]
        buf_ref[CARRY_ROWS:CARRY_ROWS + tm, :] = _dot(h_ref[...], wup_ref[:, cols])
        carry_ref[:, cols] = buf_ref[tm:tm + CARRY_ROWS, :]
        return _conv3(buf_ref, cw_ref[:, cols], tm)

    for c in range(D_FF // tf):
        a = up_conv(ua_ref, c * tf)
        gt = up_conv(ug_ref, D_FF + c * tf)
        act = (a * jax.nn.sigmoid(a) * gt).astype(BF16)
        part = _dot(act, wdn_ref[c * tf:(c + 1) * tf, :])
        if c == 0:
            acc_ref[...] = x + part
        else:
            acc_ref[...] += part
    y = acc_ref[...]
    if final_norm:
        y = _rms_scale(y) * gf_ref[...]
    o_ref[0] = y


def _ffn(x, g, w_up, conv_w, w_down, *, attn=None, w_o=None, final_g=None, tm=512, tf=1408):
    b, s, d = x.shape
    has_attn = attn is not None
    final_norm = final_g is not None
    const = lambda *_: (0, 0)
    row = lambda i, j: (i, j, 0)
    one = pl.Buffered(1)
    args = [x]
    in_specs = [pl.BlockSpec((1, tm, d), row)]
    if has_attn:
        ka = attn.shape[-1]
        args += [attn, w_o]
        in_specs += [pl.BlockSpec((1, tm, ka), row),
                     pl.BlockSpec((ka, d), const, pipeline_mode=one)]
    args += [g.reshape(1, d), w_up.astype(BF16), conv_w, w_down.astype(BF16)]
    in_specs += [pl.BlockSpec((1, d), const),
                 pl.BlockSpec((d, 2 * D_FF), const, pipeline_mode=one),
                 pl.BlockSpec((CONV_W, 2 * D_FF), const),
                 pl.BlockSpec((D_FF, d), const, pipeline_mode=one)]
    if final_norm:
        args.append(final_g.reshape(1, d))
        in_specs.append(pl.BlockSpec((1, d), const))
    return pl.pallas_call(
        functools.partial(_ffn_kernel, tm=tm, tf=tf, has_attn=has_attn, final_norm=final_norm),
        grid=(b, s // tm),
        in_specs=in_specs,
        out_specs=pl.BlockSpec((1, tm, d), row),
        out_shape=jax.ShapeDtypeStruct((b, s, d), F32),
        scratch_shapes=[
            pltpu.VMEM((tm, d), BF16),
            pltpu.VMEM((CARRY_ROWS + tm, tf), F32),
            pltpu.VMEM((CARRY_ROWS + tm, tf), F32),
            pltpu.VMEM((CARRY_ROWS, 2 * D_FF), F32),
            pltpu.VMEM((tm, d), F32),
        ],
        compiler_params=pltpu.CompilerParams(
            dimension_semantics=("arbitrary", "arbitrary"), vmem_limit_bytes=VMEM_LIMIT),
        name="ffn_attn" if has_attn else "ffn",
    )(*args)


def _proj_kernel(x_ref, gm_ref, gk_ref, wq_ref, wg_ref, wkc_ref, wkr_ref,
                 q_ref, gate_ref, kvc_ref, kvr_ref):
    xn = _rms_scale(x_ref[0])
    hq = (xn * gm_ref[...]).astype(BF16)
    hs = (xn * gk_ref[...]).astype(BF16)
    q_ref[0] = (_dot(hq, wq_ref[...]) * (HEAD_DIM ** -0.5)).astype(BF16)
    gate_ref[0] = jax.nn.sigmoid(_dot(hq, wg_ref[...]))
    kvc_ref[0] = _dot(hs, wkc_ref[...])
    kvr_ref[0] = _dot(hs, wkr_ref[...]).astype(BF16)


def _proj(x, g_mix, g_kv, w_qg, w_kv, *, tm=512):
    b, s, d = x.shape
    hd = N_HEADS * HEAD_DIM
    gd = N_GROUPS * HEAD_DIM
    w_q = w_qg[:, :hd].astype(BF16)
    w_g = jnp.pad(w_qg[:, hd:], ((0, 0), (0, LANES - 3 * N_HEADS))).astype(BF16)
    w_kc = w_kv[:, :2 * gd].astype(BF16)
    w_kr = (w_kv[:, 2 * gd:].reshape(d, 4, N_GROUPS, HEAD_DIM)
            .transpose(0, 2, 1, 3).reshape(d, 4 * gd).astype(BF16))
    const = lambda *_: (0, 0)
    row = lambda i, j: (i, j, 0)
    one = pl.Buffered(1)
    return pl.pallas_call(
        _proj_kernel,
        grid=(b, s // tm),
        in_specs=[
            pl.BlockSpec((1, tm, d), row),
            pl.BlockSpec((1, d), const),
            pl.BlockSpec((1, d), const),
            pl.BlockSpec((d, hd), const, pipeline_mode=one),
            pl.BlockSpec((d, LANES), const, pipeline_mode=one),
            pl.BlockSpec((d, 2 * gd), const, pipeline_mode=one),
            pl.BlockSpec((d, 4 * gd), const, pipeline_mode=one),
        ],
        out_specs=[
            pl.BlockSpec((1, tm, hd), row),
            pl.BlockSpec((1, tm, LANES), row),
            pl.BlockSpec((1, tm, 2 * gd), row),
            pl.BlockSpec((1, tm, 4 * gd), row),
        ],
        out_shape=[
            jax.ShapeDtypeStruct((b, s, hd), BF16),
            jax.ShapeDtypeStruct((b, s, LANES), F32),
            jax.ShapeDtypeStruct((b, s, 2 * gd), F32),
            jax.ShapeDtypeStruct((b, s, 4 * gd), BF16),
        ],
        compiler_params=pltpu.CompilerParams(
            dimension_semantics=("arbitrary", "arbitrary"), vmem_limit_bytes=VMEM_LIMIT),
        name="proj",
    )(x, g_mix.reshape(1, d), g_kv.reshape(1, d), w_q, w_g, w_kc, w_kr)


def _gelu_tanh(x):
    return 0.5 * x * (1.0 + jnp.tanh(math.sqrt(2.0 / math.pi) * (x + 0.044715 * (x * x * x))))


def _compress_kernel(z_ref, pe_ref, w1_ref, b1_ref, w2_ref, o_ref):
    half = (CMP_LEN // 2) * HEAD_DIM
    n_chunks = z_ref.shape[3]
    acc = None
    for j in range(2):
        z = z_ref[0, j, 0]
        za = (z + pe_ref[j, 0:1, :]).astype(BF16)
        zb = (z + pe_ref[j, 1:2, :]).astype(BF16)
        h1 = _dot(za, w1_ref[j, 0:half, :])
        h2 = _dot(zb, w1_ref[j, half:2 * half, :])
        pre = h1 + pltpu.roll(h2, n_chunks - 1, 0) + b1_ref[j]
        part = _dot(_gelu_tanh(pre).astype(BF16), w2_ref[j])
        acc = part if acc is None else acc + part
    o_ref[0, 0] = acc.astype(BF16)


def _compress(kvc, cmp_pe, cmp_w1, cmp_b1, cmp_w2):
    b, s, _ = kvc.shape
    n_chunks = s // CMP_STRIDE
    half = (CMP_LEN // 2) * HEAD_DIM
    z = (kvc.reshape(b, n_chunks, CMP_STRIDE, 2, N_GROUPS, HEAD_DIM)
         .transpose(0, 3, 4, 1, 2, 5).reshape(b, 2, N_GROUPS, n_chunks, half))
    pe = cmp_pe.reshape(2, 2, half)
    w1 = cmp_w1.astype(BF16)
    b1 = cmp_b1.reshape(2, 1, CMP_HIDDEN)
    w2 = jnp.stack([jnp.pad(cmp_w2[0], ((0, 0), (0, HEAD_DIM))),
                    jnp.pad(cmp_w2[1], ((0, 0), (HEAD_DIM, 0)))]).astype(BF16)
    return pl.pallas_call(
        _compress_kernel,
        grid=(b, N_GROUPS),
        in_specs=[
            pl.BlockSpec((1, 2, 1, n_chunks, half), lambda i, g: (i, 0, g, 0, 0)),
            pl.BlockSpec((2, 2, half), lambda *_: (0, 0, 0)),
            pl.BlockSpec((2, 2 * half, CMP_HIDDEN), lambda *_: (0, 0, 0)),
            pl.BlockSpec((2, 1, CMP_HIDDEN), lambda *_: (0, 0, 0)),
            pl.BlockSpec((2, CMP_HIDDEN, LANES), lambda *_: (0, 0, 0)),
        ],
        out_specs=pl.BlockSpec((1, 1, n_chunks, LANES), lambda i, g: (i, g, 0, 0)),
        out_shape=jax.ShapeDtypeStruct((b, N_GROUPS, n_chunks, LANES), BF16),
        compiler_params=pltpu.CompilerParams(
            dimension_semantics=("arbitrary", "arbitrary"), vmem_limit_bytes=VMEM_LIMIT),
        name="compress",
    )(z, pe, w1, b1, w2)


TILE_HIDDEN, TILE_PREV2, TILE_PREV, TILE_DIAG = 0, 1, 2, 3
N_KINDS = 4
CMP_WIN = 32
LOG2E = 1.0 / math.log(2.0)


def _rel_bucket_np(dist):
    max_exact = N_BUCKETS // 2
    d = np.maximum(dist, 0)
    df = np.maximum(d, 1).astype(np.float32)
    large = max_exact + (np.log(df / max_exact) / np.float32(math.log(MAX_DISTANCE / max_exact))
                         * (N_BUCKETS - max_exact)).astype(np.int32)
    return np.where(d < max_exact, d, np.minimum(large, N_BUCKETS - 1))


def _toeplitz(w):
    t = w.shape[-1] // 2
    lead = w.shape[:-1]
    a = jnp.broadcast_to(w[..., None, :], lead + (t, 2 * t)).reshape(lead + (2 * t * t,))
    return a[..., :t * (2 * t - 1)].reshape(lead + (t, 2 * t - 1))[..., :t]


def _attn_tables(rel_bias, s):
    nq = s // TQ
    assert _rel_bucket_np(np.arange(TQ // 2, 4 * TQ)).min() == N_BUCKETS - 1
    relb = (rel_bias.astype(F32) - rel_bias[N_BUCKETS - 1].astype(F32)).T
    f = jnp.take(relb, jnp.asarray(_rel_bucket_np(np.arange(2 * TQ))), axis=1)
    neg = jnp.full((N_HEADS, TQ), NEG, F32)
    w_diag = jnp.concatenate([f[:, 0:1], neg, jnp.flip(f[:, 1:TQ], axis=1)], axis=1)
    w_prev = jnp.concatenate([jnp.flip(f[:, 1:TQ + 1], axis=1), jnp.zeros((N_HEADS, 1), F32),
                              jnp.flip(f[:, TQ + 1:2 * TQ], axis=1)], axis=1)
    ti = np.arange(TQ)[:, None]
    ki = np.arange(TQ)[None, :]
    prev2 = jnp.broadcast_to(jnp.asarray(np.where(ki > ti, 0.0, NEG), F32), (N_HEADS, TQ, TQ))
    hidden = jnp.full((N_HEADS, TQ, TQ), NEG, F32)
    b3 = jnp.stack([hidden, prev2, _toeplitz(w_prev), _toeplitz(w_diag)], axis=1)
    b3 = (b3.reshape(N_GROUPS, HPG, N_KINDS, TQ, TQ).transpose(0, 2, 1, 3, 4)
          .reshape(N_GROUPS, N_KINDS, HPG * TQ, TQ))
    n_cmp = s // CMP_STRIDE
    lead = CMP_WIN // 4
    dist = ti - CMP_STRIDE * (np.arange(CMP_WIN)[None, :] - lead) - (CMP_LEN - 1)
    pw = jnp.take(relb, jnp.asarray(_rel_bucket_np(dist)), axis=1)
    pw = jnp.where(jnp.asarray(dist >= 0), pw, NEG)
    full = jnp.concatenate([jnp.zeros((N_HEADS, TQ, n_cmp - lead), F32), pw,
                            jnp.full((N_HEADS, TQ, n_cmp), NEG, F32)], axis=-1)
    per_tile = TQ // CMP_STRIDE
    bc = jnp.stack([full[..., n_cmp - per_tile * t:2 * n_cmp - per_tile * t] for t in range(nq)])
    bc = bc.reshape(nq, N_GROUPS, HPG * TQ, n_cmp)
    return b3, bc


def _attn_consts(s):
    n_slc = s // SLC_LEN
    n_cmp = s // CMP_STRIDE
    key = np.arange(s)[:, None]
    lane = np.arange(LANES)[None, :]
    ceneg = np.where((key // SLC_LEN == lane) & (lane < n_slc), NEG, 0.0)
    j = np.arange(n_slc)[:, None]
    i = np.arange(n_cmp)[None, :]
    ov = ((i * CMP_STRIDE < j * SLC_LEN + SLC_LEN) & (i * CMP_STRIDE + CMP_LEN > j * SLC_LEN)
          & (i < n_cmp - 1))
    place = np.eye(n_slc, LANES)
    return (jnp.asarray(ceneg, BF16), jnp.ones((3 * TQ, LANES), BF16),
            jnp.asarray(ov, F32), jnp.asarray(place, BF16))


def _attn_kernel(q_ref, gate_ref, kv_ref, kvc_ref, b3_ref, bc_ref, ceneg_ref, ones_ref,
                 ov_ref, place_ref, o_ref, q4_ref, sc_ref, nsc_ref, macc_ref, mb_ref, oacc_ref,
                 oc_ref, ow_ref):
    qb = pl.program_id(2)
    n_slc = ov_ref.shape[0]
    rows = HPG * TQ
    lane = lax.broadcasted_iota(jnp.int32, (TQ, LANES), 1)
    low = lane < HEAD_DIM

    for pair in range(HPG // 2):
        qp = q_ref[0, :, pair * LANES:(pair + 1) * LANES].astype(F32)
        q4_ref[(2 * pair) * TQ:(2 * pair + 1) * TQ, 0:LANES] = jnp.where(low, qp, 0.0).astype(BF16)
        q4_ref[(2 * pair + 1) * TQ:(2 * pair + 2) * TQ, 0:LANES] = (
            jnp.where(low, pltpu.roll(qp, HEAD_DIM, 1), 0.0).astype(BF16))
    q4 = q4_ref[:, 0:LANES]
    ones = ones_ref[...]

    def key_rows(kt, col0):
        k0 = pl.multiple_of(kt * TQ, TQ)
        return kv_ref[0, pl.ds(k0, TQ), col0:col0 + LANES]

    def lane_tiles(x):
        return [x[:, i:i + LANES] for i in range(0, x.shape[1], LANES)]

    def tile_max(x):
        return functools.reduce(jnp.maximum, lane_tiles(x))

    def exp2_rel(sv, mb):
        return jnp.concatenate([jnp.exp2(t - mb) for t in lane_tiles(sv)], axis=1).astype(BF16)

    kt1 = jnp.maximum(qb - 1, 0)
    kt2 = jnp.maximum(qb - 2, 0)
    kind1 = jnp.where(qb >= 1, TILE_PREV, TILE_HIDDEN)
    kind2 = jnp.where(qb >= 2, TILE_PREV2, TILE_HIDDEN)

    rhs_w = jnp.concatenate([key_rows(kt2, LANES), key_rows(kt1, LANES), key_rows(qb, LANES)], axis=0)
    bias_w = jnp.concatenate([b3_ref[0, kind2], b3_ref[0, kind1], b3_ref[0, TILE_DIAG]], axis=1)
    s_w = (_dot_nt(q4, rhs_w) + bias_w) * LOG2E
    m_w = jnp.max(tile_max(s_w), axis=1, keepdims=True)
    p_w = exp2_rel(s_w, jnp.broadcast_to(m_w, (rows, LANES)))
    ow_ref[...] = _dot(p_w, jnp.concatenate([rhs_w, ones], axis=1))

    kvc = kvc_ref[0, 0]
    bias_c = bc_ref[0, 0]
    s_c = _dot_nt(q4, kvc) + bias_c
    m_c = jnp.max(s_c, axis=1, keepdims=True)
    e_c = jnp.where(bias_c > 0.5 * NEG, jnp.exp(s_c - m_c), 0.0)
    o_c = _dot(e_c.astype(BF16), jnp.concatenate([kvc, ones[0:kvc.shape[0]]], axis=1))
    oc_ref[...] = o_c
    l_c = o_c[:, LANES:2 * LANES]
    p_c = e_c / jnp.where(l_c > 0.0, l_c, 1.0)
    psum = p_c[0:TQ]
    for hh in range(1, HPG):
        psum = psum + p_c[hh * TQ:(hh + 1) * TQ]

    imp_t = _dot_nt(ov_ref[...], psum, precision=lax.Precision.HIGHEST)
    jrow = lax.broadcasted_iota(jnp.int32, (n_slc, TQ), 0)
    tpos = qb * TQ + lax.broadcasted_iota(jnp.int32, (n_slc, TQ), 1)
    cur = lax.shift_right_logical(tpos, int(math.log2(SLC_LEN)))
    forced = (jrow == 0) | (jrow == cur) | (jrow == cur - 1)
    score = jnp.where(forced, 3e38, jnp.where(jrow <= cur, imp_t, -1.0))
    rank = jnp.zeros((n_slc, TQ), F32)
    for k in range(n_slc):
        sk = score[k:k + 1, :]
        beats = (sk > score) | ((sk == score) & (jrow > k))
        rank = rank + beats.astype(F32)
    notsel_t = (rank >= float(N_SEL)).astype(BF16)
    q_mask = lax.dot_general(notsel_t, place_ref[...], (((0,), (0,)), ((), ())),
                             preferred_element_type=F32).astype(BF16)
    for hh in range(HPG):
        q4_ref[hh * TQ:(hh + 1) * TQ, LANES:2 * LANES] = q_mask

    n_far = jnp.maximum(qb - 1, 0)
    n_pairs = lax.shift_right_logical(n_far + 1, 1)
    pair = 2 * TQ
    macc_ref[...] = jnp.full((rows, LANES), -3e38, F32)

    def far_scores(p, carry):
        k0 = pl.multiple_of(p * pair, pair)
        rhs = jnp.concatenate([kv_ref[0, pl.ds(k0, pair), 0:LANES], ceneg_ref[pl.ds(k0, pair), :]],
                              axis=1)
        sv = _dot_nt(q4_ref[...], rhs) * LOG2E
        hide = jnp.where(2 * p + 1 < n_far, 0.0, NEG)
        sv = jnp.concatenate([sv[:, :TQ], sv[:, TQ:] + hide], axis=1)
        sc_ref[p] = sv
        macc_ref[...] = jnp.maximum(macc_ref[...], tile_max(sv))
        return carry

    lax.fori_loop(0, n_pairs, far_scores, 0)

    rhs_n = jnp.concatenate([key_rows(kt1, 0), key_rows(qb, 0)], axis=0)
    k1 = pl.multiple_of(kt1 * TQ, TQ)
    kq = pl.multiple_of(qb * TQ, TQ)
    cen_n = jnp.concatenate([ceneg_ref[pl.ds(k1, TQ), :], ceneg_ref[pl.ds(kq, TQ), :]], axis=0)
    bias_n = jnp.concatenate([b3_ref[0, kind1], b3_ref[0, TILE_DIAG]], axis=1)
    s_n = (_dot_nt(q4_ref[...], jnp.concatenate([rhs_n, cen_n], axis=1)) + bias_n) * LOG2E
    nsc_ref[...] = s_n
    m_s = jnp.max(jnp.maximum(macc_ref[...], tile_max(s_n)), axis=1, keepdims=True)
    mb_ref[...] = jnp.broadcast_to(m_s, (rows, LANES))
    oacc_ref[...] = jnp.zeros((rows, 2 * LANES), F32)

    def far_values(p, carry):
        k0 = pl.multiple_of(p * pair, pair)
        pv = exp2_rel(sc_ref[p], mb_ref[...])
        oacc_ref[...] += _dot(pv, jnp.concatenate([kv_ref[0, pl.ds(k0, pair), 0:LANES],
                                                   ones[0:pair]], axis=1))
        return carry

    lax.fori_loop(0, n_pairs, far_values, 0)
    p_n = exp2_rel(nsc_ref[...], mb_ref[...])
    o_s = oacc_ref[...] + _dot(p_n, jnp.concatenate([rhs_n, ones[0:pair]], axis=1))

    for hh in range(HPG):
        r = slice(hh * TQ, (hh + 1) * TQ)
        comb = None
        for br, o_br in enumerate((oc_ref, o_s, ow_ref)):
            l = o_br[r, LANES:2 * LANES]
            gate = gate_ref[0, 0, :, 3 * hh + br:3 * hh + br + 1]
            term = o_br[r, 0:LANES] * (gate / jnp.where(l > 0.0, l, 1.0))
            comb = term if comb is None else comb + term
        o_ref[0, :, hh * LANES:(hh + 1) * LANES] = jnp.where(low, 0.0, comb).astype(BF16)


def _attention(q, gates, kvr, kvcmp, rel_bias):
    b, s, hd = q.shape
    nq = s // TQ
    n_slc = s // SLC_LEN
    n_cmp = s // CMP_STRIDE
    rows = HPG * TQ
    b3, bc = _attn_tables(rel_bias, s)
    ceneg, ones, ov, place = _attn_consts(s)
    gw = 3 * HPG
    gates_g = (gates[:, :, :3 * N_HEADS].reshape(b, s, N_GROUPS, gw).transpose(0, 2, 1, 3))
    gl = HPG * HEAD_DIM
    return pl.pallas_call(
        _attn_kernel,
        grid=(b, N_GROUPS, nq),
        in_specs=[
            pl.BlockSpec((1, TQ, gl), lambda i, g, t: (i, t, g)),
            pl.BlockSpec((1, 1, TQ, gw), lambda i, g, t: (i, g, t, 0)),
            pl.BlockSpec((1, s, 4 * HEAD_DIM), lambda i, g, t: (i, 0, g)),
            pl.BlockSpec((1, 1, n_cmp, LANES), lambda i, g, t: (i, g, 0, 0)),
            pl.BlockSpec((1, N_KINDS, rows, TQ), lambda i, g, t: (g, 0, 0, 0)),
            pl.BlockSpec((1, 1, rows, n_cmp), lambda i, g, t: (t, g, 0, 0)),
            pl.BlockSpec((s, LANES), lambda *_: (0, 0)),
            pl.BlockSpec((3 * TQ, LANES), lambda *_: (0, 0)),
            pl.BlockSpec((n_slc, n_cmp), lambda *_: (0, 0)),
            pl.BlockSpec((n_slc, LANES), lambda *_: (0, 0)),
        ],
        out_specs=pl.BlockSpec((1, TQ, HPG * LANES), lambda i, g, t: (i, t, g)),
        out_shape=jax.ShapeDtypeStruct((b, s, N_HEADS * LANES), BF16),
        scratch_shapes=[
            pltpu.VMEM((rows, 2 * LANES), BF16),
            pltpu.VMEM(((nq - 1) // 2, rows, 2 * TQ), F32),
            pltpu.VMEM((rows, 2 * TQ), F32),
            pltpu.VMEM((rows, LANES), F32),
            pltpu.VMEM((rows, LANES), F32),
            pltpu.VMEM((rows, 2 * LANES), F32),
            pltpu.VMEM((rows, 2 * LANES), F32),
            pltpu.VMEM((rows, 2 * LANES), F32),
        ],
        compiler_params=pltpu.CompilerParams(
            dimension_semantics=("arbitrary", "arbitrary", "arbitrary"),
            vmem_limit_bytes=VMEM_LIMIT),
        name="nsa_attn",
    )(q, gates_g, kvr, kvcmp, b3, bc, ceneg, ones, ov, place)


def _wo_padded(w_o):
    d = w_o.shape[1]
    w = w_o.reshape(N_HEADS, HEAD_DIM, d)
    return jnp.pad(w, ((0, 0), (HEAD_DIM, 0), (0, 0))).reshape(N_HEADS * LANES, d).astype(BF16)


def kernel(x, mix_norm, a_w_in, a_conv, a_w_out, ffn_norm, ffn_up, ffn_conv, ffn_down,
           kv_norm, w_kv, cmp_pe, cmp_w1, cmp_b1, cmp_w2, b_w_qg, b_w_o, rel_bias, final_norm):
    b, s, d = x.shape
    assert d == D_MODEL and s % 512 == 0 and mix_norm.shape[0] == 2
    x = _mixer(x, mix_norm[0], a_w_in[0], a_conv[0], a_w_out[0])
    x = _ffn(x, ffn_norm[0], ffn_up[0], ffn_conv[0], ffn_down[0])
    q, gates, kvc, kvr = _proj(x, mix_norm[1], kv_norm, b_w_qg[0], w_kv)
    kvcmp = _compress(kvc, cmp_pe, cmp_w1, cmp_b1, cmp_w2)
    attn = _attention(q, gates, kvr, kvcmp, rel_bias)
    return _ffn(x, ffn_norm[1], ffn_up[1], ffn_conv[1], ffn_down[1],
                attn=attn, w_o=_wo_padded(b_w_o[0]), final_g=final_norm)
```

```python
import functools
import math

import jax
import jax.numpy as jnp
import numpy as np
from jax import lax
from jax.experimental import pallas as pl
from jax.experimental.pallas import tpu as pltpu

D_MODEL = 1024
CONV_W = 3
D_FF = 2816
N_HEADS = 16
N_GROUPS = 4
HPG = N_HEADS // N_GROUPS
HEAD_DIM = 64
CMP_LEN = 32
CMP_STRIDE = 16
CMP_HIDDEN = 128
SLC_LEN = 64
N_SEL = 16
WINDOW = 512
N_BUCKETS = 32
MAX_DISTANCE = 128
EPS = 1e-6
NEG = -1e30

LANES = 128
V7X_VMEM_BYTES = 64 * 2**20
VMEM_LIMIT = V7X_VMEM_BYTES - 8 * 2**20

TQ = 256
CARRY_ROWS = 8
F32 = jnp.float32
BF16 = jnp.bfloat16


def _dot(a, b):
    return jnp.dot(a, b, preferred_element_type=F32)


def _dot_nt(a, b, precision=None):
    return lax.dot_general(a, b, (((1,), (1,)), ((), ())),
                           preferred_element_type=F32, precision=precision)


def _split_rows(dot, a, b):
    h = a.shape[0] // 2
    return jnp.concatenate([dot(a[:h], b), dot(a[h:], b)], axis=0)


def _rms_scale(x):
    return x * lax.rsqrt(jnp.mean(x * x, axis=-1, keepdims=True) + EPS)


def _conv3(buf_ref, cw, rows):
    c = CARRY_ROWS
    return (cw[0:1, :] * buf_ref[c - 2:c - 2 + rows, :]
            + cw[1:2, :] * buf_ref[c - 1:c - 1 + rows, :]
            + cw[2:3, :] * buf_ref[c:c + rows, :])


def _mixer_kernel(x_ref, g_ref, win_ref, cw_ref, wout_ref, o_ref, ubuf_ref, *, tm):
    d = D_MODEL

    @pl.when(pl.program_id(1) == 0)
    def _():
        ubuf_ref[0:CARRY_ROWS, :] = jnp.zeros((CARRY_ROWS, d), F32)

    x = x_ref[0]
    h = (_rms_scale(x) * g_ref[...]).astype(BF16)
    cg = _dot(h, win_ref[:, d:2 * d])
    v = _dot(h, win_ref[:, 2 * d:3 * d])
    ubuf_ref[CARRY_ROWS:CARRY_ROWS + tm, :] = cg * v
    conv = _conv3(ubuf_ref, cw_ref[...], tm)
    ubuf_ref[0:CARRY_ROWS, :] = ubuf_ref[tm:tm + CARRY_ROWS, :]
    bg = _dot(h, win_ref[:, 0:d])
    y = (bg * conv).astype(BF16)
    o_ref[0] = x + _dot(y, wout_ref[...])


def _mixer(x, g, w_in, conv_w, w_out, *, tm=512):
    b, s, d = x.shape
    const = lambda *_: (0, 0)
    one = pl.Buffered(1)
    return pl.pallas_call(
        functools.partial(_mixer_kernel, tm=tm),
        grid=(b, s // tm),
        in_specs=[
            pl.BlockSpec((1, tm, d), lambda i, j: (i, j, 0)),
            pl.BlockSpec((1, d), const),
            pl.BlockSpec((d, 3 * d), const, pipeline_mode=one),
            pl.BlockSpec((CONV_W, d), const),
            pl.BlockSpec((d, d), const, pipeline_mode=one),
        ],
        out_specs=pl.BlockSpec((1, tm, d), lambda i, j: (i, j, 0)),
        out_shape=jax.ShapeDtypeStruct((b, s, d), F32),
        scratch_shapes=[pltpu.VMEM((CARRY_ROWS + tm, d), F32)],
        compiler_params=pltpu.CompilerParams(
            dimension_semantics=("arbitrary", "arbitrary"), vmem_limit_bytes=VMEM_LIMIT),
        name="mixer",
    )(x, g.reshape(1, d), w_in.astype(BF16), conv_w, w_out.astype(BF16))


def _ffn_kernel(*refs, tm, tf, has_attn, final_norm):
    refs = list(refs)
    x_ref = refs.pop(0)
    a_ref = refs.pop(0) if has_attn else None
    wo_ref = refs.pop(0) if has_attn else None
    g_ref, wup_ref, cw_ref, wdn_ref = refs[:4]
    refs = refs[4:]
    gf_ref = refs.pop(0) if final_norm else None
    o_ref, h_ref, ua_ref, ug_ref, carry_ref, acc_ref = refs

    @pl.when(pl.program_id(1) == 0)
    def _():
        carry_ref[...] = jnp.zeros(carry_ref.shape, F32)

    x = x_ref[0]
    if has_attn:
        x = x + _dot(a_ref[0], wo_ref[...])
    h_ref[...] = (_rms_scale(x) * g_ref[...]).astype(BF16)

    def up_conv(buf_ref, c0):
        cols = slice(c0, c0 + tf)
        buf_ref[0:CARRY_ROWS, :] = carry_ref[:, cols]
        buf_ref[CARRY_ROWS:CARRY_ROWS + tm, :] = _dot(h_ref[...], wup_ref[:, cols])
        carry_ref[:, cols] = buf_ref[tm:tm + CARRY_ROWS, :]
        return _conv3(buf_ref, cw_ref[:, cols], tm)

    for c in range(D_FF // tf):
        a = up_conv(ua_ref, c * tf)
        gt = up_conv(ug_ref, D_FF + c * tf)
        act = (a * jax.nn.sigmoid(a) * gt).astype(BF16)
        part = _dot(act, wdn_ref[c * tf:(c + 1) * tf, :])
        if c == 0:
            acc_ref[...] = x + part
        else:
            acc_ref[...] += part
    y = acc_ref[...]
    if final_norm:
        y = _rms_scale(y) * gf_ref[...]
    o_ref[0] = y


def _ffn(x, g, w_up, conv_w, w_down, *, attn=None, w_o=None, final_g=None, tm=512, tf=1408):
    b, s, d = x.shape
    has_attn = attn is not None
    final_norm = final_g is not None
    const = lambda *_: (0, 0)
    row = lambda i, j: (i, j, 0)
    one = pl.Buffered(1)
    args = [x]
    in_specs = [pl.BlockSpec((1, tm, d), row)]
    if has_attn:
        ka = attn.shape[-1]
        args += [attn, w_o]
        in_specs += [pl.BlockSpec((1, tm, ka), row),
                     pl.BlockSpec((ka, d), const, pipeline_mode=one)]
    args += [g.reshape(1, d), w_up.astype(BF16), conv_w, w_down.astype(BF16)]
    in_specs += [pl.BlockSpec((1, d), const),
                 pl.BlockSpec((d, 2 * D_FF), const, pipeline_mode=one),
                 pl.BlockSpec((CONV_W, 2 * D_FF), const),
                 pl.BlockSpec((D_FF, d), const, pipeline_mode=one)]
    if final_norm:
        args.append(final_g.reshape(1, d))
        in_specs.append(pl.BlockSpec((1, d), const))
    return pl.pallas_call(
        functools.partial(_ffn_kernel, tm=tm, tf=tf, has_attn=has_attn, final_norm=final_norm),
        grid=(b, s // tm),
        in_specs=in_specs,
        out_specs=pl.BlockSpec((1, tm, d), row),
        out_shape=jax.ShapeDtypeStruct((b, s, d), F32),
        scratch_shapes=[
            pltpu.VMEM((tm, d), BF16),
            pltpu.VMEM((CARRY_ROWS + tm, tf), F32),
            pltpu.VMEM((CARRY_ROWS + tm, tf), F32),
            pltpu.VMEM((CARRY_ROWS, 2 * D_FF), F32),
            pltpu.VMEM((tm, d), F32),
        ],
        compiler_params=pltpu.CompilerParams(
            dimension_semantics=("arbitrary", "arbitrary"), vmem_limit_bytes=VMEM_LIMIT),
        name="ffn_attn" if has_attn else "ffn",
    )(*args)


def _proj_kernel(x_ref, gm_ref, gk_ref, wq_ref, wg_ref, wkc_ref, wkr_ref,
                 q_ref, gate_ref, kvc_ref, kvr_ref):
    xn = _rms_scale(x_ref[0])
    hq = (xn * gm_ref[...]).astype(BF16)
    hs = (xn * gk_ref[...]).astype(BF16)
    q_ref[0] = (_dot(hq, wq_ref[...]) * (HEAD_DIM ** -0.5)).astype(BF16)
    gate_ref[0] = jax.nn.sigmoid(_dot(hq, wg_ref[...]))
    kvc_ref[0] = _dot(hs, wkc_ref[...])
    kvr_ref[0] = _dot(hs, wkr_ref[...]).astype(BF16)


def _proj(x, g_mix, g_kv, w_qg, w_kv, *, tm=512):
    b, s, d = x.shape
    hd = N_HEADS * HEAD_DIM
    gd = N_GROUPS * HEAD_DIM
    w_q = w_qg[:, :hd].astype(BF16)
    w_g = jnp.pad(w_qg[:, hd:], ((0, 0), (0, LANES - 3 * N_HEADS))).astype(BF16)
    w_kc = w_kv[:, :2 * gd].astype(BF16)
    w_kr = (w_kv[:, 2 * gd:].reshape(d, 4, N_GROUPS, HEAD_DIM)
            .transpose(0, 2, 1, 3).reshape(d, 4 * gd).astype(BF16))
    const = lambda *_: (0, 0)
    row = lambda i, j: (i, j, 0)
    one = pl.Buffered(1)
    return pl.pallas_call(
        _proj_kernel,
        grid=(b, s // tm),
        in_specs=[
            pl.BlockSpec((1, tm, d), row),
            pl.BlockSpec((1, d), const),
            pl.BlockSpec((1, d), const),
            pl.BlockSpec((d, hd), const, pipeline_mode=one),
            pl.BlockSpec((d, LANES), const, pipeline_mode=one),
            pl.BlockSpec((d, 2 * gd), const, pipeline_mode=one),
            pl.BlockSpec((d, 4 * gd), const, pipeline_mode=one),
        ],
        out_specs=[
            pl.BlockSpec((1, tm, hd), row),
            pl.BlockSpec((1, tm, LANES), row),
            pl.BlockSpec((1, tm, 2 * gd), row),
            pl.BlockSpec((1, tm, 4 * gd), row),
        ],
        out_shape=[
            jax.ShapeDtypeStruct((b, s, hd), BF16),
            jax.ShapeDtypeStruct((b, s, LANES), F32),
            jax.ShapeDtypeStruct((b, s, 2 * gd), F32),
            jax.ShapeDtypeStruct((b, s, 4 * gd), BF16),
        ],
        compiler_params=pltpu.CompilerParams(
            dimension_semantics=("arbitrary", "arbitrary"), vmem_limit_bytes=VMEM_LIMIT),
        name="proj",
    )(x, g_mix.reshape(1, d), g_kv.reshape(1, d), w_q, w_g, w_kc, w_kr)


def _gelu_tanh(x):
    return 0.5 * x * (1.0 + jnp.tanh(math.sqrt(2.0 / math.pi) * (x + 0.044715 * (x * x * x))))


def _compress_kernel(z_ref, pe_ref, w1_ref, b1_ref, w2_ref, o_ref):
    half = (CMP_LEN // 2) * HEAD_DIM
    n_chunks = z_ref.shape[3]
    acc = None
    for j in range(2):
        z = z_ref[0, j, 0]
        za = (z + pe_ref[j, 0:1, :]).astype(BF16)
        zb = (z + pe_ref[j, 1:2, :]).astype(BF16)
        h1 = _dot(za, w1_ref[j, 0:half, :])
        h2 = _dot(zb, w1_ref[j, half:2 * half, :])
        pre = h1 + pltpu.roll(h2, n_chunks - 1, 0) + b1_ref[j]
        part = _dot(_gelu_tanh(pre).astype(BF16), w2_ref[j])
        acc = part if acc is None else acc + part
    o_ref[0, 0] = acc.astype(BF16)


def _compress(kvc, cmp_pe, cmp_w1, cmp_b1, cmp_w2):
    b, s, _ = kvc.shape
    n_chunks = s // CMP_STRIDE
    half = (CMP_LEN // 2) * HEAD_DIM
    z = (kvc.reshape(b, n_chunks, CMP_STRIDE, 2, N_GROUPS, HEAD_DIM)
         .transpose(0, 3, 4, 1, 2, 5).reshape(b, 2, N_GROUPS, n_chunks, half))
    pe = cmp_pe.reshape(2, 2, half)
    w1 = cmp_w1.astype(BF16)
    b1 = cmp_b1.reshape(2, 1, CMP_HIDDEN)
    w2 = jnp.stack([jnp.pad(cmp_w2[0], ((0, 0), (0, HEAD_DIM))),
                    jnp.pad(cmp_w2[1], ((0, 0), (HEAD_DIM, 0)))]).astype(BF16)
    return pl.pallas_call(
        _compress_kernel,
        grid=(b, N_GROUPS),
        in_specs=[
            pl.BlockSpec((1, 2, 1, n_chunks, half), lambda i, g: (i, 0, g, 0, 0)),
            pl.BlockSpec((2, 2, half), lambda *_: (0, 0, 0)),
            pl.BlockSpec((2, 2 * half, CMP_HIDDEN), lambda *_: (0, 0, 0)),
            pl.BlockSpec((2, 1, CMP_HIDDEN), lambda *_: (0, 0, 0)),
            pl.BlockSpec((2, CMP_HIDDEN, LANES), lambda *_: (0, 0, 0)),
        ],
        out_specs=pl.BlockSpec((1, 1, n_chunks, LANES), lambda i, g: (i, g, 0, 0)),
        out_shape=jax.ShapeDtypeStruct((b, N_GROUPS, n_chunks, LANES), BF16),
        compiler_params=pltpu.CompilerParams(
            dimension_semantics=("arbitrary", "arbitrary"), vmem_limit_bytes=VMEM_LIMIT),
        name="compress",
    )(z, pe, w1, b1, w2)


TILE_HIDDEN, TILE_PREV2, TILE_PREV, TILE_DIAG = 0, 1, 2, 3
N_KINDS = 4
CMP_WIN = 32
LOG2E = 1.0 / math.log(2.0)


def _rel_bucket_np(dist):
    max_exact = N_BUCKETS // 2
    d = np.maximum(dist, 0)
    df = np.maximum(d, 1).astype(np.float32)
    large = max_exact + (np.log(df / max_exact) / np.float32(math.log(MAX_DISTANCE / max_exact))
                         * (N_BUCKETS - max_exact)).astype(np.int32)
    return np.where(d < max_exact, d, np.minimum(large, N_BUCKETS - 1))


def _toeplitz(w):
    t = w.shape[-1] // 2
    lead = w.shape[:-1]
    a = jnp.broadcast_to(w[..., None, :], lead + (t, 2 * t)).reshape(lead + (2 * t * t,))
    return a[..., :t * (2 * t - 1)].reshape(lead + (t, 2 * t - 1))[..., :t]


def _attn_tables(rel_bias, s):
    nq = s // TQ
    assert _rel_bucket_np(np.arange(TQ // 2, 4 * TQ)).min() == N_BUCKETS - 1
    relb = (rel_bias.astype(F32) - rel_bias[N_BUCKETS - 1].astype(F32)).T
    f = jnp.take(relb, jnp.asarray(_rel_bucket_np(np.arange(2 * TQ))), axis=1)
    neg = jnp.full((N_HEADS, TQ), NEG, F32)
    w_diag = jnp.concatenate([f[:, 0:1], neg, jnp.flip(f[:, 1:TQ], axis=1)], axis=1)
    w_prev = jnp.concatenate([jnp.flip(f[:, 1:TQ + 1], axis=1), jnp.zeros((N_HEADS, 1), F32),
                              jnp.flip(f[:, TQ + 1:2 * TQ], axis=1)], axis=1)
    ti = np.arange(TQ)[:, None]
    ki = np.arange(TQ)[None, :]
    prev2 = jnp.broadcast_to(jnp.asarray(np.where(ki > ti, 0.0, NEG), F32), (N_HEADS, TQ, TQ))
    hidden = jnp.full((N_HEADS, TQ, TQ), NEG, F32)
    b3 = jnp.stack([hidden, prev2, _toeplitz(w_prev), _toeplitz(w_diag)], axis=1)
    b3 = (b3.reshape(N_GROUPS, HPG, N_KINDS, TQ, TQ).transpose(0, 2, 1, 3, 4)
          .reshape(N_GROUPS, N_KINDS, HPG * TQ, TQ))
    n_cmp = s // CMP_STRIDE
    lead = CMP_WIN // 4
    dist = ti - CMP_STRIDE * (np.arange(CMP_WIN)[None, :] - lead) - (CMP_LEN - 1)
    pw = jnp.take(relb, jnp.asarray(_rel_bucket_np(dist)), axis=1)
    pw = jnp.where(jnp.asarray(dist >= 0), pw, NEG)
    full = jnp.concatenate([jnp.zeros((N_HEADS, TQ, n_cmp - lead), F32), pw,
                            jnp.full((N_HEADS, TQ, n_cmp), NEG, F32)], axis=-1)
    per_tile = TQ // CMP_STRIDE
    bc = jnp.stack([full[..., n_cmp - per_tile * t:2 * n_cmp - per_tile * t] for t in range(nq)])
    bc = bc.reshape(nq, N_GROUPS, HPG * TQ, n_cmp)
    return b3, bc


def _attn_consts(s):
    n_slc = s // SLC_LEN
    n_cmp = s // CMP_STRIDE
    key = np.arange(s)[:, None]
    lane = np.arange(LANES)[None, :]
    ceneg = np.where((key // SLC_LEN == lane) & (lane < n_slc), NEG, 0.0)
    j = np.arange(n_slc)[:, None]
    i = np.arange(n_cmp)[None, :]
    ov = ((i * CMP_STRIDE < j * SLC_LEN + SLC_LEN) & (i * CMP_STRIDE + CMP_LEN > j * SLC_LEN)
          & (i < n_cmp - 1))
    place = np.eye(n_slc, LANES)
    return (jnp.asarray(ceneg, BF16), jnp.ones((3 * TQ, LANES), BF16),
            jnp.asarray(np.concatenate([ov] * 3, axis=1), BF16), jnp.asarray(place, BF16))


def _attn_kernel(q_ref, gate_ref, kv_ref, kvc_ref, b3_ref, bc_ref, ceneg_ref, ones_ref,
                 ov_ref, place_ref, o_ref, q4_ref, sc_ref, nsc_ref, macc_ref, mb_ref, oacc_ref,
                 comb_ref):
    qb = pl.program_id(2)
    n_slc = ov_ref.shape[0]
    rows = HPG * TQ
    lane = lax.broadcasted_iota(jnp.int32, (TQ, LANES), 1)
    low = lane < HEAD_DIM

    for pair in range(HPG // 2):
        qp = q_ref[0, :, pair * LANES:(pair + 1) * LANES].astype(F32)
        q4_ref[(2 * pair) * TQ:(2 * pair + 1) * TQ, 0:LANES] = jnp.where(low, qp, 0.0).astype(BF16)
        q4_ref[(2 * pair + 1) * TQ:(2 * pair + 2) * TQ, 0:LANES] = (
            jnp.where(low, pltpu.roll(qp, HEAD_DIM, 1), 0.0).astype(BF16))
    q4 = q4_ref[:, 0:LANES]
    ones = ones_ref[...]

    def key_rows(kt, col0):
        k0 = pl.multiple_of(kt * TQ, TQ)
        return kv_ref[0, pl.ds(k0, TQ), col0:col0 + LANES]

    def lane_tiles(x):
        return [x[:, i:i + LANES] for i in range(0, x.shape[1], LANES)]

    def tile_max(x):
        return functools.reduce(jnp.maximum, lane_tiles(x))

    def exp2_rel(sv, mb):
        return jnp.concatenate([jnp.exp2(t - mb) for t in lane_tiles(sv)], axis=1).astype(BF16)

    kt1 = jnp.maximum(qb - 1, 0)
    kt2 = jnp.maximum(qb - 2, 0)
    kind1 = jnp.where(qb >= 1, TILE_PREV, TILE_HIDDEN)
    kind2 = jnp.where(qb >= 2, TILE_PREV2, TILE_HIDDEN)

    def split3(x):
        hi = x.astype(BF16)
        r1 = x - hi.astype(F32)
        mid = r1.astype(BF16)
        lo = (r1 - mid.astype(F32)).astype(BF16)
        return jnp.concatenate([hi, mid, lo], axis=1)

    def weighted(o, hh, br):
        r = slice(hh * TQ, (hh + 1) * TQ)
        l = o[r, LANES:2 * LANES]
        gate = gate_ref[0, 0, :, 3 * hh + br:3 * hh + br + 1]
        return o[r, 0:LANES] * (gate / jnp.where(l > 0.0, l, 1.0))

    tiles_w = [key_rows(kt2, LANES), key_rows(kt1, LANES), key_rows(qb, LANES)]
    kinds_w = [kind2, kind1, TILE_DIAG]

    def win_scores(i):
        return (_split_rows(_dot_nt, q4, tiles_w[i]) + b3_ref[0, kinds_w[i]]) * LOG2E

    kvc = kvc_ref[0, 0]
    bias_c = bc_ref[0, 0]
    s_c = _split_rows(_dot_nt, q4, kvc) + bias_c
    s_w0 = win_scores(0)
    m_c = jnp.max(s_c, axis=1, keepdims=True)
    e_c = jnp.where(bias_c > 0.5 * NEG, jnp.exp(s_c - m_c), 0.0)
    o_c = _split_rows(_dot, e_c.astype(BF16),
                      jnp.concatenate([kvc, ones[0:kvc.shape[0]]], axis=1))
    s_w1 = win_scores(1)
    l_c = o_c[:, LANES:2 * LANES]
    p_c = e_c / jnp.where(l_c > 0.0, l_c, 1.0)
    psum = p_c[0:TQ]
    for hh in range(1, HPG):
        psum = psum + p_c[hh * TQ:(hh + 1) * TQ]

    imp_t = _dot_nt(ov_ref[...], split3(psum))
    s_w2 = win_scores(2)
    s_w = jnp.concatenate([s_w0, s_w1, s_w2], axis=1)
    m_w = jnp.max(tile_max(s_w), axis=1, keepdims=True)
    p_w = exp2_rel(s_w, jnp.broadcast_to(m_w, (rows, LANES)))
    o_w = _split_rows(_dot, p_w,
                      jnp.concatenate([jnp.concatenate(tiles_w, axis=0), ones], axis=1))

    jrow = lax.broadcasted_iota(jnp.int32, (n_slc, TQ), 0)
    tpos = qb * TQ + lax.broadcasted_iota(jnp.int32, (n_slc, TQ), 1)
    cur = lax.shift_right_logical(tpos, int(math.log2(SLC_LEN)))
    forced = (jrow == 0) | (jrow == cur) | (jrow == cur - 1)
    score = jnp.where(forced, 3e38, jnp.where(jrow <= cur, imp_t, -1.0))
    rank = jnp.zeros((n_slc, TQ), F32)
    for k in range(n_slc):
        sk = score[k:k + 1, :]
        beats = (sk > score) | ((sk == score) & (jrow > k))
        rank = rank + beats.astype(F32)
    notsel_t = (rank >= float(N_SEL)).astype(BF16)
    q_mask = lax.dot_general(notsel_t, place_ref[...], (((0,), (0,)), ((), ())),
                             preferred_element_type=F32).astype(BF16)
    for hh in range(HPG):
        q4_ref[hh * TQ:(hh + 1) * TQ, LANES:2 * LANES] = q_mask

    for hh in range(HPG):
        comb_ref[hh * TQ:(hh + 1) * TQ, :] = weighted(o_c, hh, 0) + weighted(o_w, hh, 2)

    n_far = jnp.maximum(qb - 1, 0)
    n_pairs = lax.shift_right_logical(n_far + 1, 1)
    pair = 2 * TQ
    macc_ref[...] = jnp.full((rows, LANES), -3e38, F32)

    def far_scores(p, carry):
        k0 = pl.multiple_of(p * pair, pair)
        rhs = jnp.concatenate([kv_ref[0, pl.ds(k0, pair), 0:LANES], ceneg_ref[pl.ds(k0, pair), :]],
                              axis=1)
        sv = _dot_nt(q4_ref[...], rhs) * LOG2E
        hide = jnp.where(2 * p + 1 < n_far, 0.0, NEG)
        sv = jnp.concatenate([sv[:, :TQ], sv[:, TQ:] + hide], axis=1)
        sc_ref[p] = sv
        macc_ref[...] = jnp.maximum(macc_ref[...], tile_max(sv))
        return carry

    lax.fori_loop(0, n_pairs, far_scores, 0)

    rhs_n = jnp.concatenate([key_rows(kt1, 0), key_rows(qb, 0)], axis=0)
    k1 = pl.multiple_of(kt1 * TQ, TQ)
    kq = pl.multiple_of(qb * TQ, TQ)
    cen_n = jnp.concatenate([ceneg_ref[pl.ds(k1, TQ), :], ceneg_ref[pl.ds(kq, TQ), :]], axis=0)
    bias_n = jnp.concatenate([b3_ref[0, kind1], b3_ref[0, TILE_DIAG]], axis=1)
    s_n = (_dot_nt(q4_ref[...], jnp.concatenate([rhs_n, cen_n], axis=1)) + bias_n) * LOG2E
    nsc_ref[...] = s_n
    m_s = jnp.max(jnp.maximum(macc_ref[...], tile_max(s_n)), axis=1, keepdims=True)
    mb_ref[...] = jnp.broadcast_to(m_s, (rows, LANES))
    oacc_ref[...] = jnp.zeros((rows, 2 * LANES), F32)

    def far_values(p, carry):
        k0 = pl.multiple_of(p * pair, pair)
        rhs = jnp.concatenate([kv_ref[0, pl.ds(k0, pair), 0:LANES], ones[0:pair]], axis=1)
        for r in (slice(0, rows // 2), slice(rows // 2, rows)):
            oacc_ref[r, :] += _dot(exp2_rel(sc_ref[p, r, :], mb_ref[r, :]), rhs)
        return carry

    lax.fori_loop(0, n_pairs, far_values, 0)
    p_n = exp2_rel(nsc_ref[...], mb_ref[...])
    o_s = oacc_ref[...] + _split_rows(_dot, p_n, jnp.concatenate([rhs_n, ones[0:pair]], axis=1))

    for hh in range(HPG):
        comb = comb_ref[hh * TQ:(hh + 1) * TQ, :] + weighted(o_s, hh, 1)
        o_ref[0, :, hh * LANES:(hh + 1) * LANES] = jnp.where(low, 0.0, comb).astype(BF16)


def _attention(q, gates, kvr, kvcmp, rel_bias):
    b, s, hd = q.shape
    nq = s // TQ
    n_slc = s // SLC_LEN
    n_cmp = s // CMP_STRIDE
    rows = HPG * TQ
    b3, bc = _attn_tables(rel_bias, s)
    ceneg, ones, ov, place = _attn_consts(s)
    gw = 3 * HPG
    gates_g = (gates[:, :, :3 * N_HEADS].reshape(b, s, N_GROUPS, gw).transpose(0, 2, 1, 3))
    gl = HPG * HEAD_DIM
    return pl.pallas_call(
        _attn_kernel,
        grid=(b, N_GROUPS, nq),
        in_specs=[
            pl.BlockSpec((1, TQ, gl), lambda i, g, t: (i, t, g)),
            pl.BlockSpec((1, 1, TQ, gw), lambda i, g, t: (i, g, t, 0)),
            pl.BlockSpec((1, s, 4 * HEAD_DIM), lambda i, g, t: (i, 0, g)),
            pl.BlockSpec((1, 1, n_cmp, LANES), lambda i, g, t: (i, g, 0, 0)),
            pl.BlockSpec((1, N_KINDS, rows, TQ), lambda i, g, t: (g, 0, 0, 0)),
            pl.BlockSpec((1, 1, rows, n_cmp), lambda i, g, t: (t, g, 0, 0)),
            pl.BlockSpec((s, LANES), lambda *_: (0, 0)),
            pl.BlockSpec((3 * TQ, LANES), lambda *_: (0, 0)),
            pl.BlockSpec((n_slc, 3 * n_cmp), lambda *_: (0, 0)),
            pl.BlockSpec((n_slc, LANES), lambda *_: (0, 0)),
        ],
        out_specs=pl.BlockSpec((1, TQ, HPG * LANES), lambda i, g, t: (i, t, g)),
        out_shape=jax.ShapeDtypeStruct((b, s, N_HEADS * LANES), BF16),
        scratch_shapes=[
            pltpu.VMEM((rows, 2 * LANES), BF16),
            pltpu.VMEM(((nq - 1) // 2, rows, 2 * TQ), F32),
            pltpu.VMEM((rows, 2 * TQ), F32),
            pltpu.VMEM((rows, LANES), F32),
            pltpu.VMEM((rows, LANES), F32),
            pltpu.VMEM((rows, 2 * LANES), F32),
            pltpu.VMEM((rows, LANES), F32),
        ],
        compiler_params=pltpu.CompilerParams(
            dimension_semantics=("arbitrary", "arbitrary", "arbitrary"),
            vmem_limit_bytes=VMEM_LIMIT),
        name="nsa_attn",
    )(q, gates_g, kvr, kvcmp, b3, bc, ceneg, ones, ov, place)


def _wo_padded(w_o):
    d = w_o.shape[1]
    w = w_o.reshape(N_HEADS, HEAD_DIM, d)
    return jnp.pad(w, ((0, 0), (HEAD_DIM, 0), (0, 0))).reshape(N_HEADS * LANES, d).astype(BF16)


def kernel(x, mix_norm, a_w_in, a_conv, a_w_out, ffn_norm, ffn_up, ffn_conv, ffn_down,
           kv_norm, w_kv, cmp_pe, cmp_w1, cmp_b1, cmp_w2, b_w_qg, b_w_o, rel_bias, final_norm):
    b, s, d = x.shape
    assert d == D_MODEL and s % 512 == 0 and mix_norm.shape[0] == 2
    x = _mixer(x, mix_norm[0], a_w_in[0], a_conv[0], a_w_out[0])
    x = _ffn(x, ffn_norm[0], ffn_up[0], ffn_conv[0], ffn_down[0])
    q, gates, kvc, kvr = _proj(x, mix_norm[1], kv_norm, b_w_qg[0], w_kv)
    kvcmp = _compress(kvc, cmp_pe, cmp_w1, cmp_b1, cmp_w2)
    attn = _attention(q, gates, kvr, kvcmp, rel_bias)
    return _ffn(x, ffn_norm[1], ffn_up[1], ffn_conv[1], ffn_down[1],
                attn=attn, w_o=_wo_padded(b_w_o[0]), final_g=final_norm)
```

```python
import functools
import math

import jax
import jax.numpy as jnp
import numpy as np
from jax import lax
from jax.experimental import pallas as pl
from jax.experimental.pallas import tpu as pltpu

D_MODEL = 1024
CONV_W = 3
D_FF = 2816
N_HEADS = 16
N_GROUPS = 4
HPG = N_HEADS // N_GROUPS
HEAD_DIM = 64
CMP_LEN = 32
CMP_STRIDE = 16
CMP_HIDDEN = 128
SLC_LEN = 64
N_SEL = 16
WINDOW = 512
N_BUCKETS = 32
MAX_DISTANCE = 128
EPS = 1e-6
NEG = -1e30

LANES = 128
V7X_VMEM_BYTES = 64 * 2**20
VMEM_LIMIT = V7X_VMEM_BYTES - 8 * 2**20

TQ = 256
CARRY_ROWS = 8
F32 = jnp.float32
BF16 = jnp.bfloat16


def _dot(a, b):
    return jnp.dot(a, b, preferred_element_type=F32)


def _dot_nt(a, b, precision=None):
    return lax.dot_general(a, b, (((1,), (1,)), ((), ())),
                           preferred_element_type=F32, precision=precision)


def _split_rows(dot, a, b):
    h = a.shape[0] // 2
    return jnp.concatenate([dot(a[:h], b), dot(a[h:], b)], axis=0)


def _rms_scale(x):
    return x * lax.rsqrt(jnp.mean(x * x, axis=-1, keepdims=True) + EPS)


def _conv3(buf_ref, cw, rows):
    c = CARRY_ROWS
    return (cw[0:1, :] * buf_ref[c - 2:c - 2 + rows, :]
            + cw[1:2, :] * buf_ref[c - 1:c - 1 + rows, :]
            + cw[2:3, :] * buf_ref[c:c + rows, :])


def _mixer_kernel(x_ref, g_ref, win_ref, cw_ref, wout_ref, o_ref, ubuf_ref, *, tm):
    d = D_MODEL

    @pl.when(pl.program_id(1) == 0)
    def _():
        ubuf_ref[0:CARRY_ROWS, :] = jnp.zeros((CARRY_ROWS, d), F32)

    x = x_ref[0]
    h = (_rms_scale(x) * g_ref[...]).astype(BF16)
    cg = _dot(h, win_ref[:, d:2 * d])
    v = _dot(h, win_ref[:, 2 * d:3 * d])
    ubuf_ref[CARRY_ROWS:CARRY_ROWS + tm, :] = cg * v
    conv = _conv3(ubuf_ref, cw_ref[...], tm)
    ubuf_ref[0:CARRY_ROWS, :] = ubuf_ref[tm:tm + CARRY_ROWS, :]
    bg = _dot(h, win_ref[:, 0:d])
    y = (bg * conv).astype(BF16)
    o_ref[0] = x + _dot(y, wout_ref[...])


def _mixer(x, g, w_in, conv_w, w_out, *, tm=512):
    b, s, d = x.shape
    const = lambda *_: (0, 0)
    one = pl.Buffered(1)
    return pl.pallas_call(
        functools.partial(_mixer_kernel, tm=tm),
        grid=(b, s // tm),
        in_specs=[
            pl.BlockSpec((1, tm, d), lambda i, j: (i, j, 0)),
            pl.BlockSpec((1, d), const),
            pl.BlockSpec((d, 3 * d), const, pipeline_mode=one),
            pl.BlockSpec((CONV_W, d), const),
            pl.BlockSpec((d, d), const, pipeline_mode=one),
        ],
        out_specs=pl.BlockSpec((1, tm, d), lambda i, j: (i, j, 0)),
        out_shape=jax.ShapeDtypeStruct((b, s, d), F32),
        scratch_shapes=[pltpu.VMEM((CARRY_ROWS + tm, d), F32)],
        compiler_params=pltpu.CompilerParams(
            dimension_semantics=("arbitrary", "arbitrary"), vmem_limit_bytes=VMEM_LIMIT),
        name="mixer",
    )(x, g.reshape(1, d), w_in.astype(BF16), conv_w, w_out.astype(BF16))


def _ffn_kernel(*refs, tm, tf, has_attn, final_norm):
    refs = list(refs)
    x_ref = refs.pop(0)
    a_ref = refs.pop(0) if has_attn else None
    wo_ref = refs.pop(0) if has_attn else None
    g_ref, wup_ref, cw_ref, wdn_ref = refs[:4]
    refs = refs[4:]
    gf_ref = refs.pop(0) if final_norm else None
    n_chunks = D_FF // tf
    o_ref, h_ref, carry_ref, acc_ref = refs[:4]
    u_refs = refs[4:]
    assert len(u_refs) == 2 * n_chunks

    @pl.when(pl.program_id(1) == 0)
    def _():
        carry_ref[...] = jnp.zeros(carry_ref.shape, F32)

    x = x_ref[0]
    if has_attn:
        x = x + _dot(a_ref[0], wo_ref[...])
    h_ref[...] = (_rms_scale(x) * g_ref[...]).astype(BF16)

    def up(buf_ref, c0):
        cols = slice(c0, c0 + tf)
        buf_ref[0:CARRY_ROWS, :] = carry_ref[:, cols]
        buf_ref[CARRY_ROWS:CARRY_ROWS + tm, :] = _dot(h_ref[...], wup_ref[:, cols])
        carry_ref[:, cols] = buf_ref[tm:tm + CARRY_ROWS, :]

    for c in range(n_chunks):
        up(u_refs[2 * c], c * tf)
        up(u_refs[2 * c + 1], D_FF + c * tf)

    for c in range(n_chunks):
        a = _conv3(u_refs[2 * c], cw_ref[:, c * tf:(c + 1) * tf], tm)
        gt = _conv3(u_refs[2 * c + 1], cw_ref[:, D_FF + c * tf:D_FF + (c + 1) * tf], tm)
        act = (a * jax.nn.sigmoid(a) * gt).astype(BF16)
        part = _dot(act, wdn_ref[c * tf:(c + 1) * tf, :])
        if c == 0:
            acc_ref[...] = x + part
        else:
            acc_ref[...] += part
    y = acc_ref[...]
    if final_norm:
        y = _rms_scale(y) * gf_ref[...]
    o_ref[0] = y


def _ffn(x, g, w_up, conv_w, w_down, *, attn=None, w_o=None, final_g=None, tm=512, tf=1408):
    b, s, d = x.shape
    has_attn = attn is not None
    final_norm = final_g is not None
    const = lambda *_: (0, 0)
    row = lambda i, j: (i, j, 0)
    one = pl.Buffered(1)
    args = [x]
    in_specs = [pl.BlockSpec((1, tm, d), row)]
    if has_attn:
        ka = attn.shape[-1]
        args += [attn, w_o]
        in_specs += [pl.BlockSpec((1, tm, ka), row),
                     pl.BlockSpec((ka, d), const, pipeline_mode=one)]
    args += [g.reshape(1, d), w_up.astype(BF16), conv_w, w_down.astype(BF16)]
    in_specs += [pl.BlockSpec((1, d), const),
                 pl.BlockSpec((d, 2 * D_FF), const, pipeline_mode=one),
                 pl.BlockSpec((CONV_W, 2 * D_FF), const),
                 pl.BlockSpec((D_FF, d), const, pipeline_mode=one)]
    if final_norm:
        args.append(final_g.reshape(1, d))
        in_specs.append(pl.BlockSpec((1, d), const))
    return pl.pallas_call(
        functools.partial(_ffn_kernel, tm=tm, tf=tf, has_attn=has_attn, final_norm=final_norm),
        grid=(b, s // tm),
        in_specs=in_specs,
        out_specs=pl.BlockSpec((1, tm, d), row),
        out_shape=jax.ShapeDtypeStruct((b, s, d), F32),
        scratch_shapes=[
            pltpu.VMEM((tm, d), BF16),
            pltpu.VMEM((CARRY_ROWS, 2 * D_FF), F32),
            pltpu.VMEM((tm, d), F32),
        ] + [pltpu.VMEM((CARRY_ROWS + tm, tf), F32)] * (2 * (D_FF // tf)),
        compiler_params=pltpu.CompilerParams(
            dimension_semantics=("arbitrary", "arbitrary"), vmem_limit_bytes=VMEM_LIMIT),
        name="ffn_attn" if has_attn else "ffn",
    )(*args)


def _proj_kernel(x_ref, gm_ref, gk_ref, wq_ref, wg_ref, wkc_ref, wkr_ref,
                 q_ref, gate_ref, kvc_ref, kvr_ref):
    xn = _rms_scale(x_ref[0])
    hq = (xn * gm_ref[...]).astype(BF16)
    hs = (xn * gk_ref[...]).astype(BF16)
    q_ref[0] = (_dot(hq, wq_ref[...]) * (HEAD_DIM ** -0.5)).astype(BF16)
    gates = jax.nn.sigmoid(_dot(hq, wg_ref[...]))
    gate_ref[0] = gates
    kvc = _dot(hs, wkc_ref[...])
    for sl in range(2 * N_GROUPS * HEAD_DIM // LANES):
        kvc_ref[0, sl] = kvc[:, sl * LANES:(sl + 1) * LANES]
    kvr_ref[0] = _dot(hs, wkr_ref[...]).astype(BF16)


def _proj(x, g_mix, g_kv, w_qg, w_kv, *, tm=512):
    b, s, d = x.shape
    hd = N_HEADS * HEAD_DIM
    gd = N_GROUPS * HEAD_DIM
    w_q = w_qg[:, :hd].astype(BF16)
    w_g = jnp.pad(w_qg[:, hd:], ((0, 0), (0, LANES - 3 * N_HEADS))).astype(BF16)
    w_kc = w_kv[:, :2 * gd].astype(BF16)
    w_kr = (w_kv[:, 2 * gd:].reshape(d, 4, N_GROUPS, HEAD_DIM)
            .transpose(0, 2, 1, 3).reshape(d, 4 * gd).astype(BF16))
    const = lambda *_: (0, 0)
    row = lambda i, j: (i, j, 0)
    one = pl.Buffered(1)
    return pl.pallas_call(
        _proj_kernel,
        grid=(b, s // tm),
        in_specs=[
            pl.BlockSpec((1, tm, d), row),
            pl.BlockSpec((1, d), const),
            pl.BlockSpec((1, d), const),
            pl.BlockSpec((d, hd), const, pipeline_mode=one),
            pl.BlockSpec((d, LANES), const, pipeline_mode=one),
            pl.BlockSpec((d, 2 * gd), const, pipeline_mode=one),
            pl.BlockSpec((d, 4 * gd), const, pipeline_mode=one),
        ],
        out_specs=[
            pl.BlockSpec((1, tm, hd), row),
            pl.BlockSpec((1, tm, LANES), row),
            pl.BlockSpec((1, 2 * gd // LANES, tm, LANES), lambda i, j: (i, 0, j, 0)),
            pl.BlockSpec((1, tm, 4 * gd), row),
        ],
        out_shape=[
            jax.ShapeDtypeStruct((b, s, hd), BF16),
            jax.ShapeDtypeStruct((b, s, LANES), F32),
            jax.ShapeDtypeStruct((b, 2 * gd // LANES, s, LANES), F32),
            jax.ShapeDtypeStruct((b, s, 4 * gd), BF16),
        ],
        compiler_params=pltpu.CompilerParams(
            dimension_semantics=("arbitrary", "arbitrary"), vmem_limit_bytes=VMEM_LIMIT),
        name="proj",
    )(x, g_mix.reshape(1, d), g_kv.reshape(1, d), w_q, w_g, w_kc, w_kr)


def _gelu_tanh(x):
    return 0.5 * x * (1.0 + jnp.tanh(math.sqrt(2.0 / math.pi) * (x + 0.044715 * (x * x * x))))


def _compress_kernel(k_ref, v_ref, pe_ref, w1_ref, b1_ref, w2_ref, o_ref):
    n_chunks = int(o_ref.shape[2])
    outs = [None, None]
    for j, src in enumerate((k_ref, v_ref)):
        first = None
        second = None
        for l in range(CMP_STRIDE):
            x = src.at[0, 0][pl.ds(l, n_chunks, stride=CMP_STRIDE), :]
            pa = _dot((x + pe_ref[j, l:l + 1, :]).astype(BF16), w1_ref[j, l])
            pb = _dot((x + pe_ref[j, CMP_STRIDE + l:CMP_STRIDE + l + 1, :]).astype(BF16),
                      w1_ref[j, CMP_STRIDE + l])
            first = pa if first is None else first + pa
            second = pb if second is None else second + pb
        pre = first + pltpu.roll(second, n_chunks - 1, 0) + b1_ref[j]
        hid = _gelu_tanh(pre).astype(BF16)
        for e in range(2):
            part = _dot(hid[:, e * CMP_HIDDEN:(e + 1) * CMP_HIDDEN], w2_ref[j])
            outs[e] = part if outs[e] is None else outs[e] + part
    for e in range(2):
        o_ref[0, e] = outs[e].astype(BF16)


def _compress(kvc, cmp_pe, cmp_w1, cmp_b1, cmp_w2):
    b, n_slabs, s, _ = kvc.shape
    n_chunks = s // CMP_STRIDE
    pairs = N_GROUPS // 2
    assert n_slabs == 2 * pairs and 2 * HEAD_DIM == LANES
    pe = jnp.concatenate([cmp_pe, cmp_pe], axis=-1)
    w1 = cmp_w1.reshape(2, CMP_LEN, HEAD_DIM, CMP_HIDDEN)
    zero = jnp.zeros_like(w1)
    w1 = jnp.concatenate([jnp.concatenate([w1, zero], axis=-1),
                          jnp.concatenate([zero, w1], axis=-1)], axis=2).astype(BF16)
    b1 = jnp.concatenate([cmp_b1, cmp_b1], axis=-1).reshape(2, 1, 2 * CMP_HIDDEN)
    w2 = jnp.stack([jnp.pad(cmp_w2[0], ((0, 0), (0, HEAD_DIM))),
                    jnp.pad(cmp_w2[1], ((0, 0), (HEAD_DIM, 0)))]).astype(BF16)
    const = lambda *_: (0, 0, 0)
    return pl.pallas_call(
        _compress_kernel,
        grid=(b, pairs),
        in_specs=[
            pl.BlockSpec((1, 1, s, LANES), lambda i, p: (i, p, 0, 0)),
            pl.BlockSpec((1, 1, s, LANES), lambda i, p: (i, pairs + p, 0, 0)),
            pl.BlockSpec((2, CMP_LEN, LANES), const),
            pl.BlockSpec((2, CMP_LEN, LANES, 2 * CMP_HIDDEN), lambda *_: (0, 0, 0, 0)),
            pl.BlockSpec((2, 1, 2 * CMP_HIDDEN), const),
            pl.BlockSpec((2, CMP_HIDDEN, LANES), const),
        ],
        out_specs=pl.BlockSpec((1, 2, n_chunks, LANES), lambda i, p: (i, p, 0, 0)),
        out_shape=jax.ShapeDtypeStruct((b, N_GROUPS, n_chunks, LANES), BF16),
        compiler_params=pltpu.CompilerParams(
            dimension_semantics=("arbitrary", "arbitrary"), vmem_limit_bytes=VMEM_LIMIT),
        name="compress",
    )(kvc, kvc, pe, w1, b1, w2)


TILE_HIDDEN, TILE_PREV2, TILE_PREV, TILE_DIAG = 0, 1, 2, 3
N_KINDS = 4
CMP_WIN = 32
LOG2E = 1.0 / math.log(2.0)


def _rel_bucket_np(dist):
    max_exact = N_BUCKETS // 2
    d = np.maximum(dist, 0)
    df = np.maximum(d, 1).astype(np.float32)
    large = max_exact + (np.log(df / max_exact) / np.float32(math.log(MAX_DISTANCE / max_exact))
                         * (N_BUCKETS - max_exact)).astype(np.int32)
    return np.where(d < max_exact, d, np.minimum(large, N_BUCKETS - 1))


def _toeplitz(w):
    t = w.shape[-1] // 2
    lead = w.shape[:-1]
    a = jnp.broadcast_to(w[..., None, :], lead + (t, 2 * t)).reshape(lead + (2 * t * t,))
    return a[..., :t * (2 * t - 1)].reshape(lead + (t, 2 * t - 1))[..., :t]


def _attn_tables(rel_bias, s):
    nq = s // TQ
    assert _rel_bucket_np(np.arange(TQ // 2, 4 * TQ)).min() == N_BUCKETS - 1
    relb = (rel_bias.astype(F32) - rel_bias[N_BUCKETS - 1].astype(F32)).T
    f = jnp.take(relb, jnp.asarray(_rel_bucket_np(np.arange(2 * TQ))), axis=1)
    neg = jnp.full((N_HEADS, TQ), NEG, F32)
    w_diag = jnp.concatenate([f[:, 0:1], neg, jnp.flip(f[:, 1:TQ], axis=1)], axis=1)
    w_prev = jnp.concatenate([jnp.flip(f[:, 1:TQ + 1], axis=1), jnp.zeros((N_HEADS, 1), F32),
                              jnp.flip(f[:, TQ + 1:2 * TQ], axis=1)], axis=1)
    ti = np.arange(TQ)[:, None]
    ki = np.arange(TQ)[None, :]
    prev2 = jnp.broadcast_to(jnp.asarray(np.where(ki > ti, 0.0, NEG), F32), (N_HEADS, TQ, TQ))
    hidden = jnp.full((N_HEADS, TQ, TQ), NEG, F32)
    b3 = jnp.stack([hidden, prev2, _toeplitz(w_prev), _toeplitz(w_diag)], axis=1)
    b3 = (b3.reshape(N_GROUPS, HPG, N_KINDS, TQ, TQ).transpose(0, 2, 1, 3, 4)
          .reshape(N_GROUPS, N_KINDS, HPG * TQ, TQ))
    n_cmp = s // CMP_STRIDE
    lead = CMP_WIN // 4
    dist = ti - CMP_STRIDE * (np.arange(CMP_WIN)[None, :] - lead) - (CMP_LEN - 1)
    pw = jnp.take(relb, jnp.asarray(_rel_bucket_np(dist)), axis=1)
    pw = jnp.where(jnp.asarray(dist >= 0), pw, NEG)
    full = jnp.concatenate([jnp.zeros((N_HEADS, TQ, n_cmp - lead), F32), pw,
                            jnp.full((N_HEADS, TQ, n_cmp), NEG, F32)], axis=-1)
    per_tile = TQ // CMP_STRIDE
    bc = jnp.stack([full[..., n_cmp - per_tile * t:2 * n_cmp - per_tile * t] for t in range(nq)])
    bc = bc.reshape(nq, N_GROUPS, HPG * TQ, n_cmp)
    return b3, bc


def _attn_consts(s):
    n_slc = s // SLC_LEN
    n_cmp = s // CMP_STRIDE
    key = np.arange(s)[:, None]
    lane = np.arange(LANES)[None, :]
    ceneg = np.where((key // SLC_LEN == lane) & (lane < n_slc), NEG, 0.0)
    j = np.arange(n_slc)[:, None]
    i = np.arange(n_cmp)[None, :]
    ov = ((i * CMP_STRIDE < j * SLC_LEN + SLC_LEN) & (i * CMP_STRIDE + CMP_LEN > j * SLC_LEN)
          & (i < n_cmp - 1))
    place = np.eye(n_slc, LANES)
    return (jnp.asarray(ceneg, BF16), jnp.ones((3 * TQ, LANES), BF16),
            jnp.asarray(np.concatenate([ov] * 3, axis=1), BF16), jnp.asarray(place, BF16))


def _attn_kernel(q_ref, gate_ref, kv_ref, kvc_ref, b3_ref, bc_ref, ceneg_ref, ones_ref,
                 ov_ref, place_ref, o_ref, q4_ref, sc_ref, nsc_ref, macc_ref, mb_ref, oacc_ref,
                 comb_ref):
    qb = pl.program_id(2)
    n_slc = ov_ref.shape[0]
    rows = HPG * TQ
    lane = lax.broadcasted_iota(jnp.int32, (TQ, LANES), 1)
    low = lane < HEAD_DIM

    for pair in range(HPG // 2):
        qp = q_ref[0, :, pair * LANES:(pair + 1) * LANES].astype(F32)
        q4_ref[(2 * pair) * TQ:(2 * pair + 1) * TQ, 0:LANES] = jnp.where(low, qp, 0.0).astype(BF16)
        q4_ref[(2 * pair + 1) * TQ:(2 * pair + 2) * TQ, 0:LANES] = (
            jnp.where(low, pltpu.roll(qp, HEAD_DIM, 1), 0.0).astype(BF16))
    q4 = q4_ref[:, 0:LANES]
    ones = ones_ref[...]

    def key_rows(kt, col0):
        k0 = pl.multiple_of(kt * TQ, TQ)
        return kv_ref[0, pl.ds(k0, TQ), col0:col0 + LANES]

    def lane_tiles(x):
        return [x[:, i:i + LANES] for i in range(0, x.shape[1], LANES)]

    def tile_max(x):
        return functools.reduce(jnp.maximum, lane_tiles(x))

    def exp2_rel(sv, mb):
        return jnp.concatenate([jnp.exp2(t - mb) for t in lane_tiles(sv)], axis=1).astype(BF16)

    kt1 = jnp.maximum(qb - 1, 0)
    kt2 = jnp.maximum(qb - 2, 0)
    kind1 = jnp.where(qb >= 1, TILE_PREV, TILE_HIDDEN)
    kind2 = jnp.where(qb >= 2, TILE_PREV2, TILE_HIDDEN)

    def split3(x):
        hi = x.astype(BF16)
        r1 = x - hi.astype(F32)
        mid = r1.astype(BF16)
        lo = (r1 - mid.astype(F32)).astype(BF16)
        return jnp.concatenate([hi, mid, lo], axis=1)

    def weighted(o, hh, br):
        r = slice(hh * TQ, (hh + 1) * TQ)
        l = o[r, LANES:2 * LANES]
        gate = gate_ref[0, 0, :, 3 * hh + br:3 * hh + br + 1]
        return o[r, 0:LANES] * (gate / jnp.where(l > 0.0, l, 1.0))

    tiles_w = [key_rows(kt2, LANES), key_rows(kt1, LANES), key_rows(qb, LANES)]
    kinds_w = [kind2, kind1, TILE_DIAG]

    def win_scores(i):
        return (_split_rows(_dot_nt, q4, tiles_w[i]) + b3_ref[0, kinds_w[i]]) * LOG2E

    kvc = kvc_ref[0, 0]
    bias_c = bc_ref[0, 0]
    s_c = _split_rows(_dot_nt, q4, kvc) + bias_c
    s_w0 = win_scores(0)
    m_c = jnp.max(s_c, axis=1, keepdims=True)
    e_c = jnp.where(bias_c > 0.5 * NEG, jnp.exp(s_c - m_c), 0.0)
    o_c = _split_rows(_dot, e_c.astype(BF16),
                      jnp.concatenate([kvc, ones[0:kvc.shape[0]]], axis=1))
    s_w1 = win_scores(1)
    l_c = o_c[:, LANES:2 * LANES]
    p_c = e_c / jnp.where(l_c > 0.0, l_c, 1.0)
    psum = p_c[0:TQ]
    for hh in range(1, HPG):
        psum = psum + p_c[hh * TQ:(hh + 1) * TQ]

    imp_t = _dot_nt(ov_ref[...], split3(psum))
    s_w2 = win_scores(2)
    s_w = jnp.concatenate([s_w0, s_w1, s_w2], axis=1)
    m_w = jnp.max(tile_max(s_w), axis=1, keepdims=True)
    p_w = exp2_rel(s_w, jnp.broadcast_to(m_w, (rows, LANES)))
    o_w = _split_rows(_dot, p_w,
                      jnp.concatenate([jnp.concatenate(tiles_w, axis=0), ones], axis=1))

    jrow = lax.broadcasted_iota(jnp.int32, (n_slc, TQ), 0)
    tpos = qb * TQ + lax.broadcasted_iota(jnp.int32, (n_slc, TQ), 1)
    cur = lax.shift_right_logical(tpos, int(math.log2(SLC_LEN)))
    forced = (jrow == 0) | (jrow == cur) | (jrow == cur - 1)
    score = jnp.where(forced, 3e38, jnp.where(jrow <= cur, imp_t, -1.0))
    rank = jnp.zeros((n_slc, TQ), F32)
    for k in range(n_slc):
        sk = score[k:k + 1, :]
        beats = (sk > score) | ((sk == score) & (jrow > k))
        rank = rank + beats.astype(F32)
    notsel_t = (rank >= float(N_SEL)).astype(BF16)
    q_mask = lax.dot_general(notsel_t, place_ref[...], (((0,), (0,)), ((), ())),
                             preferred_element_type=F32).astype(BF16)
    for hh in range(HPG):
        q4_ref[hh * TQ:(hh + 1) * TQ, LANES:2 * LANES] = q_mask

    for hh in range(HPG):
        comb_ref[hh * TQ:(hh + 1) * TQ, :] = weighted(o_c, hh, 0) + weighted(o_w, hh, 2)

    n_far = jnp.maximum(qb - 1, 0)
    n_pairs = lax.shift_right_logical(n_far + 1, 1)
    pair = 2 * TQ
    rhs_n = jnp.concatenate([key_rows(kt1, 0), key_rows(qb, 0)], axis=0)
    k1 = pl.multiple_of(kt1 * TQ, TQ)
    kq = pl.multiple_of(qb * TQ, TQ)
    cen_n = jnp.concatenate([ceneg_ref[pl.ds(k1, TQ), :], ceneg_ref[pl.ds(kq, TQ), :]], axis=0)
    bias_n = jnp.concatenate([b3_ref[0, kind1], b3_ref[0, TILE_DIAG]], axis=1)
    s_n = (_dot_nt(q4_ref[...], jnp.concatenate([rhs_n, cen_n], axis=1)) + bias_n) * LOG2E
    nsc_ref[...] = s_n
    macc_ref[...] = tile_max(s_n)

    def far_scores(p, carry):
        k0 = pl.multiple_of(p * pair, pair)
        rhs = jnp.concatenate([kv_ref[0, pl.ds(k0, pair), 0:LANES], ceneg_ref[pl.ds(k0, pair), :]],
                              axis=1)
        sv = _dot_nt(q4_ref[...], rhs) * LOG2E
        hide = jnp.where(2 * p + 1 < n_far, 0.0, NEG)
        sv = jnp.concatenate([sv[:, :TQ], sv[:, TQ:] + hide], axis=1)
        sc_ref[p] = sv
        macc_ref[...] = jnp.maximum(macc_ref[...], tile_max(sv))
        return carry

    lax.fori_loop(0, n_pairs, far_scores, 0)

    m_s = jnp.max(macc_ref[...], axis=1, keepdims=True)
    mb_ref[...] = jnp.broadcast_to(m_s, (rows, LANES))
    oacc_ref[...] = jnp.zeros((rows, 2 * LANES), F32)

    def far_values(p, carry):
        k0 = pl.multiple_of(p * pair, pair)
        rhs = jnp.concatenate([kv_ref[0, pl.ds(k0, pair), 0:LANES], ones[0:pair]], axis=1)
        for r in (slice(0, rows // 2), slice(rows // 2, rows)):
            oacc_ref[r, :] += _dot(exp2_rel(sc_ref[p, r, :], mb_ref[r, :]), rhs)
        return carry

    lax.fori_loop(0, n_pairs, far_values, 0)
    p_n = exp2_rel(nsc_ref[...], mb_ref[...])
    rhs_v = jnp.concatenate([key_rows(kt1, 0), key_rows(qb, 0)], axis=0)
    o_s = oacc_ref[...] + _split_rows(_dot, p_n, jnp.concatenate([rhs_v, ones[0:pair]], axis=1))

    for hp in range(HPG // 2):
        even, odd = [comb_ref[hh * TQ:(hh + 1) * TQ, :] + weighted(o_s, hh, 1)
                     for hh in (2 * hp, 2 * hp + 1)]
        o_ref[0, :, hp * LANES:(hp + 1) * LANES] = (
            jnp.where(low, pltpu.roll(even, HEAD_DIM, 1), odd).astype(BF16))


def _attention(q, gates, kvr, kvcmp, rel_bias):
    b, s, hd = q.shape
    nq = s // TQ
    n_slc = s // SLC_LEN
    n_cmp = s // CMP_STRIDE
    rows = HPG * TQ
    b3, bc = _attn_tables(rel_bias, s)
    ceneg, ones, ov, place = _attn_consts(s)
    gw = 3 * HPG
    gates_g = (gates[:, :, :3 * N_HEADS].reshape(b, s, N_GROUPS, gw).transpose(0, 2, 1, 3))
    gl = HPG * HEAD_DIM
    return pl.pallas_call(
        _attn_kernel,
        grid=(b, N_GROUPS, nq),
        in_specs=[
            pl.BlockSpec((1, TQ, gl), lambda i, g, t: (i, t, g)),
            pl.BlockSpec((1, 1, TQ, gw), lambda i, g, t: (i, g, t, 0)),
            pl.BlockSpec((1, s, 4 * HEAD_DIM), lambda i, g, t: (i, 0, g)),
            pl.BlockSpec((1, 1, n_cmp, LANES), lambda i, g, t: (i, g, 0, 0)),
            pl.BlockSpec((1, N_KINDS, rows, TQ), lambda i, g, t: (g, 0, 0, 0)),
            pl.BlockSpec((1, 1, rows, n_cmp), lambda i, g, t: (t, g, 0, 0)),
            pl.BlockSpec((s, LANES), lambda *_: (0, 0)),
            pl.BlockSpec((3 * TQ, LANES), lambda *_: (0, 0)),
            pl.BlockSpec((n_slc, 3 * n_cmp), lambda *_: (0, 0)),
            pl.BlockSpec((n_slc, LANES), lambda *_: (0, 0)),
        ],
        out_specs=pl.BlockSpec((1, TQ, gl), lambda i, g, t: (i, t, g)),
        out_shape=jax.ShapeDtypeStruct((b, s, N_HEADS * HEAD_DIM), BF16),
        scratch_shapes=[
            pltpu.VMEM((rows, 2 * LANES), BF16),
            pltpu.VMEM(((nq - 1) // 2, rows, 2 * TQ), F32),
            pltpu.VMEM((rows, 2 * TQ), F32),
            pltpu.VMEM((rows, LANES), F32),
            pltpu.VMEM((rows, LANES), F32),
            pltpu.VMEM((rows, 2 * LANES), F32),
            pltpu.VMEM((rows, LANES), F32),
        ],
        compiler_params=pltpu.CompilerParams(
            dimension_semantics=("arbitrary", "arbitrary", "arbitrary"),
            vmem_limit_bytes=VMEM_LIMIT),
        name="nsa_attn",
    )(q, gates_g, kvr, kvcmp, b3, bc, ceneg, ones, ov, place)


def kernel(x, mix_norm, a_w_in, a_conv, a_w_out, ffn_norm, ffn_up, ffn_conv, ffn_down,
           kv_norm, w_kv, cmp_pe, cmp_w1, cmp_b1, cmp_w2, b_w_qg, b_w_o, rel_bias, final_norm):
    b, s, d = x.shape
    assert d == D_MODEL and s % 512 == 0 and mix_norm.shape[0] == 2
    x = _mixer(x, mix_norm[0], a_w_in[0], a_conv[0], a_w_out[0])
    x = _ffn(x, ffn_norm[0], ffn_up[0], ffn_conv[0], ffn_down[0])
    q, gates, kvc, kvr = _proj(x, mix_norm[1], kv_norm, b_w_qg[0], w_kv)
    kvcmp = _compress(kvc, cmp_pe, cmp_w1, cmp_b1, cmp_w2)
    attn = _attention(q, gates, kvr, kvcmp, rel_bias)
    return _ffn(x, ffn_norm[1], ffn_up[1], ffn_conv[1], ffn_down[1],
                attn=attn, w_o=b_w_o[0].astype(BF16), final_g=final_norm)
```

```python
import functools
import math

import jax
import jax.numpy as jnp
import numpy as np
from jax import lax
from jax.experimental import pallas as pl
from jax.experimental.pallas import tpu as pltpu

D_MODEL = 1024
CONV_W = 3
D_FF = 2816
N_HEADS = 16
N_GROUPS = 4
HPG = N_HEADS // N_GROUPS
HEAD_DIM = 64
CMP_LEN = 32
CMP_STRIDE = 16
CMP_HIDDEN = 128
SLC_LEN = 64
N_SEL = 16
WINDOW = 512
N_BUCKETS = 32
MAX_DISTANCE = 128
EPS = 1e-6
NEG = -1e30

LANES = 128
V7X_VMEM_BYTES = 64 * 2**20
VMEM_LIMIT = V7X_VMEM_BYTES - 8 * 2**20

TQ = 256
CARRY_ROWS = 8
F32 = jnp.float32
BF16 = jnp.bfloat16


def _dot(a, b):
    return jnp.dot(a, b, preferred_element_type=F32)


def _dot_nt(a, b, precision=None):
    return lax.dot_general(a, b, (((1,), (1,)), ((), ())),
                           preferred_element_type=F32, precision=precision)


def _split_rows(dot, a, b):
    h = a.shape[0] // 2
    return jnp.concatenate([dot(a[:h], b), dot(a[h:], b)], axis=0)


def _rms_scale(x):
    return x * lax.rsqrt(jnp.mean(x * x, axis=-1, keepdims=True) + EPS)


def _conv3(buf_ref, cw, rows):
    c = CARRY_ROWS
    return (cw[0:1, :] * buf_ref[c - 2:c - 2 + rows, :]
            + cw[1:2, :] * buf_ref[c - 1:c - 1 + rows, :]
            + cw[2:3, :] * buf_ref[c:c + rows, :])


def _mixer_kernel(x_ref, g_ref, win_ref, cw_ref, wout_ref, o_ref, ubuf_ref, *, tm):
    d = D_MODEL

    @pl.when(pl.program_id(1) == 0)
    def _():
        ubuf_ref[0:CARRY_ROWS, :] = jnp.zeros((CARRY_ROWS, d), F32)

    x = x_ref[0]
    h = (_rms_scale(x) * g_ref[...]).astype(BF16)
    cg = _dot(h, win_ref[:, d:2 * d])
    v = _dot(h, win_ref[:, 2 * d:3 * d])
    ubuf_ref[CARRY_ROWS:CARRY_ROWS + tm, :] = cg * v
    conv = _conv3(ubuf_ref, cw_ref[...], tm)
    ubuf_ref[0:CARRY_ROWS, :] = ubuf_ref[tm:tm + CARRY_ROWS, :]
    bg = _dot(h, win_ref[:, 0:d])
    y = (bg * conv).astype(BF16)
    o_ref[0] = x + _dot(y, wout_ref[...])


def _mixer(x, g, w_in, conv_w, w_out, *, tm=512):
    b, s, d = x.shape
    const = lambda *_: (0, 0)
    one = pl.Buffered(1)
    return pl.pallas_call(
        functools.partial(_mixer_kernel, tm=tm),
        grid=(b, s // tm),
        in_specs=[
            pl.BlockSpec((1, tm, d), lambda i, j: (i, j, 0)),
            pl.BlockSpec((1, d), const),
            pl.BlockSpec((d, 3 * d), const, pipeline_mode=one),
            pl.BlockSpec((CONV_W, d), const),
            pl.BlockSpec((d, d), const, pipeline_mode=one),
        ],
        out_specs=pl.BlockSpec((1, tm, d), lambda i, j: (i, j, 0)),
        out_shape=jax.ShapeDtypeStruct((b, s, d), F32),
        scratch_shapes=[pltpu.VMEM((CARRY_ROWS + tm, d), F32)],
        compiler_params=pltpu.CompilerParams(
            dimension_semantics=("arbitrary", "arbitrary"), vmem_limit_bytes=VMEM_LIMIT),
        name="mixer",
    )(x, g.reshape(1, d), w_in.astype(BF16), conv_w, w_out.astype(BF16))


def _ffn_kernel(*refs, tm, tf, has_attn, final_norm):
    refs = list(refs)
    x_ref = refs.pop(0)
    a_ref = refs.pop(0) if has_attn else None
    wo_ref = refs.pop(0) if has_attn else None
    g_ref, wup_ref, cw_ref, wdn_ref = refs[:4]
    refs = refs[4:]
    gf_ref = refs.pop(0) if final_norm else None
    n_chunks = D_FF // tf
    o_ref, h_ref, carry_ref, acc_ref = refs[:4]
    u_refs = refs[4:]
    assert len(u_refs) == 2 * n_chunks

    @pl.when(pl.program_id(1) == 0)
    def _():
        carry_ref[...] = jnp.zeros(carry_ref.shape, F32)

    x = x_ref[0]
    if has_attn:
        x = x + _dot(a_ref[0], wo_ref[...])
    h_ref[...] = (_rms_scale(x) * g_ref[...]).astype(BF16)

    def up(buf_ref, c0):
        cols = slice(c0, c0 + tf)
        buf_ref[0:CARRY_ROWS, :] = carry_ref[:, cols]
        buf_ref[CARRY_ROWS:CARRY_ROWS + tm, :] = _dot(h_ref[...], wup_ref[:, cols])
        carry_ref[:, cols] = buf_ref[tm:tm + CARRY_ROWS, :]

    for c in range(n_chunks):
        up(u_refs[2 * c], c * tf)
        up(u_refs[2 * c + 1], D_FF + c * tf)

    for c in range(n_chunks):
        a = _conv3(u_refs[2 * c], cw_ref[:, c * tf:(c + 1) * tf], tm)
        gt = _conv3(u_refs[2 * c + 1], cw_ref[:, D_FF + c * tf:D_FF + (c + 1) * tf], tm)
        act = (a * jax.nn.sigmoid(a) * gt).astype(BF16)
        part = _dot(act, wdn_ref[c * tf:(c + 1) * tf, :])
        if c == 0:
            acc_ref[...] = x + part
        else:
            acc_ref[...] += part
    y = acc_ref[...]
    if final_norm:
        y = _rms_scale(y) * gf_ref[...]
    o_ref[0] = y


def _ffn(x, g, w_up, conv_w, w_down, *, attn=None, w_o=None, final_g=None, tm=512, tf=1408):
    b, s, d = x.shape
    has_attn = attn is not None
    final_norm = final_g is not None
    const = lambda *_: (0, 0)
    row = lambda i, j: (i, j, 0)
    one = pl.Buffered(1)
    args = [x]
    in_specs = [pl.BlockSpec((1, tm, d), row)]
    if has_attn:
        ka = attn.shape[-1]
        args += [attn, w_o]
        in_specs += [pl.BlockSpec((1, tm, ka), row),
                     pl.BlockSpec((ka, d), const, pipeline_mode=one)]
    args += [g.reshape(1, d), w_up.astype(BF16), conv_w, w_down.astype(BF16)]
    in_specs += [pl.BlockSpec((1, d), const),
                 pl.BlockSpec((d, 2 * D_FF), const, pipeline_mode=one),
                 pl.BlockSpec((CONV_W, 2 * D_FF), const),
                 pl.BlockSpec((D_FF, d), const, pipeline_mode=one)]
    if final_norm:
        args.append(final_g.reshape(1, d))
        in_specs.append(pl.BlockSpec((1, d), const))
    return pl.pallas_call(
        functools.partial(_ffn_kernel, tm=tm, tf=tf, has_attn=has_attn, final_norm=final_norm),
        grid=(b, s // tm),
        in_specs=in_specs,
        out_specs=pl.BlockSpec((1, tm, d), row),
        out_shape=jax.ShapeDtypeStruct((b, s, d), F32),
        scratch_shapes=[
            pltpu.VMEM((tm, d), BF16),
            pltpu.VMEM((CARRY_ROWS, 2 * D_FF), F32),
            pltpu.VMEM((tm, d), F32),
        ] + [pltpu.VMEM((CARRY_ROWS + tm, tf), F32)] * (2 * (D_FF // tf)),
        compiler_params=pltpu.CompilerParams(
            dimension_semantics=("arbitrary", "arbitrary"), vmem_limit_bytes=VMEM_LIMIT),
        name="ffn_attn" if has_attn else "ffn",
    )(*args)


def _proj_kernel(x_ref, gm_ref, gk_ref, wq_ref, wg_ref, wkc_ref, wkr_ref,
                 q_ref, gate_ref, kvc_ref, kvr_ref):
    xn = _rms_scale(x_ref[0])
    hq = (xn * gm_ref[...]).astype(BF16)
    hs = (xn * gk_ref[...]).astype(BF16)
    q_ref[0] = (_dot(hq, wq_ref[...]) * (HEAD_DIM ** -0.5)).astype(BF16)
    gates = jax.nn.sigmoid(_dot(hq, wg_ref[...]))
    gate_ref[0] = gates
    kvc = _dot(hs, wkc_ref[...])
    for sl in range(2 * N_GROUPS * HEAD_DIM // LANES):
        kvc_ref[0, sl] = kvc[:, sl * LANES:(sl + 1) * LANES]
    kvr_ref[0] = _dot(hs, wkr_ref[...]).astype(BF16)


def _proj(x, g_mix, g_kv, w_qg, w_kv, *, tm=512):
    b, s, d = x.shape
    hd = N_HEADS * HEAD_DIM
    gd = N_GROUPS * HEAD_DIM
    w_q = w_qg[:, :hd].astype(BF16)
    w_g = jnp.pad(w_qg[:, hd:], ((0, 0), (0, LANES - 3 * N_HEADS))).astype(BF16)
    w_kc = w_kv[:, :2 * gd].astype(BF16)
    w_kr = (w_kv[:, 2 * gd:].reshape(d, 4, N_GROUPS, HEAD_DIM)
            .transpose(0, 2, 1, 3).reshape(d, 4 * gd).astype(BF16))
    const = lambda *_: (0, 0)
    row = lambda i, j: (i, j, 0)
    one = pl.Buffered(1)
    return pl.pallas_call(
        _proj_kernel,
        grid=(b, s // tm),
        in_specs=[
            pl.BlockSpec((1, tm, d), row),
            pl.BlockSpec((1, d), const),
            pl.BlockSpec((1, d), const),
            pl.BlockSpec((d, hd), const, pipeline_mode=one),
            pl.BlockSpec((d, LANES), const, pipeline_mode=one),
            pl.BlockSpec((d, 2 * gd), const, pipeline_mode=one),
            pl.BlockSpec((d, 4 * gd), const, pipeline_mode=one),
        ],
        out_specs=[
            pl.BlockSpec((1, tm, hd), row),
            pl.BlockSpec((1, tm, LANES), row),
            pl.BlockSpec((1, 2 * gd // LANES, tm, LANES), lambda i, j: (i, 0, j, 0)),
            pl.BlockSpec((1, tm, 4 * gd), row),
        ],
        out_shape=[
            jax.ShapeDtypeStruct((b, s, hd), BF16),
            jax.ShapeDtypeStruct((b, s, LANES), F32),
            jax.ShapeDtypeStruct((b, 2 * gd // LANES, s, LANES), F32),
            jax.ShapeDtypeStruct((b, s, 4 * gd), BF16),
        ],
        compiler_params=pltpu.CompilerParams(
            dimension_semantics=("arbitrary", "arbitrary"), vmem_limit_bytes=VMEM_LIMIT),
        name="proj",
    )(x, g_mix.reshape(1, d), g_kv.reshape(1, d), w_q, w_g, w_kc, w_kr)


def _gelu_tanh(x):
    return 0.5 * x * (1.0 + jnp.tanh(math.sqrt(2.0 / math.pi) * (x + 0.044715 * (x * x * x))))


def _compress_kernel(k_ref, v_ref, pe_ref, w1_ref, b1_ref, w2_ref, o_ref):
    n_chunks = int(o_ref.shape[2])
    outs = [None, None]
    for j, src in enumerate((k_ref, v_ref)):
        first = None
        second = None
        for l in range(CMP_STRIDE):
            x = src.at[0, 0][pl.ds(l, n_chunks, stride=CMP_STRIDE), :]
            pa = _dot((x + pe_ref[j, l:l + 1, :]).astype(BF16), w1_ref[j, l])
            pb = _dot((x + pe_ref[j, CMP_STRIDE + l:CMP_STRIDE + l + 1, :]).astype(BF16),
                      w1_ref[j, CMP_STRIDE + l])
            first = pa if first is None else first + pa
            second = pb if second is None else second + pb
        pre = first + pltpu.roll(second, n_chunks - 1, 0) + b1_ref[j]
        hid = _gelu_tanh(pre).astype(BF16)
        for e in range(2):
            part = _dot(hid[:, e * CMP_HIDDEN:(e + 1) * CMP_HIDDEN], w2_ref[j])
            outs[e] = part if outs[e] is None else outs[e] + part
    for e in range(2):
        o_ref[0, e] = outs[e].astype(BF16)


def _compress(kvc, cmp_pe, cmp_w1, cmp_b1, cmp_w2):
    b, n_slabs, s, _ = kvc.shape
    n_chunks = s // CMP_STRIDE
    pairs = N_GROUPS // 2
    assert n_slabs == 2 * pairs and 2 * HEAD_DIM == LANES
    pe = jnp.concatenate([cmp_pe, cmp_pe], axis=-1)
    w1 = cmp_w1.reshape(2, CMP_LEN, HEAD_DIM, CMP_HIDDEN)
    zero = jnp.zeros_like(w1)
    w1 = jnp.concatenate([jnp.concatenate([w1, zero], axis=-1),
                          jnp.concatenate([zero, w1], axis=-1)], axis=2).astype(BF16)
    b1 = jnp.concatenate([cmp_b1, cmp_b1], axis=-1).reshape(2, 1, 2 * CMP_HIDDEN)
    w2 = jnp.stack([jnp.pad(cmp_w2[0], ((0, 0), (0, HEAD_DIM))),
                    jnp.pad(cmp_w2[1], ((0, 0), (HEAD_DIM, 0)))]).astype(BF16)
    const = lambda *_: (0, 0, 0)
    return pl.pallas_call(
        _compress_kernel,
        grid=(b, pairs),
        in_specs=[
            pl.BlockSpec((1, 1, s, LANES), lambda i, p: (i, p, 0, 0)),
            pl.BlockSpec((1, 1, s, LANES), lambda i, p: (i, pairs + p, 0, 0)),
            pl.BlockSpec((2, CMP_LEN, LANES), const),
            pl.BlockSpec((2, CMP_LEN, LANES, 2 * CMP_HIDDEN), lambda *_: (0, 0, 0, 0)),
            pl.BlockSpec((2, 1, 2 * CMP_HIDDEN), const),
            pl.BlockSpec((2, CMP_HIDDEN, LANES), const),
        ],
        out_specs=pl.BlockSpec((1, 2, n_chunks, LANES), lambda i, p: (i, p, 0, 0)),
        out_shape=jax.ShapeDtypeStruct((b, N_GROUPS, n_chunks, LANES), BF16),
        compiler_params=pltpu.CompilerParams(
            dimension_semantics=("arbitrary", "arbitrary"), vmem_limit_bytes=VMEM_LIMIT),
        name="compress",
    )(kvc, kvc, pe, w1, b1, w2)


TILE_HIDDEN, TILE_PREV2, TILE_PREV, TILE_DIAG = 0, 1, 2, 3
N_KINDS = 4
CMP_WIN = 32
LOG2E = 1.0 / math.log(2.0)


def _rel_bucket_np(dist):
    max_exact = N_BUCKETS // 2
    d = np.maximum(dist, 0)
    df = np.maximum(d, 1).astype(np.float32)
    large = max_exact + (np.log(df / max_exact) / np.float32(math.log(MAX_DISTANCE / max_exact))
                         * (N_BUCKETS - max_exact)).astype(np.int32)
    return np.where(d < max_exact, d, np.minimum(large, N_BUCKETS - 1))


def _toeplitz(w):
    t = w.shape[-1] // 2
    lead = w.shape[:-1]
    a = jnp.broadcast_to(w[..., None, :], lead + (t, 2 * t)).reshape(lead + (2 * t * t,))
    return a[..., :t * (2 * t - 1)].reshape(lead + (t, 2 * t - 1))[..., :t]


def _attn_tables(rel_bias, s):
    nq = s // TQ
    assert _rel_bucket_np(np.arange(TQ // 2, 4 * TQ)).min() == N_BUCKETS - 1
    relb = (rel_bias.astype(F32) - rel_bias[N_BUCKETS - 1].astype(F32)).T
    f = jnp.take(relb, jnp.asarray(_rel_bucket_np(np.arange(2 * TQ))), axis=1)
    neg = jnp.full((N_HEADS, TQ), NEG, F32)
    w_diag = jnp.concatenate([f[:, 0:1], neg, jnp.flip(f[:, 1:TQ], axis=1)], axis=1)
    w_prev = jnp.concatenate([jnp.flip(f[:, 1:TQ + 1], axis=1), jnp.zeros((N_HEADS, 1), F32),
                              jnp.flip(f[:, TQ + 1:2 * TQ], axis=1)], axis=1)
    ti = np.arange(TQ)[:, None]
    ki = np.arange(TQ)[None, :]
    prev2 = jnp.broadcast_to(jnp.asarray(np.where(ki > ti, 0.0, NEG), F32), (N_HEADS, TQ, TQ))
    hidden = jnp.full((N_HEADS, TQ, TQ), NEG, F32)
    b3 = jnp.stack([hidden, prev2, _toeplitz(w_prev), _toeplitz(w_diag)], axis=1)
    b3 = (b3.reshape(N_GROUPS, HPG, N_KINDS, TQ, TQ).transpose(0, 2, 1, 3, 4)
          .reshape(N_GROUPS, N_KINDS, HPG * TQ, TQ))
    n_cmp = s // CMP_STRIDE
    lead = CMP_WIN // 4
    dist = ti - CMP_STRIDE * (np.arange(CMP_WIN)[None, :] - lead) - (CMP_LEN - 1)
    pw = jnp.take(relb, jnp.asarray(_rel_bucket_np(dist)), axis=1)
    pw = jnp.where(jnp.asarray(dist >= 0), pw, NEG)
    full = jnp.concatenate([jnp.zeros((N_HEADS, TQ, n_cmp - lead), F32), pw,
                            jnp.full((N_HEADS, TQ, n_cmp), NEG, F32)], axis=-1)
    per_tile = TQ // CMP_STRIDE
    bc = jnp.stack([full[..., n_cmp - per_tile * t:2 * n_cmp - per_tile * t] for t in range(nq)])
    bc = bc.reshape(nq, N_GROUPS, HPG * TQ, n_cmp)
    return b3, bc


def _attn_consts(s):
    n_slc = s // SLC_LEN
    n_cmp = s // CMP_STRIDE
    key = np.arange(s)[:, None]
    lane = np.arange(LANES)[None, :]
    ceneg = np.where((key // SLC_LEN == lane) & (lane < n_slc), NEG, 0.0)
    j = np.arange(n_slc)[:, None]
    i = np.arange(n_cmp)[None, :]
    ov = ((i * CMP_STRIDE < j * SLC_LEN + SLC_LEN) & (i * CMP_STRIDE + CMP_LEN > j * SLC_LEN)
          & (i < n_cmp - 1))
    place = np.eye(n_slc, LANES)
    return (jnp.asarray(ceneg, BF16), jnp.asarray(np.concatenate([ov] * 3, axis=1), BF16),
            jnp.asarray(place, BF16))


def _attn_kernel(q_ref, gate_ref, kv_ref, kvc_ref, b3_ref, bc_ref, ceneg_ref, ov_ref, place_ref,
                 prev_ref, o_ref, q4_ref, sc_ref, nsc_ref, macc_ref, mb_ref, oacc_ref, comb_ref,
                 *, n_pairs):
    del prev_ref
    odd = pl.program_id(2)
    qb = 2 * n_pairs + odd
    n_slc = ov_ref.shape[0]
    rows = HPG * TQ
    lane = lax.broadcasted_iota(jnp.int32, (TQ, LANES), 1)
    low = lane < HEAD_DIM

    for pair in range(HPG // 2):
        qp = q_ref[0, :, pair * LANES:(pair + 1) * LANES].astype(F32)
        q4_ref[(2 * pair) * TQ:(2 * pair + 1) * TQ, 0:LANES] = jnp.where(low, qp, 0.0).astype(BF16)
        q4_ref[(2 * pair + 1) * TQ:(2 * pair + 2) * TQ, 0:LANES] = (
            jnp.where(low, pltpu.roll(qp, HEAD_DIM, 1), 0.0).astype(BF16))
    q4 = q4_ref[:, 0:LANES]

    def key_rows(kt, col0):
        k0 = pl.multiple_of(kt * TQ, TQ)
        return kv_ref[0, pl.ds(k0, TQ), col0:col0 + LANES]

    def lane_tiles(x):
        return [x[:, i:i + LANES] for i in range(0, x.shape[1], LANES)]

    def tile_max(x):
        return functools.reduce(jnp.maximum, lane_tiles(x))

    def exp2_rel(sv, mb):
        return jnp.concatenate([jnp.exp2(t - mb) for t in lane_tiles(sv)], axis=1).astype(BF16)

    if n_pairs >= 1:
        kt1, kt2, kind1, kind2 = qb - 1, qb - 2, TILE_PREV, TILE_PREV2
    else:
        kt1 = jnp.maximum(qb - 1, 0)
        kt2 = jnp.maximum(qb - 2, 0)
        kind1 = jnp.where(qb >= 1, TILE_PREV, TILE_HIDDEN)
        kind2 = TILE_HIDDEN

    def split3(x):
        hi = x.astype(BF16)
        r1 = x - hi.astype(F32)
        mid = r1.astype(BF16)
        lo = (r1 - mid.astype(F32)).astype(BF16)
        return jnp.concatenate([hi, mid, lo], axis=1)

    def ones_v(tile):
        return jnp.concatenate([tile, jnp.ones_like(tile)], axis=1)

    def weighted(o, hh, br):
        r = slice(hh * TQ, (hh + 1) * TQ)
        l = o[r, LANES:2 * LANES]
        gate = gate_ref[0, 0, :, 3 * hh + br:3 * hh + br + 1]
        return o[r, 0:LANES] * (gate / jnp.where(l > 0.0, l, 1.0))

    tiles_w = [key_rows(kt2, LANES), key_rows(kt1, LANES), key_rows(qb, LANES)]
    kinds_w = [kind2, kind1, TILE_DIAG]

    def win_scores(i):
        return (_split_rows(_dot_nt, q4, tiles_w[i]) + b3_ref[0, kinds_w[i]]) * LOG2E

    kvc = kvc_ref[0, 0]
    bias_c = bc_ref[0, 0]
    s_c = _split_rows(_dot_nt, q4, kvc) + bias_c
    s_w0 = win_scores(0)
    m_c = jnp.max(s_c, axis=1, keepdims=True)
    e_c = jnp.where(bias_c > 0.5 * NEG, jnp.exp(s_c - m_c), 0.0)
    o_c = _split_rows(_dot, e_c.astype(BF16), ones_v(kvc))
    s_w1 = win_scores(1)
    l_c = o_c[:, LANES:2 * LANES]
    p_c = e_c / jnp.where(l_c > 0.0, l_c, 1.0)
    psum = p_c[0:TQ]
    for hh in range(1, HPG):
        psum = psum + p_c[hh * TQ:(hh + 1) * TQ]

    imp_t = _dot_nt(ov_ref[...], split3(psum))
    s_w2 = win_scores(2)
    s_w = jnp.concatenate([s_w0, s_w1, s_w2], axis=1)
    m_w = jnp.max(tile_max(s_w), axis=1, keepdims=True)
    p_w = exp2_rel(s_w, jnp.broadcast_to(m_w, (rows, LANES)))
    o_w = _split_rows(_dot, p_w, ones_v(jnp.concatenate(tiles_w, axis=0)))

    jrow = lax.broadcasted_iota(jnp.int32, (n_slc, TQ), 0)
    tpos = qb * TQ + lax.broadcasted_iota(jnp.int32, (n_slc, TQ), 1)
    cur = lax.shift_right_logical(tpos, int(math.log2(SLC_LEN)))
    forced = (jrow == 0) | (jrow == cur) | (jrow == cur - 1)
    score = jnp.where(forced, 3e38, jnp.where(jrow <= cur, imp_t, -1.0))
    rank = jnp.zeros((n_slc, TQ), F32)
    for k in range(n_slc):
        sk = score[k:k + 1, :]
        beats = (sk > score) | ((sk == score) & (jrow > k))
        rank = rank + beats.astype(F32)
    notsel_t = (rank >= float(N_SEL)).astype(BF16)
    q_mask = lax.dot_general(notsel_t, place_ref[...], (((0,), (0,)), ((), ())),
                             preferred_element_type=F32).astype(BF16)
    for hh in range(HPG):
        q4_ref[hh * TQ:(hh + 1) * TQ, LANES:2 * LANES] = q_mask

    for hh in range(HPG):
        comb_ref[hh * TQ:(hh + 1) * TQ, :] = weighted(o_c, hh, 0) + weighted(o_w, hh, 2)

    pair = 2 * TQ
    rhs_n = jnp.concatenate([key_rows(kt1, 0), key_rows(qb, 0)], axis=0)
    k1 = pl.multiple_of(kt1 * TQ, TQ)
    kq = pl.multiple_of(qb * TQ, TQ)
    cen_n = jnp.concatenate([ceneg_ref[pl.ds(k1, TQ), :], ceneg_ref[pl.ds(kq, TQ), :]], axis=0)
    bias_n = jnp.concatenate([b3_ref[0, kind1], b3_ref[0, TILE_DIAG]], axis=1)
    s_n = (_dot_nt(q4_ref[...], jnp.concatenate([rhs_n, cen_n], axis=1)) + bias_n) * LOG2E
    nsc_ref[...] = s_n
    macc_ref[n_pairs] = tile_max(s_n)

    for p in range(n_pairs):
        ks = slice(p * pair, (p + 1) * pair)
        rhs = jnp.concatenate([kv_ref[0, ks, 0:LANES], ceneg_ref[ks, :]], axis=1)
        sv = _dot_nt(q4_ref[...], rhs) * LOG2E
        if p == n_pairs - 1:
            hide = jnp.where(odd == 1, 0.0, NEG)
            sv = jnp.concatenate([sv[:, :TQ], sv[:, TQ:] + hide], axis=1)
        sc_ref[p] = sv
        macc_ref[p] = tile_max(sv)

    m_el = macc_ref[0]
    for p in range(1, n_pairs + 1):
        m_el = jnp.maximum(m_el, macc_ref[p])
    m_s = jnp.max(m_el, axis=1, keepdims=True)
    mb_ref[...] = jnp.broadcast_to(m_s, (rows, LANES))
    oacc_ref[...] = jnp.zeros((rows, 2 * LANES), F32)

    for p in range(n_pairs):
        ks = slice(p * pair, (p + 1) * pair)
        rhs = ones_v(kv_ref[0, ks, 0:LANES])
        for r in (slice(0, rows // 2), slice(rows // 2, rows)):
            oacc_ref[r, :] += _dot(exp2_rel(sc_ref[p, r, :], mb_ref[r, :]), rhs)

    p_n = exp2_rel(nsc_ref[...], mb_ref[...])
    rhs_v = jnp.concatenate([key_rows(kt1, 0), key_rows(qb, 0)], axis=0)
    o_s = oacc_ref[...] + _split_rows(_dot, p_n, ones_v(rhs_v))

    for hp in range(HPG // 2):
        even, odd = [comb_ref[hh * TQ:(hh + 1) * TQ, :] + weighted(o_s, hh, 1)
                     for hh in (2 * hp, 2 * hp + 1)]
        o_ref[0, :, hp * LANES:(hp + 1) * LANES] = (
            jnp.where(low, pltpu.roll(even, HEAD_DIM, 1), odd).astype(BF16))


def _attention(q, gates, kvr, kvcmp, rel_bias):
    b, s, hd = q.shape
    nq = s // TQ
    n_slc = s // SLC_LEN
    n_cmp = s // CMP_STRIDE
    rows = HPG * TQ
    b3, bc = _attn_tables(rel_bias, s)
    ceneg, ov, place = _attn_consts(s)
    gw = 3 * HPG
    gates_g = (gates[:, :, :3 * N_HEADS].reshape(b, s, N_GROUPS, gw).transpose(0, 2, 1, 3))
    gl = HPG * HEAD_DIM
    out = jnp.zeros((b, s, N_HEADS * HEAD_DIM), BF16)
    for n_pairs in range(nq // 2):
        t0 = 2 * n_pairs
        args = [q, gates_g, kvr, kvcmp, b3, bc, ceneg, ov, place]
        in_specs = [
            pl.BlockSpec((1, TQ, gl), lambda g, i, t, t0=t0: (i, t0 + t, g)),
            pl.BlockSpec((1, 1, TQ, gw), lambda g, i, t, t0=t0: (i, g, t0 + t, 0)),
            pl.BlockSpec((1, s, 4 * HEAD_DIM), lambda g, i, t: (i, 0, g)),
            pl.BlockSpec((1, 1, n_cmp, LANES), lambda g, i, t: (i, g, 0, 0)),
            pl.BlockSpec((1, N_KINDS, rows, TQ), lambda g, i, t: (g, 0, 0, 0)),
            pl.BlockSpec((1, 1, rows, n_cmp), lambda g, i, t, t0=t0: (t0 + t, g, 0, 0)),
            pl.BlockSpec((s, LANES), lambda *_: (0, 0)),
            pl.BlockSpec((n_slc, 3 * n_cmp), lambda *_: (0, 0)),
            pl.BlockSpec((n_slc, LANES), lambda *_: (0, 0)),
        ]
        args.append(out)
        in_specs.append(pl.BlockSpec(memory_space=pl.ANY))
        aliases = {len(args) - 1: 0}
        out = pl.pallas_call(
            functools.partial(_attn_kernel, n_pairs=n_pairs),
            grid=(N_GROUPS, b, 2),
            in_specs=in_specs,
            out_specs=pl.BlockSpec((1, TQ, gl), lambda g, i, t, t0=t0: (i, t0 + t, g)),
            out_shape=jax.ShapeDtypeStruct((b, s, N_HEADS * HEAD_DIM), BF16),
            scratch_shapes=[
                pltpu.VMEM((rows, 2 * LANES), BF16),
                pltpu.VMEM((max(n_pairs, 1), rows, 2 * TQ), F32),
                pltpu.VMEM((rows, 2 * TQ), F32),
                pltpu.VMEM((n_pairs + 1, rows, LANES), F32),
                pltpu.VMEM((rows, LANES), F32),
                pltpu.VMEM((rows, 2 * LANES), F32),
                pltpu.VMEM((rows, LANES), F32),
            ],
            input_output_aliases=aliases,
            compiler_params=pltpu.CompilerParams(
                dimension_semantics=("arbitrary", "arbitrary", "arbitrary"),
                vmem_limit_bytes=VMEM_LIMIT),
            name=f"nsa_attn{n_pairs}",
        )(*args)
    return out


def kernel(x, mix_norm, a_w_in, a_conv, a_w_out, ffn_norm, ffn_up, ffn_conv, ffn_down,
           kv_norm, w_kv, cmp_pe, cmp_w1, cmp_b1, cmp_w2, b_w_qg, b_w_o, rel_bias, final_norm):
    b, s, d = x.shape
    assert d == D_MODEL and s % 512 == 0 and mix_norm.shape[0] == 2
    x = _mixer(x, mix_norm[0], a_w_in[0], a_conv[0], a_w_out[0])
    x = _ffn(x, ffn_norm[0], ffn_up[0], ffn_conv[0], ffn_down[0])
    q, gates, kvc, kvr = _proj(x, mix_norm[1], kv_norm, b_w_qg[0], w_kv)
    kvcmp = _compress(kvc, cmp_pe, cmp_w1, cmp_b1, cmp_w2)
    attn = _attention(q, gates, kvr, kvcmp, rel_bias)
    return _ffn(x, ffn_norm[1], ffn_up[1], ffn_conv[1], ffn_down[1],
                attn=attn, w_o=b_w_o[0].astype(BF16), final_g=final_norm)
```

```python
import functools
import math

import jax
import jax.numpy as jnp
import numpy as np
from jax import lax
from jax.experimental import pallas as pl
from jax.experimental.pallas import tpu as pltpu

D_MODEL = 1024
CONV_W = 3
D_FF = 2816
N_HEADS = 16
N_GROUPS = 4
HPG = N_HEADS // N_GROUPS
HEAD_DIM = 64
CMP_LEN = 32
CMP_STRIDE = 16
CMP_HIDDEN = 128
SLC_LEN = 64
N_SEL = 16
WINDOW = 512
N_BUCKETS = 32
MAX_DISTANCE = 128
EPS = 1e-6
NEG = -1e30

LANES = 128
V7X_VMEM_BYTES = 64 * 2**20
VMEM_LIMIT = V7X_VMEM_BYTES - 8 * 2**20

TQ = 256
CARRY_ROWS = 8
F32 = jnp.float32
BF16 = jnp.bfloat16


def _dot(a, b):
    return jnp.dot(a, b, preferred_element_type=F32)


def _dot_nt(a, b, precision=None):
    return lax.dot_general(a, b, (((1,), (1,)), ((), ())),
                           preferred_element_type=F32, precision=precision)


def _split_rows(dot, a, b):
    h = a.shape[0] // 2
    return jnp.concatenate([dot(a[:h], b), dot(a[h:], b)], axis=0)


def _rms_scale(x):
    return x * lax.rsqrt(jnp.mean(x * x, axis=-1, keepdims=True) + EPS)


def _conv3(buf_ref, cw, rows):
    c = CARRY_ROWS
    return (cw[0:1, :] * buf_ref[c - 2:c - 2 + rows, :]
            + cw[1:2, :] * buf_ref[c - 1:c - 1 + rows, :]
            + cw[2:3, :] * buf_ref[c:c + rows, :])


def _mixer_kernel(x_ref, g_ref, win_ref, cw_ref, wout_ref, o_ref, ubuf_ref, *, tm):
    d = D_MODEL

    @pl.when(pl.program_id(1) == 0)
    def _():
        ubuf_ref[0:CARRY_ROWS, :] = jnp.zeros((CARRY_ROWS, d), F32)

    x = x_ref[0]
    h = (_rms_scale(x) * g_ref[...]).astype(BF16)
    cg = _dot(h, win_ref[:, d:2 * d])
    v = _dot(h, win_ref[:, 2 * d:3 * d])
    ubuf_ref[CARRY_ROWS:CARRY_ROWS + tm, :] = cg * v
    conv = _conv3(ubuf_ref, cw_ref[...], tm)
    ubuf_ref[0:CARRY_ROWS, :] = ubuf_ref[tm:tm + CARRY_ROWS, :]
    bg = _dot(h, win_ref[:, 0:d])
    y = (bg * conv).astype(BF16)
    o_ref[0] = x + _dot(y, wout_ref[...])


def _mixer(x, g, w_in, conv_w, w_out, *, tm=512):
    b, s, d = x.shape
    const = lambda *_: (0, 0)
    one = pl.Buffered(1)
    return pl.pallas_call(
        functools.partial(_mixer_kernel, tm=tm),
        grid=(b, s // tm),
        in_specs=[
            pl.BlockSpec((1, tm, d), lambda i, j: (i, j, 0)),
            pl.BlockSpec((1, d), const),
            pl.BlockSpec((d, 3 * d), const, pipeline_mode=one),
            pl.BlockSpec((CONV_W, d), const),
            pl.BlockSpec((d, d), const, pipeline_mode=one),
        ],
        out_specs=pl.BlockSpec((1, tm, d), lambda i, j: (i, j, 0)),
        out_shape=jax.ShapeDtypeStruct((b, s, d), F32),
        scratch_shapes=[pltpu.VMEM((CARRY_ROWS + tm, d), F32)],
        compiler_params=pltpu.CompilerParams(
            dimension_semantics=("arbitrary", "arbitrary"), vmem_limit_bytes=VMEM_LIMIT),
        name="mixer",
    )(x, g.reshape(1, d), w_in.astype(BF16), conv_w, w_out.astype(BF16))


def _ffn_kernel(*refs, tm, tf, has_attn, final_norm):
    refs = list(refs)
    x_ref = refs.pop(0)
    a_ref = refs.pop(0) if has_attn else None
    wo_ref = refs.pop(0) if has_attn else None
    g_ref, wup_ref, cw_ref, wdn_ref = refs[:4]
    refs = refs[4:]
    gf_ref = refs.pop(0) if final_norm else None
    n_chunks = D_FF // tf
    o_ref, h_ref, carry_ref, acc_ref = refs[:4]
    u_refs = refs[4:]
    assert len(u_refs) == 2 * n_chunks

    @pl.when(pl.program_id(1) == 0)
    def _():
        carry_ref[...] = jnp.zeros(carry_ref.shape, F32)

    x = x_ref[0]
    if has_attn:
        x = x + _dot(a_ref[0], wo_ref[...])
    h_ref[...] = (_rms_scale(x) * g_ref[...]).astype(BF16)

    def up(buf_ref, c0):
        cols = slice(c0, c0 + tf)
        buf_ref[0:CARRY_ROWS, :] = carry_ref[:, cols]
        buf_ref[CARRY_ROWS:CARRY_ROWS + tm, :] = _dot(h_ref[...], wup_ref[:, cols])
        carry_ref[:, cols] = buf_ref[tm:tm + CARRY_ROWS, :]

    for c in range(n_chunks):
        up(u_refs[2 * c], c * tf)
        up(u_refs[2 * c + 1], D_FF + c * tf)

    for c in range(n_chunks):
        a = _conv3(u_refs[2 * c], cw_ref[:, c * tf:(c + 1) * tf], tm)
        gt = _conv3(u_refs[2 * c + 1], cw_ref[:, D_FF + c * tf:D_FF + (c + 1) * tf], tm)
        act = (a * jax.nn.sigmoid(a) * gt).astype(BF16)
        part = _dot(act, wdn_ref[c * tf:(c + 1) * tf, :])
        if c == 0:
            acc_ref[...] = x + part
        else:
            acc_ref[...] += part
    y = acc_ref[...]
    if final_norm:
        y = _rms_scale(y) * gf_ref[...]
    o_ref[0] = y


def _ffn(x, g, w_up, conv_w, w_down, *, attn=None, w_o=None, final_g=None, tm=512, tf=1408):
    b, s, d = x.shape
    has_attn = attn is not None
    final_norm = final_g is not None
    const = lambda *_: (0, 0)
    row = lambda i, j: (i, j, 0)
    one = pl.Buffered(1)
    args = [x]
    in_specs = [pl.BlockSpec((1, tm, d), row)]
    if has_attn:
        ka = attn.shape[-1]
        args += [attn, w_o]
        in_specs += [pl.BlockSpec((1, tm, ka), row),
                     pl.BlockSpec((ka, d), const, pipeline_mode=one)]
    args += [g.reshape(1, d), w_up.astype(BF16), conv_w, w_down.astype(BF16)]
    in_specs += [pl.BlockSpec((1, d), const),
                 pl.BlockSpec((d, 2 * D_FF), const, pipeline_mode=one),
                 pl.BlockSpec((CONV_W, 2 * D_FF), const),
                 pl.BlockSpec((D_FF, d), const, pipeline_mode=one)]
    if final_norm:
        args.append(final_g.reshape(1, d))
        in_specs.append(pl.BlockSpec((1, d), const))
    return pl.pallas_call(
        functools.partial(_ffn_kernel, tm=tm, tf=tf, has_attn=has_attn, final_norm=final_norm),
        grid=(b, s // tm),
        in_specs=in_specs,
        out_specs=pl.BlockSpec((1, tm, d), row),
        out_shape=jax.ShapeDtypeStruct((b, s, d), F32),
        scratch_shapes=[
            pltpu.VMEM((tm, d), BF16),
            pltpu.VMEM((CARRY_ROWS, 2 * D_FF), F32),
            pltpu.VMEM((tm, d), F32),
        ] + [pltpu.VMEM((CARRY_ROWS + tm, tf), F32)] * (2 * (D_FF // tf)),
        compiler_params=pltpu.CompilerParams(
            dimension_semantics=("arbitrary", "arbitrary"), vmem_limit_bytes=VMEM_LIMIT),
        name="ffn_attn" if has_attn else "ffn",
    )(*args)


def _proj_kernel(x_ref, gm_ref, gk_ref, wq_ref, wg_ref, wkc_ref, wkr_ref,
                 q_ref, gate_ref, kvc_ref, kvr_ref):
    xn = _rms_scale(x_ref[0])
    hq = (xn * gm_ref[...]).astype(BF16)
    hs = (xn * gk_ref[...]).astype(BF16)
    q_ref[0] = (_dot(hq, wq_ref[...]) * (HEAD_DIM ** -0.5)).astype(BF16)
    gates = jax.nn.sigmoid(_dot(hq, wg_ref[...]))
    gate_ref[0] = gates
    kvc = _dot(hs, wkc_ref[...])
    for sl in range(2 * N_GROUPS * HEAD_DIM // LANES):
        kvc_ref[0, sl] = kvc[:, sl * LANES:(sl + 1) * LANES]
    kvr_ref[0] = _dot(hs, wkr_ref[...]).astype(BF16)


def _proj(x, g_mix, g_kv, w_qg, w_kv, *, tm=512):
    b, s, d = x.shape
    hd = N_HEADS * HEAD_DIM
    gd = N_GROUPS * HEAD_DIM
    w_q = w_qg[:, :hd].astype(BF16)
    w_g = jnp.pad(w_qg[:, hd:], ((0, 0), (0, LANES - 3 * N_HEADS))).astype(BF16)
    w_kc = w_kv[:, :2 * gd].astype(BF16)
    w_kr = (w_kv[:, 2 * gd:].reshape(d, 4, N_GROUPS, HEAD_DIM)
            .transpose(0, 2, 1, 3).reshape(d, 4 * gd).astype(BF16))
    const = lambda *_: (0, 0)
    row = lambda i, j: (i, j, 0)
    one = pl.Buffered(1)
    return pl.pallas_call(
        _proj_kernel,
        grid=(b, s // tm),
        in_specs=[
            pl.BlockSpec((1, tm, d), row),
            pl.BlockSpec((1, d), const),
            pl.BlockSpec((1, d), const),
            pl.BlockSpec((d, hd), const, pipeline_mode=one),
            pl.BlockSpec((d, LANES), const, pipeline_mode=one),
            pl.BlockSpec((d, 2 * gd), const, pipeline_mode=one),
            pl.BlockSpec((d, 4 * gd), const, pipeline_mode=one),
        ],
        out_specs=[
            pl.BlockSpec((1, tm, hd), row),
            pl.BlockSpec((1, tm, LANES), row),
            pl.BlockSpec((1, 2 * gd // LANES, tm, LANES), lambda i, j: (i, 0, j, 0)),
            pl.BlockSpec((1, tm, 4 * gd), row),
        ],
        out_shape=[
            jax.ShapeDtypeStruct((b, s, hd), BF16),
            jax.ShapeDtypeStruct((b, s, LANES), F32),
            jax.ShapeDtypeStruct((b, 2 * gd // LANES, s, LANES), F32),
            jax.ShapeDtypeStruct((b, s, 4 * gd), BF16),
        ],
        compiler_params=pltpu.CompilerParams(
            dimension_semantics=("arbitrary", "arbitrary"), vmem_limit_bytes=VMEM_LIMIT),
        name="proj",
    )(x, g_mix.reshape(1, d), g_kv.reshape(1, d), w_q, w_g, w_kc, w_kr)


def _gelu_tanh(x):
    return 0.5 * x * (1.0 + jnp.tanh(math.sqrt(2.0 / math.pi) * (x + 0.044715 * (x * x * x))))


def _compress_kernel(k_ref, v_ref, pe_ref, w1_ref, b1_ref, w2_ref, o_ref):
    n_chunks = int(o_ref.shape[2])
    outs = [None, None]
    for j, src in enumerate((k_ref, v_ref)):
        first = None
        second = None
        for l in range(CMP_STRIDE):
            x = src.at[0, 0][pl.ds(l, n_chunks, stride=CMP_STRIDE), :]
            pa = _dot((x + pe_ref[j, l:l + 1, :]).astype(BF16), w1_ref[j, l])
            pb = _dot((x + pe_ref[j, CMP_STRIDE + l:CMP_STRIDE + l + 1, :]).astype(BF16),
                      w1_ref[j, CMP_STRIDE + l])
            first = pa if first is None else first + pa
            second = pb if second is None else second + pb
        pre = first + pltpu.roll(second, n_chunks - 1, 0) + b1_ref[j]
        hid = _gelu_tanh(pre).astype(BF16)
        for e in range(2):
            part = _dot(hid[:, e * CMP_HIDDEN:(e + 1) * CMP_HIDDEN], w2_ref[j])
            outs[e] = part if outs[e] is None else outs[e] + part
    for e in range(2):
        o_ref[0, e] = outs[e].astype(BF16)


def _compress(kvc, cmp_pe, cmp_w1, cmp_b1, cmp_w2):
    b, n_slabs, s, _ = kvc.shape
    n_chunks = s // CMP_STRIDE
    pairs = N_GROUPS // 2
    assert n_slabs == 2 * pairs and 2 * HEAD_DIM == LANES
    pe = jnp.concatenate([cmp_pe, cmp_pe], axis=-1)
    w1 = cmp_w1.reshape(2, CMP_LEN, HEAD_DIM, CMP_HIDDEN)
    zero = jnp.zeros_like(w1)
    w1 = jnp.concatenate([jnp.concatenate([w1, zero], axis=-1),
                          jnp.concatenate([zero, w1], axis=-1)], axis=2).astype(BF16)
    b1 = jnp.concatenate([cmp_b1, cmp_b1], axis=-1).reshape(2, 1, 2 * CMP_HIDDEN)
    w2 = jnp.stack([jnp.pad(cmp_w2[0], ((0, 0), (0, HEAD_DIM))),
                    jnp.pad(cmp_w2[1], ((0, 0), (HEAD_DIM, 0)))]).astype(BF16)
    const = lambda *_: (0, 0, 0)
    return pl.pallas_call(
        _compress_kernel,
        grid=(b, pairs),
        in_specs=[
            pl.BlockSpec((1, 1, s, LANES), lambda i, p: (i, p, 0, 0)),
            pl.BlockSpec((1, 1, s, LANES), lambda i, p: (i, pairs + p, 0, 0)),
            pl.BlockSpec((2, CMP_LEN, LANES), const),
            pl.BlockSpec((2, CMP_LEN, LANES, 2 * CMP_HIDDEN), lambda *_: (0, 0, 0, 0)),
            pl.BlockSpec((2, 1, 2 * CMP_HIDDEN), const),
            pl.BlockSpec((2, CMP_HIDDEN, LANES), const),
        ],
        out_specs=pl.BlockSpec((1, 2, n_chunks, LANES), lambda i, p: (i, p, 0, 0)),
        out_shape=jax.ShapeDtypeStruct((b, N_GROUPS, n_chunks, LANES), BF16),
        compiler_params=pltpu.CompilerParams(
            dimension_semantics=("arbitrary", "arbitrary"), vmem_limit_bytes=VMEM_LIMIT),
        name="compress",
    )(kvc, kvc, pe, w1, b1, w2)


TILE_PREV2, TILE_PREV, TILE_DIAG = 0, 1, 2
N_KINDS = 3
CMP_WIN = 32
LOG2E = 1.0 / math.log(2.0)


def _rel_bucket_np(dist):
    max_exact = N_BUCKETS // 2
    d = np.maximum(dist, 0)
    df = np.maximum(d, 1).astype(np.float32)
    large = max_exact + (np.log(df / max_exact) / np.float32(math.log(MAX_DISTANCE / max_exact))
                         * (N_BUCKETS - max_exact)).astype(np.int32)
    return np.where(d < max_exact, d, np.minimum(large, N_BUCKETS - 1))


def _toeplitz(w):
    t = w.shape[-1] // 2
    lead = w.shape[:-1]
    a = jnp.broadcast_to(w[..., None, :], lead + (t, 2 * t)).reshape(lead + (2 * t * t,))
    return a[..., :t * (2 * t - 1)].reshape(lead + (t, 2 * t - 1))[..., :t]


def _attn_tables(rel_bias, s):
    nq = s // TQ
    assert _rel_bucket_np(np.arange(TQ // 2, 4 * TQ)).min() == N_BUCKETS - 1
    relb = (rel_bias.astype(F32) - rel_bias[N_BUCKETS - 1].astype(F32)).T
    f = jnp.take(relb, jnp.asarray(_rel_bucket_np(np.arange(2 * TQ))), axis=1)
    neg = jnp.full((N_HEADS, TQ), NEG, F32)
    w_diag = jnp.concatenate([f[:, 0:1], neg, jnp.flip(f[:, 1:TQ], axis=1)], axis=1)
    w_prev = jnp.concatenate([jnp.flip(f[:, 1:TQ + 1], axis=1), jnp.zeros((N_HEADS, 1), F32),
                              jnp.flip(f[:, TQ + 1:2 * TQ], axis=1)], axis=1)
    ti = np.arange(TQ)[:, None]
    ki = np.arange(TQ)[None, :]
    prev2 = jnp.broadcast_to(jnp.asarray(np.where(ki > ti, 0.0, NEG), F32), (N_HEADS, TQ, TQ))
    b3 = jnp.stack([prev2, _toeplitz(w_prev), _toeplitz(w_diag)], axis=1)
    b3 = (b3.reshape(N_GROUPS, HPG, N_KINDS, TQ, TQ).transpose(0, 2, 1, 3, 4)
          .reshape(N_GROUPS, N_KINDS, HPG * TQ, TQ))
    n_cmp = s // CMP_STRIDE
    lead = CMP_WIN // 4
    dist = ti - CMP_STRIDE * (np.arange(CMP_WIN)[None, :] - lead) - (CMP_LEN - 1)
    pw = jnp.take(relb, jnp.asarray(_rel_bucket_np(dist)), axis=1)
    pw = jnp.where(jnp.asarray(dist >= 0), pw, NEG)
    full = jnp.concatenate([jnp.zeros((N_HEADS, TQ, n_cmp - lead), F32), pw,
                            jnp.full((N_HEADS, TQ, n_cmp), NEG, F32)], axis=-1)
    per_tile = TQ // CMP_STRIDE
    bc = jnp.stack([full[..., n_cmp - per_tile * t:2 * n_cmp - per_tile * t] for t in range(nq)])
    bc = bc.reshape(nq, N_GROUPS, HPG * TQ, n_cmp)
    return b3, bc


def _attn_consts(s):
    n_slc = s // SLC_LEN
    n_cmp = s // CMP_STRIDE
    key = np.arange(s)[:, None]
    lane = np.arange(LANES)[None, :]
    ceneg = np.where((key // SLC_LEN == lane) & (lane < n_slc), NEG, 0.0)
    j = np.arange(n_slc)[:, None]
    i = np.arange(n_cmp)[None, :]
    ov = ((i * CMP_STRIDE < j * SLC_LEN + SLC_LEN) & (i * CMP_STRIDE + CMP_LEN > j * SLC_LEN)
          & (i < n_cmp - 1))
    place = np.eye(n_slc, LANES)
    return (jnp.asarray(ceneg, BF16), jnp.asarray(np.concatenate([ov] * 3, axis=1), BF16),
            jnp.asarray(place, BF16))


def _attn_kernel(q_ref, gate_ref, kv_ref, kvc_ref, b3_ref, bc_ref, ceneg_ref, ov_ref, place_ref,
                 prev_ref, o_ref, q4_ref, sc_ref, nsc_ref, macc_ref, mb_ref, oacc_ref, comb_ref,
                 *, qb):
    del prev_ref
    n_slc = ov_ref.shape[0]
    rows = HPG * TQ
    lane = lax.broadcasted_iota(jnp.int32, (TQ, LANES), 1)
    low = lane < HEAD_DIM

    for pair in range(HPG // 2):
        qp = q_ref[0, :, pair * LANES:(pair + 1) * LANES].astype(F32)
        q4_ref[(2 * pair) * TQ:(2 * pair + 1) * TQ, 0:LANES] = jnp.where(low, qp, 0.0).astype(BF16)
        q4_ref[(2 * pair + 1) * TQ:(2 * pair + 2) * TQ, 0:LANES] = (
            jnp.where(low, pltpu.roll(qp, HEAD_DIM, 1), 0.0).astype(BF16))
    q4 = q4_ref[:, 0:LANES]

    def key_rows(kt, col0, n=1):
        return kv_ref[0, kt * TQ:(kt + n) * TQ, col0:col0 + LANES]

    def lane_tiles(x):
        return [x[:, i:i + LANES] for i in range(0, x.shape[1], LANES)]

    def tile_max(x):
        return functools.reduce(jnp.maximum, lane_tiles(x))

    def exp2_rel(sv, mb):
        return jnp.concatenate([jnp.exp2(t - mb) for t in lane_tiles(sv)], axis=1).astype(BF16)

    win_tiles = [(kt, kind) for kt, kind in
                 ((qb - 2, TILE_PREV2), (qb - 1, TILE_PREV), (qb, TILE_DIAG)) if kt >= 0]
    near_tiles = win_tiles[-2:]
    n_far = max(qb - 1, 0)
    far_groups = [(t, min(2, n_far - t)) for t in range(0, n_far, 2)]

    def split3(x):
        hi = x.astype(BF16)
        r1 = x - hi.astype(F32)
        mid = r1.astype(BF16)
        lo = (r1 - mid.astype(F32)).astype(BF16)
        return jnp.concatenate([hi, mid, lo], axis=1)

    def ones_v(tile):
        return jnp.concatenate([tile, jnp.ones_like(tile)], axis=1)

    def weighted(o, hh, br):
        r = slice(hh * TQ, (hh + 1) * TQ)
        l = o[r, LANES:2 * LANES]
        gate = gate_ref[0, 0, :, 3 * hh + br:3 * hh + br + 1]
        return o[r, 0:LANES] * (gate / jnp.where(l > 0.0, l, 1.0))

    tiles_w = [key_rows(kt, LANES) for kt, _ in win_tiles]
    s_w = []

    def win_scores():
        i = len(s_w)
        if i < len(win_tiles):
            s_w.append((_split_rows(_dot_nt, q4, tiles_w[i]) + b3_ref[0, win_tiles[i][1]]) * LOG2E)

    kvc = kvc_ref[0, 0]
    bias_c = bc_ref[0, 0]
    s_c = _split_rows(_dot_nt, q4, kvc) + bias_c
    win_scores()
    m_c = jnp.max(s_c, axis=1, keepdims=True)
    e_c = jnp.where(bias_c > 0.5 * NEG, jnp.exp(s_c - m_c), 0.0)
    o_c = _split_rows(_dot, e_c.astype(BF16), ones_v(kvc))
    win_scores()
    l_c = o_c[:, LANES:2 * LANES]
    p_c = e_c / jnp.where(l_c > 0.0, l_c, 1.0)
    psum = p_c[0:TQ]
    for hh in range(1, HPG):
        psum = psum + p_c[hh * TQ:(hh + 1) * TQ]

    imp_t = _dot_nt(ov_ref[...], split3(psum))
    win_scores()
    s_w = jnp.concatenate(s_w, axis=1)
    m_w = jnp.max(tile_max(s_w), axis=1, keepdims=True)
    p_w = exp2_rel(s_w, jnp.broadcast_to(m_w, (rows, LANES)))
    o_w = _split_rows(_dot, p_w, ones_v(jnp.concatenate(tiles_w, axis=0)))

    jrow = lax.broadcasted_iota(jnp.int32, (n_slc, TQ), 0)
    tpos = qb * TQ + lax.broadcasted_iota(jnp.int32, (n_slc, TQ), 1)
    cur = lax.shift_right_logical(tpos, int(math.log2(SLC_LEN)))
    forced = (jrow == 0) | (jrow == cur) | (jrow == cur - 1)
    score = jnp.where(forced, 3e38, jnp.where(jrow <= cur, imp_t, -1.0))
    rank = jnp.zeros((n_slc, TQ), F32)
    for k in range(n_slc):
        sk = score[k:k + 1, :]
        beats = (sk > score) | ((sk == score) & (jrow > k))
        rank = rank + beats.astype(F32)
    notsel_t = (rank >= float(N_SEL)).astype(BF16)
    q_mask = lax.dot_general(notsel_t, place_ref[...], (((0,), (0,)), ((), ())),
                             preferred_element_type=F32).astype(BF16)
    for hh in range(HPG):
        q4_ref[hh * TQ:(hh + 1) * TQ, LANES:2 * LANES] = q_mask

    for hh in range(HPG):
        comb_ref[hh * TQ:(hh + 1) * TQ, :] = weighted(o_c, hh, 0) + weighted(o_w, hh, 2)

    kt_n, n_near = near_tiles[0][0], len(near_tiles)
    ks_n = slice(kt_n * TQ, (kt_n + n_near) * TQ)
    bias_n = jnp.concatenate([b3_ref[0, kind] for _, kind in near_tiles], axis=1)
    rhs_n = jnp.concatenate([key_rows(kt_n, 0, n_near), ceneg_ref[ks_n, :]], axis=1)
    dot_n = _dot_nt if n_near == 2 else functools.partial(_split_rows, _dot_nt)
    s_n = (dot_n(q4_ref[...], rhs_n) + bias_n) * LOG2E
    nsc_ref[...] = s_n
    macc_ref[len(far_groups)] = tile_max(s_n)

    for i, (kt, n) in enumerate(far_groups):
        ks = slice(kt * TQ, (kt + n) * TQ)
        rhs = jnp.concatenate([key_rows(kt, 0, n), ceneg_ref[ks, :]], axis=1)
        dot_f = _dot_nt if n == 2 else functools.partial(_split_rows, _dot_nt)
        sv = dot_f(q4_ref[...], rhs) * LOG2E
        sc_ref[i, :, 0:n * TQ] = sv
        macc_ref[i] = tile_max(sv)

    m_el = macc_ref[0]
    for i in range(1, len(far_groups) + 1):
        m_el = jnp.maximum(m_el, macc_ref[i])
    m_s = jnp.max(m_el, axis=1, keepdims=True)
    mb_ref[...] = jnp.broadcast_to(m_s, (rows, LANES))
    oacc_ref[...] = jnp.zeros((rows, 2 * LANES), F32)

    for i, (kt, n) in enumerate(far_groups):
        rhs = ones_v(key_rows(kt, 0, n))
        for r in (slice(0, rows // 2), slice(rows // 2, rows)):
            oacc_ref[r, :] += _dot(exp2_rel(sc_ref[i, r, 0:n * TQ], mb_ref[r, :]), rhs)

    p_n = exp2_rel(nsc_ref[...], mb_ref[...])
    o_s = oacc_ref[...] + _split_rows(_dot, p_n, ones_v(key_rows(kt_n, 0, n_near)))

    for hp in range(HPG // 2):
        even, odd = [comb_ref[hh * TQ:(hh + 1) * TQ, :] + weighted(o_s, hh, 1)
                     for hh in (2 * hp, 2 * hp + 1)]
        o_ref[0, :, hp * LANES:(hp + 1) * LANES] = (
            jnp.where(low, pltpu.roll(even, HEAD_DIM, 1), odd).astype(BF16))


def _attention(q, gates, kvr, kvcmp, rel_bias):
    b, s, hd = q.shape
    nq = s // TQ
    n_slc = s // SLC_LEN
    n_cmp = s // CMP_STRIDE
    rows = HPG * TQ
    b3, bc = _attn_tables(rel_bias, s)
    ceneg, ov, place = _attn_consts(s)
    gw = 3 * HPG
    gates_g = (gates[:, :, :3 * N_HEADS].reshape(b, s, N_GROUPS, gw).transpose(0, 2, 1, 3))
    gl = HPG * HEAD_DIM
    out = jnp.zeros((b, s, N_HEADS * HEAD_DIM), BF16)
    for qb in range(nq):
        n_groups = (max(qb - 1, 0) + 1) // 2
        n_near = min(qb + 1, 2)
        in_specs = [
            pl.BlockSpec((1, TQ, gl), lambda g, i, qb=qb: (i, qb, g)),
            pl.BlockSpec((1, 1, TQ, gw), lambda g, i, qb=qb: (i, g, qb, 0)),
            pl.BlockSpec((1, s, 4 * HEAD_DIM), lambda g, i: (i, 0, g)),
            pl.BlockSpec((1, 1, n_cmp, LANES), lambda g, i: (i, g, 0, 0)),
            pl.BlockSpec((1, N_KINDS, rows, TQ), lambda g, i: (g, 0, 0, 0)),
            pl.BlockSpec((1, 1, rows, n_cmp), lambda g, i, qb=qb: (qb, g, 0, 0)),
            pl.BlockSpec((s, LANES), lambda *_: (0, 0)),
            pl.BlockSpec((n_slc, 3 * n_cmp), lambda *_: (0, 0)),
            pl.BlockSpec((n_slc, LANES), lambda *_: (0, 0)),
            pl.BlockSpec(memory_space=pl.ANY),
        ]
        args = [q, gates_g, kvr, kvcmp, b3, bc, ceneg, ov, place, out]
        out = pl.pallas_call(
            functools.partial(_attn_kernel, qb=qb),
            grid=(N_GROUPS, b),
            in_specs=in_specs,
            out_specs=pl.BlockSpec((1, TQ, gl), lambda g, i, qb=qb: (i, qb, g)),
            out_shape=jax.ShapeDtypeStruct((b, s, N_HEADS * HEAD_DIM), BF16),
            scratch_shapes=[
                pltpu.VMEM((rows, 2 * LANES), BF16),
                pltpu.VMEM((max(n_groups, 1), rows, 2 * TQ), F32),
                pltpu.VMEM((rows, n_near * TQ), F32),
                pltpu.VMEM((n_groups + 1, rows, LANES), F32),
                pltpu.VMEM((rows, LANES), F32),
                pltpu.VMEM((rows, 2 * LANES), F32),
                pltpu.VMEM((rows, LANES), F32),
            ],
            input_output_aliases={len(args) - 1: 0},
            compiler_params=pltpu.CompilerParams(
                dimension_semantics=("arbitrary", "arbitrary"), vmem_limit_bytes=VMEM_LIMIT),
            name=f"nsa_attn{qb}",
        )(*args)
    return out


def kernel(x, mix_norm, a_w_in, a_conv, a_w_out, ffn_norm, ffn_up, ffn_conv, ffn_down,
           kv_norm, w_kv, cmp_pe, cmp_w1, cmp_b1, cmp_w2, b_w_qg, b_w_o, rel_bias, final_norm):
    b, s, d = x.shape
    assert d == D_MODEL and s % 512 == 0 and mix_norm.shape[0] == 2
    x = _mixer(x, mix_norm[0], a_w_in[0], a_conv[0], a_w_out[0])
    x = _ffn(x, ffn_norm[0], ffn_up[0], ffn_conv[0], ffn_down[0])
    q, gates, kvc, kvr = _proj(x, mix_norm[1], kv_norm, b_w_qg[0], w_kv)
    kvcmp = _compress(kvc, cmp_pe, cmp_w1, cmp_b1, cmp_w2)
    attn = _attention(q, gates, kvr, kvcmp, rel_bias)
    return _ffn(x, ffn_norm[1], ffn_up[1], ffn_conv[1], ffn_down[1],
                attn=attn, w_o=b_w_o[0].astype(BF16), final_g=final_norm)
```

```python
import functools
import math

import jax
import jax.numpy as jnp
import numpy as np
from jax import lax
from jax.experimental import pallas as pl
from jax.experimental.pallas import tpu as pltpu

D_MODEL = 1024
CONV_W = 3
D_FF = 2816
N_HEADS = 16
N_GROUPS = 4
HPG = N_HEADS // N_GROUPS
HEAD_DIM = 64
CMP_LEN = 32
CMP_STRIDE = 16
CMP_HIDDEN = 128
SLC_LEN = 64
N_SEL = 16
WINDOW = 512
N_BUCKETS = 32
MAX_DISTANCE = 128
EPS = 1e-6
NEG = -1e30

LANES = 128
MXU_COLS = 256
V7X_VMEM_BYTES = 64 * 2**20
VMEM_LIMIT = V7X_VMEM_BYTES - 8 * 2**20

TQ = 256
CARRY_ROWS = 8
F32 = jnp.float32
BF16 = jnp.bfloat16


def _dot(a, b):
    return jnp.dot(a, b, preferred_element_type=F32)


def _dot_nt(a, b, precision=None):
    return lax.dot_general(a, b, (((1,), (1,)), ((), ())),
                           preferred_element_type=F32, precision=precision)


def _split_rows(dot, a, b):
    h = a.shape[0] // 2
    return jnp.concatenate([dot(a[:h], b), dot(a[h:], b)], axis=0)


def _rms_scale(x):
    return x * lax.rsqrt(jnp.mean(x * x, axis=-1, keepdims=True) + EPS)


def _conv3(buf_ref, cw, rows):
    c = CARRY_ROWS
    return (cw[0:1, :] * buf_ref[c - 2:c - 2 + rows, :]
            + cw[1:2, :] * buf_ref[c - 1:c - 1 + rows, :]
            + cw[2:3, :] * buf_ref[c:c + rows, :])


def _mixer_kernel(x_ref, g_ref, win_ref, cw_ref, wout_ref, o_ref, ubuf_ref, *, tm):
    d = D_MODEL

    @pl.when(pl.program_id(1) == 0)
    def _():
        ubuf_ref[0:CARRY_ROWS, :] = jnp.zeros((CARRY_ROWS, d), F32)

    x = x_ref[0]
    h = (_rms_scale(x) * g_ref[...]).astype(BF16)
    cg = _dot(h, win_ref[:, d:2 * d])
    v = _dot(h, win_ref[:, 2 * d:3 * d])
    ubuf_ref[CARRY_ROWS:CARRY_ROWS + tm, :] = cg * v
    conv = _conv3(ubuf_ref, cw_ref[...], tm)
    ubuf_ref[0:CARRY_ROWS, :] = ubuf_ref[tm:tm + CARRY_ROWS, :]
    bg = _dot(h, win_ref[:, 0:d])
    y = (bg * conv).astype(BF16)
    o_ref[0] = x + _dot(y, wout_ref[...])


def _mixer(x, g, w_in, conv_w, w_out, *, tm=512):
    b, s, d = x.shape
    const = lambda *_: (0, 0)
    one = pl.Buffered(1)
    return pl.pallas_call(
        functools.partial(_mixer_kernel, tm=tm),
        grid=(b, s // tm),
        in_specs=[
            pl.BlockSpec((1, tm, d), lambda i, j: (i, j, 0)),
            pl.BlockSpec((1, d), const),
            pl.BlockSpec((d, 3 * d), const, pipeline_mode=one),
            pl.BlockSpec((CONV_W, d), const),
            pl.BlockSpec((d, d), const, pipeline_mode=one),
        ],
        out_specs=pl.BlockSpec((1, tm, d), lambda i, j: (i, j, 0)),
        out_shape=jax.ShapeDtypeStruct((b, s, d), F32),
        scratch_shapes=[pltpu.VMEM((CARRY_ROWS + tm, d), F32)],
        compiler_params=pltpu.CompilerParams(
            dimension_semantics=("arbitrary", "arbitrary"), vmem_limit_bytes=VMEM_LIMIT),
        name="mixer",
    )(x, g.reshape(1, d), w_in.astype(BF16), conv_w, w_out.astype(BF16))


def _ffn_chunks():
    tiles = D_FF // MXU_COLS
    assert tiles * MXU_COLS == D_FF
    first = (tiles + 1) // 2 * MXU_COLS
    return [(0, first), (first, D_FF)]


def _ffn_kernel(*refs, tm, has_attn, final_norm):
    refs = list(refs)
    x_ref = refs.pop(0)
    a_ref = refs.pop(0) if has_attn else None
    wo_ref = refs.pop(0) if has_attn else None
    g_ref, wup_ref, cw_ref, wdn_ref = refs[:4]
    refs = refs[4:]
    gf_ref = refs.pop(0) if final_norm else None
    chunks = _ffn_chunks()
    o_ref, h_ref, carry_ref, acc_ref = refs[:4]
    u_refs = refs[4:]
    assert len(u_refs) == 2 * len(chunks)

    @pl.when(pl.program_id(1) == 0)
    def _():
        carry_ref[...] = jnp.zeros(carry_ref.shape, F32)

    x = x_ref[0]
    if has_attn:
        x = x + _dot(a_ref[0], wo_ref[...])
    h_ref[...] = (_rms_scale(x) * g_ref[...]).astype(BF16)

    def up(buf_ref, cols):
        w = cols.stop - cols.start
        buf_ref[0:CARRY_ROWS, 0:w] = carry_ref[:, cols]
        buf_ref[CARRY_ROWS:CARRY_ROWS + tm, 0:w] = _dot(h_ref[...], wup_ref[:, cols])
        carry_ref[:, cols] = buf_ref[tm:tm + CARRY_ROWS, 0:w]

    def conv(buf_ref, cols):
        w = cols.stop - cols.start
        c = CARRY_ROWS
        cw = cw_ref[:, cols]
        return (cw[0:1, :] * buf_ref[c - 2:c - 2 + tm, 0:w] + cw[1:2, :] * buf_ref[c - 1:c - 1 + tm, 0:w]
                + cw[2:3, :] * buf_ref[c:c + tm, 0:w])

    for c, (lo, hi) in enumerate(chunks):
        up(u_refs[2 * c], slice(lo, hi))
        up(u_refs[2 * c + 1], slice(D_FF + lo, D_FF + hi))

    for c, (lo, hi) in enumerate(chunks):
        a = conv(u_refs[2 * c], slice(lo, hi))
        gt = conv(u_refs[2 * c + 1], slice(D_FF + lo, D_FF + hi))
        act = (a * jax.nn.sigmoid(a) * gt).astype(BF16)
        part = _dot(act, wdn_ref[lo:hi, :])
        if c == 0:
            acc_ref[...] = x + part
        else:
            acc_ref[...] += part
    y = acc_ref[...]
    if final_norm:
        y = _rms_scale(y) * gf_ref[...]
    o_ref[0] = y


def _ffn(x, g, w_up, conv_w, w_down, *, attn=None, w_o=None, final_g=None, tm=512):
    b, s, d = x.shape
    has_attn = attn is not None
    final_norm = final_g is not None
    const = lambda *_: (0, 0)
    row = lambda i, j: (i, j, 0)
    one = pl.Buffered(1)
    args = [x]
    in_specs = [pl.BlockSpec((1, tm, d), row)]
    if has_attn:
        ka = attn.shape[-1]
        args += [attn, w_o]
        in_specs += [pl.BlockSpec((1, tm, ka), row),
                     pl.BlockSpec((ka, d), const, pipeline_mode=one)]
    args += [g.reshape(1, d), w_up.astype(BF16), conv_w, w_down.astype(BF16)]
    in_specs += [pl.BlockSpec((1, d), const),
                 pl.BlockSpec((d, 2 * D_FF), const, pipeline_mode=one),
                 pl.BlockSpec((CONV_W, 2 * D_FF), const),
                 pl.BlockSpec((D_FF, d), const, pipeline_mode=one)]
    if final_norm:
        args.append(final_g.reshape(1, d))
        in_specs.append(pl.BlockSpec((1, d), const))
    return pl.pallas_call(
        functools.partial(_ffn_kernel, tm=tm, has_attn=has_attn, final_norm=final_norm),
        grid=(b, s // tm),
        in_specs=in_specs,
        out_specs=pl.BlockSpec((1, tm, d), row),
        out_shape=jax.ShapeDtypeStruct((b, s, d), F32),
        scratch_shapes=[
            pltpu.VMEM((tm, d), BF16),
            pltpu.VMEM((CARRY_ROWS, 2 * D_FF), F32),
            pltpu.VMEM((tm, d), F32),
        ] + [pltpu.VMEM((CARRY_ROWS + tm, hi - lo), F32) for lo, hi in _ffn_chunks() for _ in range(2)],
        compiler_params=pltpu.CompilerParams(
            dimension_semantics=("arbitrary", "arbitrary"), vmem_limit_bytes=VMEM_LIMIT),
        name="ffn_attn" if has_attn else "ffn",
    )(*args)


def _proj_kernel(x_ref, gm_ref, gk_ref, wq_ref, wg_ref, wkc_ref, wkr_ref,
                 q_ref, gate_ref, kvc_ref, kvr_ref):
    xn = _rms_scale(x_ref[0])
    hq = (xn * gm_ref[...]).astype(BF16)
    hs = (xn * gk_ref[...]).astype(BF16)
    q_ref[0] = (_dot(hq, wq_ref[...]) * (HEAD_DIM ** -0.5)).astype(BF16)
    gates = jax.nn.sigmoid(_dot(hq, wg_ref[...]))
    gate_ref[0] = gates
    kvc = _dot(hs, wkc_ref[...])
    for sl in range(2 * N_GROUPS * HEAD_DIM // LANES):
        kvc_ref[0, sl] = kvc[:, sl * LANES:(sl + 1) * LANES]
    kvr_ref[0] = _dot(hs, wkr_ref[...]).astype(BF16)


def _proj(x, g_mix, g_kv, w_qg, w_kv, *, tm=512):
    b, s, d = x.shape
    hd = N_HEADS * HEAD_DIM
    gd = N_GROUPS * HEAD_DIM
    w_q = w_qg[:, :hd].astype(BF16)
    w_g = jnp.pad(w_qg[:, hd:], ((0, 0), (0, LANES - 3 * N_HEADS))).astype(BF16)
    w_kc = w_kv[:, :2 * gd].astype(BF16)
    w_kr = (w_kv[:, 2 * gd:].reshape(d, 4, N_GROUPS, HEAD_DIM)
            .transpose(0, 2, 1, 3).reshape(d, 4 * gd).astype(BF16))
    const = lambda *_: (0, 0)
    row = lambda i, j: (i, j, 0)
    one = pl.Buffered(1)
    return pl.pallas_call(
        _proj_kernel,
        grid=(b, s // tm),
        in_specs=[
            pl.BlockSpec((1, tm, d), row),
            pl.BlockSpec((1, d), const),
            pl.BlockSpec((1, d), const),
            pl.BlockSpec((d, hd), const, pipeline_mode=one),
            pl.BlockSpec((d, LANES), const, pipeline_mode=one),
            pl.BlockSpec((d, 2 * gd), const, pipeline_mode=one),
            pl.BlockSpec((d, 4 * gd), const, pipeline_mode=one),
        ],
        out_specs=[
            pl.BlockSpec((1, tm, hd), row),
            pl.BlockSpec((1, tm, LANES), row),
            pl.BlockSpec((1, 2 * gd // LANES, tm, LANES), lambda i, j: (i, 0, j, 0)),
            pl.BlockSpec((1, tm, 4 * gd), row),
        ],
        out_shape=[
            jax.ShapeDtypeStruct((b, s, hd), BF16),
            jax.ShapeDtypeStruct((b, s, LANES), F32),
            jax.ShapeDtypeStruct((b, 2 * gd // LANES, s, LANES), F32),
            jax.ShapeDtypeStruct((b, s, 4 * gd), BF16),
        ],
        compiler_params=pltpu.CompilerParams(
            dimension_semantics=("arbitrary", "arbitrary"), vmem_limit_bytes=VMEM_LIMIT),
        name="proj",
    )(x, g_mix.reshape(1, d), g_kv.reshape(1, d), w_q, w_g, w_kc, w_kr)


def _gelu_tanh(x):
    return 0.5 * x * (1.0 + jnp.tanh(math.sqrt(2.0 / math.pi) * (x + 0.044715 * (x * x * x))))


def _compress_kernel(k_ref, v_ref, pe_ref, w1_ref, b1_ref, w2_ref, o_ref):
    n_chunks = int(o_ref.shape[2])
    outs = [None, None]
    for j, src in enumerate((k_ref, v_ref)):
        first = None
        second = None
        for l in range(CMP_STRIDE):
            x = src.at[0, 0][pl.ds(l, n_chunks, stride=CMP_STRIDE), :]
            pa = _dot((x + pe_ref[j, l:l + 1, :]).astype(BF16), w1_ref[j, l])
            pb = _dot((x + pe_ref[j, CMP_STRIDE + l:CMP_STRIDE + l + 1, :]).astype(BF16),
                      w1_ref[j, CMP_STRIDE + l])
            first = pa if first is None else first + pa
            second = pb if second is None else second + pb
        pre = first + pltpu.roll(second, n_chunks - 1, 0) + b1_ref[j]
        hid = _gelu_tanh(pre).astype(BF16)
        for e in range(2):
            part = _dot(hid[:, e * CMP_HIDDEN:(e + 1) * CMP_HIDDEN], w2_ref[j])
            outs[e] = part if outs[e] is None else outs[e] + part
    for e in range(2):
        o_ref[0, e] = outs[e].astype(BF16)


def _compress(kvc, cmp_pe, cmp_w1, cmp_b1, cmp_w2):
    b, n_slabs, s, _ = kvc.shape
    n_chunks = s // CMP_STRIDE
    pairs = N_GROUPS // 2
    assert n_slabs == 2 * pairs and 2 * HEAD_DIM == LANES
    pe = jnp.concatenate([cmp_pe, cmp_pe], axis=-1)
    w1 = cmp_w1.reshape(2, CMP_LEN, HEAD_DIM, CMP_HIDDEN)
    zero = jnp.zeros_like(w1)
    w1 = jnp.concatenate([jnp.concatenate([w1, zero], axis=-1),
                          jnp.concatenate([zero, w1], axis=-1)], axis=2).astype(BF16)
    b1 = jnp.concatenate([cmp_b1, cmp_b1], axis=-1).reshape(2, 1, 2 * CMP_HIDDEN)
    w2 = jnp.stack([jnp.pad(cmp_w2[0], ((0, 0), (0, HEAD_DIM))),
                    jnp.pad(cmp_w2[1], ((0, 0), (HEAD_DIM, 0)))]).astype(BF16)
    const = lambda *_: (0, 0, 0)
    return pl.pallas_call(
        _compress_kernel,
        grid=(b, pairs),
        in_specs=[
            pl.BlockSpec((1, 1, s, LANES), lambda i, p: (i, p, 0, 0)),
            pl.BlockSpec((1, 1, s, LANES), lambda i, p: (i, pairs + p, 0, 0)),
            pl.BlockSpec((2, CMP_LEN, LANES), const),
            pl.BlockSpec((2, CMP_LEN, LANES, 2 * CMP_HIDDEN), lambda *_: (0, 0, 0, 0)),
            pl.BlockSpec((2, 1, 2 * CMP_HIDDEN), const),
            pl.BlockSpec((2, CMP_HIDDEN, LANES), const),
        ],
        out_specs=pl.BlockSpec((1, 2, n_chunks, LANES), lambda i, p: (i, p, 0, 0)),
        out_shape=jax.ShapeDtypeStruct((b, N_GROUPS, n_chunks, LANES), BF16),
        compiler_params=pltpu.CompilerParams(
            dimension_semantics=("arbitrary", "arbitrary"), vmem_limit_bytes=VMEM_LIMIT),
        name="compress",
    )(kvc, kvc, pe, w1, b1, w2)


TILE_PREV2, TILE_PREV, TILE_DIAG = 0, 1, 2
N_KINDS = 3
CMP_WIN = 32
LOG2E = 1.0 / math.log(2.0)


def _rel_bucket_np(dist):
    max_exact = N_BUCKETS // 2
    d = np.maximum(dist, 0)
    df = np.maximum(d, 1).astype(np.float32)
    large = max_exact + (np.log(df / max_exact) / np.float32(math.log(MAX_DISTANCE / max_exact))
                         * (N_BUCKETS - max_exact)).astype(np.int32)
    return np.where(d < max_exact, d, np.minimum(large, N_BUCKETS - 1))


def _toeplitz(w):
    t = w.shape[-1] // 2
    lead = w.shape[:-1]
    a = jnp.broadcast_to(w[..., None, :], lead + (t, 2 * t)).reshape(lead + (2 * t * t,))
    return a[..., :t * (2 * t - 1)].reshape(lead + (t, 2 * t - 1))[..., :t]


def _attn_tables(rel_bias, s):
    nq = s // TQ
    assert _rel_bucket_np(np.arange(TQ // 2, 4 * TQ)).min() == N_BUCKETS - 1
    relb = (rel_bias.astype(F32) - rel_bias[N_BUCKETS - 1].astype(F32)).T
    f = jnp.take(relb, jnp.asarray(_rel_bucket_np(np.arange(2 * TQ))), axis=1)
    neg = jnp.full((N_HEADS, TQ), NEG, F32)
    w_diag = jnp.concatenate([f[:, 0:1], neg, jnp.flip(f[:, 1:TQ], axis=1)], axis=1)
    w_prev = jnp.concatenate([jnp.flip(f[:, 1:TQ + 1], axis=1), jnp.zeros((N_HEADS, 1), F32),
                              jnp.flip(f[:, TQ + 1:2 * TQ], axis=1)], axis=1)
    ti = np.arange(TQ)[:, None]
    ki = np.arange(TQ)[None, :]
    prev2 = jnp.broadcast_to(jnp.asarray(np.where(ki > ti, 0.0, NEG), F32), (N_HEADS, TQ, TQ))
    b3 = jnp.stack([prev2, _toeplitz(w_prev), _toeplitz(w_diag)], axis=1)
    b3 = (b3.reshape(N_GROUPS, HPG, N_KINDS, TQ, TQ).transpose(0, 2, 1, 3, 4)
          .reshape(N_GROUPS, N_KINDS, HPG * TQ, TQ))
    n_cmp = s // CMP_STRIDE
    lead = CMP_WIN // 4
    dist = ti - CMP_STRIDE * (np.arange(CMP_WIN)[None, :] - lead) - (CMP_LEN - 1)
    pw = jnp.take(relb, jnp.asarray(_rel_bucket_np(dist)), axis=1)
    pw = jnp.where(jnp.asarray(dist >= 0), pw, NEG)
    full = jnp.concatenate([jnp.zeros((N_HEADS, TQ, n_cmp - lead), F32), pw,
                            jnp.full((N_HEADS, TQ, n_cmp), NEG, F32)], axis=-1)
    per_tile = TQ // CMP_STRIDE
    bc = jnp.stack([full[..., n_cmp - per_tile * t:2 * n_cmp - per_tile * t] for t in range(nq)])
    bc = bc.reshape(nq, N_GROUPS, HPG * TQ, n_cmp)
    return b3, bc


def _attn_consts(s):
    n_slc = s // SLC_LEN
    n_cmp = s // CMP_STRIDE
    key = np.arange(s)[:, None]
    lane = np.arange(LANES)[None, :]
    ceneg = np.where((key // SLC_LEN == lane) & (lane < n_slc), NEG, 0.0)
    j = np.arange(n_slc)[:, None]
    i = np.arange(n_cmp)[None, :]
    ov = ((i * CMP_STRIDE < j * SLC_LEN + SLC_LEN) & (i * CMP_STRIDE + CMP_LEN > j * SLC_LEN)
          & (i < n_cmp - 1))
    place = np.eye(n_slc, LANES)
    return (jnp.asarray(ceneg, BF16), jnp.asarray(np.concatenate([ov] * 3, axis=1), BF16),
            jnp.asarray(place, BF16))


def _attn_kernel(q_ref, gate_ref, kv_ref, kvc_ref, b3_ref, bc_ref, ceneg_ref, ov_ref, place_ref,
                 prev_ref, o_ref, q4_ref, sc_ref, nsc_ref, macc_ref, mb_ref, oacc_ref, comb_ref,
                 *, qb):
    del prev_ref
    n_slc = ov_ref.shape[0]
    rows = HPG * TQ
    lane = lax.broadcasted_iota(jnp.int32, (TQ, LANES), 1)
    low = lane < HEAD_DIM

    for pair in range(HPG // 2):
        qp = q_ref[0, :, pair * LANES:(pair + 1) * LANES].astype(F32)
        q4_ref[(2 * pair) * TQ:(2 * pair + 1) * TQ, 0:LANES] = jnp.where(low, qp, 0.0).astype(BF16)
        q4_ref[(2 * pair + 1) * TQ:(2 * pair + 2) * TQ, 0:LANES] = (
            jnp.where(low, pltpu.roll(qp, HEAD_DIM, 1), 0.0).astype(BF16))
    q4 = q4_ref[:, 0:LANES]

    def key_rows(kt, col0, n=1):
        return kv_ref[0, kt * TQ:(kt + n) * TQ, col0:col0 + LANES]

    def lane_tiles(x):
        return [x[:, i:i + LANES] for i in range(0, x.shape[1], LANES)]

    def tile_max(x):
        return functools.reduce(jnp.maximum, lane_tiles(x))

    def exp2_rel(sv, mb):
        return jnp.concatenate([jnp.exp2(t - mb) for t in lane_tiles(sv)], axis=1).astype(BF16)

    win_tiles = [(kt, kind) for kt, kind in
                 ((qb - 2, TILE_PREV2), (qb - 1, TILE_PREV), (qb, TILE_DIAG)) if kt >= 0]
    near_tiles = win_tiles[-2:]
    n_far = max(qb - 1, 0)
    far_groups = [(t, min(2, n_far - t)) for t in range(0, n_far, 2)]

    def split3(x):
        hi = x.astype(BF16)
        r1 = x - hi.astype(F32)
        mid = r1.astype(BF16)
        lo = (r1 - mid.astype(F32)).astype(BF16)
        return jnp.concatenate([hi, mid, lo], axis=1)

    def ones_v(tile):
        return jnp.concatenate([tile, jnp.ones_like(tile)], axis=1)

    def weighted(o, hh, br):
        r = slice(hh * TQ, (hh + 1) * TQ)
        l = o[r, LANES:2 * LANES]
        gate = gate_ref[0, 0, :, 3 * hh + br:3 * hh + br + 1]
        return o[r, 0:LANES] * (gate / jnp.where(l > 0.0, l, 1.0))

    tiles_w = [key_rows(kt, LANES) for kt, _ in win_tiles]
    s_w = []

    def win_scores():
        i = len(s_w)
        if i < len(win_tiles):
            s_w.append((_split_rows(_dot_nt, q4, tiles_w[i]) + b3_ref[0, win_tiles[i][1]]) * LOG2E)

    kvc = kvc_ref[0, 0]
    bias_c = bc_ref[0, 0]
    s_c = _split_rows(_dot_nt, q4, kvc) + bias_c
    win_scores()
    m_c = jnp.max(s_c, axis=1, keepdims=True)
    e_c = jnp.where(bias_c > 0.5 * NEG, jnp.exp(s_c - m_c), 0.0)
    o_c = _split_rows(_dot, e_c.astype(BF16), ones_v(kvc))
    win_scores()
    l_c = o_c[:, LANES:2 * LANES]
    p_c = e_c / jnp.where(l_c > 0.0, l_c, 1.0)
    psum = p_c[0:TQ]
    for hh in range(1, HPG):
        psum = psum + p_c[hh * TQ:(hh + 1) * TQ]

    imp_t = _dot_nt(ov_ref[...], split3(psum))
    win_scores()
    s_w = jnp.concatenate(s_w, axis=1)
    m_w = jnp.max(tile_max(s_w), axis=1, keepdims=True)
    p_w = exp2_rel(s_w, jnp.broadcast_to(m_w, (rows, LANES)))
    o_w = _split_rows(_dot, p_w, ones_v(jnp.concatenate(tiles_w, axis=0)))

    jrow = lax.broadcasted_iota(jnp.int32, (n_slc, TQ), 0)
    tpos = qb * TQ + lax.broadcasted_iota(jnp.int32, (n_slc, TQ), 1)
    cur = lax.shift_right_logical(tpos, int(math.log2(SLC_LEN)))
    forced = (jrow == 0) | (jrow == cur) | (jrow == cur - 1)
    score = jnp.where(forced, 3e38, jnp.where(jrow <= cur, imp_t, -1.0))
    rank = jnp.zeros((n_slc, TQ), F32)
    for k in range(n_slc):
        sk = score[k:k + 1, :]
        beats = (sk > score) | ((sk == score) & (jrow > k))
        rank = rank + beats.astype(F32)
    notsel_t = (rank >= float(N_SEL)).astype(BF16)
    q_mask = lax.dot_general(notsel_t, place_ref[...], (((0,), (0,)), ((), ())),
                             preferred_element_type=F32).astype(BF16)
    for hh in range(HPG):
        q4_ref[hh * TQ:(hh + 1) * TQ, LANES:2 * LANES] = q_mask

    for hh in range(HPG):
        comb_ref[hh * TQ:(hh + 1) * TQ, :] = weighted(o_c, hh, 0) + weighted(o_w, hh, 2)

    kt_n, n_near = near_tiles[0][0], len(near_tiles)
    ks_n = slice(kt_n * TQ, (kt_n + n_near) * TQ)
    bias_n = jnp.concatenate([b3_ref[0, kind] for _, kind in near_tiles], axis=1)
    rhs_n = jnp.concatenate([key_rows(kt_n, 0, n_near), ceneg_ref[ks_n, :]], axis=1)
    dot_n = _dot_nt if n_near == 2 else functools.partial(_split_rows, _dot_nt)
    s_n = (dot_n(q4_ref[...], rhs_n) + bias_n) * LOG2E
    nsc_ref[...] = s_n
    macc_ref[len(far_groups)] = tile_max(s_n)

    for i, (kt, n) in enumerate(far_groups):
        ks = slice(kt * TQ, (kt + n) * TQ)
        rhs = jnp.concatenate([key_rows(kt, 0, n), ceneg_ref[ks, :]], axis=1)
        dot_f = _dot_nt if n == 2 else functools.partial(_split_rows, _dot_nt)
        sv = dot_f(q4_ref[...], rhs) * LOG2E
        sc_ref[i, :, 0:n * TQ] = sv
        macc_ref[i] = tile_max(sv)

    m_el = macc_ref[0]
    for i in range(1, len(far_groups) + 1):
        m_el = jnp.maximum(m_el, macc_ref[i])
    m_s = jnp.max(m_el, axis=1, keepdims=True)
    mb_ref[...] = jnp.broadcast_to(m_s, (rows, LANES))
    oacc_ref[...] = jnp.zeros((rows, 2 * LANES), F32)

    for i, (kt, n) in enumerate(far_groups):
        rhs = ones_v(key_rows(kt, 0, n))
        for r in (slice(0, rows // 2), slice(rows // 2, rows)):
            oacc_ref[r, :] += _dot(exp2_rel(sc_ref[i, r, 0:n * TQ], mb_ref[r, :]), rhs)

    p_n = exp2_rel(nsc_ref[...], mb_ref[...])
    o_s = oacc_ref[...] + _split_rows(_dot, p_n, ones_v(key_rows(kt_n, 0, n_near)))

    for hp in range(HPG // 2):
        even, odd = [comb_ref[hh * TQ:(hh + 1) * TQ, :] + weighted(o_s, hh, 1)
                     for hh in (2 * hp, 2 * hp + 1)]
        o_ref[0, :, hp * LANES:(hp + 1) * LANES] = (
            jnp.where(low, pltpu.roll(even, HEAD_DIM, 1), odd).astype(BF16))


def _attention(q, gates, kvr, kvcmp, rel_bias):
    b, s, hd = q.shape
    nq = s // TQ
    n_slc = s // SLC_LEN
    n_cmp = s // CMP_STRIDE
    rows = HPG * TQ
    b3, bc = _attn_tables(rel_bias, s)
    ceneg, ov, place = _attn_consts(s)
    gw = 3 * HPG
    gates_g = (gates[:, :, :3 * N_HEADS].reshape(b, s, N_GROUPS, gw).transpose(0, 2, 1, 3))
    gl = HPG * HEAD_DIM
    out = jnp.zeros((b, s, N_HEADS * HEAD_DIM), BF16)
    for qb in range(nq):
        n_groups = (max(qb - 1, 0) + 1) // 2
        n_near = min(qb + 1, 2)
        in_specs = [
            pl.BlockSpec((1, TQ, gl), lambda g, i, qb=qb: (i, qb, g)),
            pl.BlockSpec((1, 1, TQ, gw), lambda g, i, qb=qb: (i, g, qb, 0)),
            pl.BlockSpec((1, s, 4 * HEAD_DIM), lambda g, i: (i, 0, g)),
            pl.BlockSpec((1, 1, n_cmp, LANES), lambda g, i: (i, g, 0, 0)),
            pl.BlockSpec((1, N_KINDS, rows, TQ), lambda g, i: (g, 0, 0, 0)),
            pl.BlockSpec((1, 1, rows, n_cmp), lambda g, i, qb=qb: (qb, g, 0, 0)),
            pl.BlockSpec((s, LANES), lambda *_: (0, 0)),
            pl.BlockSpec((n_slc, 3 * n_cmp), lambda *_: (0, 0)),
            pl.BlockSpec((n_slc, LANES), lambda *_: (0, 0)),
            pl.BlockSpec(memory_space=pl.ANY),
        ]
        args = [q, gates_g, kvr, kvcmp, b3, bc, ceneg, ov, place, out]
        out = pl.pallas_call(
            functools.partial(_attn_kernel, qb=qb),
            grid=(N_GROUPS, b),
            in_specs=in_specs,
            out_specs=pl.BlockSpec((1, TQ, gl), lambda g, i, qb=qb: (i, qb, g)),
            out_shape=jax.ShapeDtypeStruct((b, s, N_HEADS * HEAD_DIM), BF16),
            scratch_shapes=[
                pltpu.VMEM((rows, 2 * LANES), BF16),
                pltpu.VMEM((max(n_groups, 1), rows, 2 * TQ), F32),
                pltpu.VMEM((rows, n_near * TQ), F32),
                pltpu.VMEM((n_groups + 1, rows, LANES), F32),
                pltpu.VMEM((rows, LANES), F32),
                pltpu.VMEM((rows, 2 * LANES), F32),
                pltpu.VMEM((rows, LANES), F32),
            ],
            input_output_aliases={len(args) - 1: 0},
            compiler_params=pltpu.CompilerParams(
                dimension_semantics=("arbitrary", "arbitrary"), vmem_limit_bytes=VMEM_LIMIT),
            name=f"nsa_attn{qb}",
        )(*args)
    return out


def kernel(x, mix_norm, a_w_in, a_conv, a_w_out, ffn_norm, ffn_up, ffn_conv, ffn_down,
           kv_norm, w_kv, cmp_pe, cmp_w1, cmp_b1, cmp_w2, b_w_qg, b_w_o, rel_bias, final_norm):
    b, s, d = x.shape
    assert d == D_MODEL and s % 512 == 0 and mix_norm.shape[0] == 2
    x = _mixer(x, mix_norm[0], a_w_in[0], a_conv[0], a_w_out[0])
    x = _ffn(x, ffn_norm[0], ffn_up[0], ffn_conv[0], ffn_down[0])
    q, gates, kvc, kvr = _proj(x, mix_norm[1], kv_norm, b_w_qg[0], w_kv)
    kvcmp = _compress(kvc, cmp_pe, cmp_w1, cmp_b1, cmp_w2)
    attn = _attention(q, gates, kvr, kvcmp, rel_bias)
    return _ffn(x, ffn_norm[1], ffn_up[1], ffn_conv[1], ffn_down[1],
                attn=attn, w_o=b_w_o[0].astype(BF16), final_g=final_norm)
```

```python
import functools
import math

import jax
import jax.numpy as jnp
import numpy as np
from jax import lax
from jax.experimental import pallas as pl
from jax.experimental.pallas import tpu as pltpu

D_MODEL = 1024
CONV_W = 3
D_FF = 2816
N_HEADS = 16
N_GROUPS = 4
HPG = N_HEADS // N_GROUPS
HEAD_DIM = 64
CMP_LEN = 32
CMP_STRIDE = 16
CMP_HIDDEN = 128
SLC_LEN = 64
N_SEL = 16
WINDOW = 512
N_BUCKETS = 32
MAX_DISTANCE = 128
EPS = 1e-6
NEG = -1e30

LANES = 128
MXU_COLS = 256
V7X_VMEM_BYTES = 64 * 2**20
VMEM_LIMIT = V7X_VMEM_BYTES - 8 * 2**20

TQ = 256
CARRY_ROWS = 8
F32 = jnp.float32
BF16 = jnp.bfloat16


def _dot(a, b):
    return jnp.dot(a, b, preferred_element_type=F32)


def _dot_nt(a, b, precision=None):
    return lax.dot_general(a, b, (((1,), (1,)), ((), ())),
                           preferred_element_type=F32, precision=precision)


def _split_rows(dot, a, b):
    h = a.shape[0] // 2
    return jnp.concatenate([dot(a[:h], b), dot(a[h:], b)], axis=0)


def _rms_scale(x):
    return x * lax.rsqrt(jnp.mean(x * x, axis=-1, keepdims=True) + EPS)


def _conv3(buf_ref, cw, rows):
    c = CARRY_ROWS
    return (cw[0:1, :] * buf_ref[c - 2:c - 2 + rows, :]
            + cw[1:2, :] * buf_ref[c - 1:c - 1 + rows, :]
            + cw[2:3, :] * buf_ref[c:c + rows, :])


def _mixer_kernel(x_ref, g_ref, win_ref, cw_ref, wout_ref, o_ref, ubuf_ref, *, tm):
    d = D_MODEL

    @pl.when(pl.program_id(1) == 0)
    def _():
        ubuf_ref[0:CARRY_ROWS, :] = jnp.zeros((CARRY_ROWS, d), F32)

    x = x_ref[0]
    h = (_rms_scale(x) * g_ref[...]).astype(BF16)
    cg = _dot(h, win_ref[:, d:2 * d])
    v = _dot(h, win_ref[:, 2 * d:3 * d])
    ubuf_ref[CARRY_ROWS:CARRY_ROWS + tm, :] = cg * v
    conv = _conv3(ubuf_ref, cw_ref[...], tm)
    ubuf_ref[0:CARRY_ROWS, :] = ubuf_ref[tm:tm + CARRY_ROWS, :]
    bg = _dot(h, win_ref[:, 0:d])
    y = (bg * conv).astype(BF16)
    o_ref[0] = x + _dot(y, wout_ref[...])


def _mixer(x, g, w_in, conv_w, w_out, *, tm=512):
    b, s, d = x.shape
    const = lambda *_: (0, 0)
    one = pl.Buffered(1)
    return pl.pallas_call(
        functools.partial(_mixer_kernel, tm=tm),
        grid=(b, s // tm),
        in_specs=[
            pl.BlockSpec((1, tm, d), lambda i, j: (i, j, 0)),
            pl.BlockSpec((1, d), const),
            pl.BlockSpec((d, 3 * d), const, pipeline_mode=one),
            pl.BlockSpec((CONV_W, d), const),
            pl.BlockSpec((d, d), const, pipeline_mode=one),
        ],
        out_specs=pl.BlockSpec((1, tm, d), lambda i, j: (i, j, 0)),
        out_shape=jax.ShapeDtypeStruct((b, s, d), F32),
        scratch_shapes=[pltpu.VMEM((CARRY_ROWS + tm, d), F32)],
        compiler_params=pltpu.CompilerParams(
            dimension_semantics=("arbitrary", "arbitrary"), vmem_limit_bytes=VMEM_LIMIT),
        name="mixer",
    )(x, g.reshape(1, d), w_in.astype(BF16), conv_w, w_out.astype(BF16))


def _ffn_chunks():
    tiles = D_FF // MXU_COLS
    assert tiles * MXU_COLS == D_FF
    first = (tiles + 1) // 2 * MXU_COLS
    return [(0, first), (first, D_FF)]


def _ffn_kernel(*refs, tm, has_attn, final_norm):
    refs = list(refs)
    x_ref = refs.pop(0)
    a_ref = refs.pop(0) if has_attn else None
    wo_ref = refs.pop(0) if has_attn else None
    g_ref, wup_ref, cw_ref, wdn_ref = refs[:4]
    refs = refs[4:]
    gf_ref = refs.pop(0) if final_norm else None
    chunks = _ffn_chunks()
    o_ref, h_ref, carry_ref, acc_ref = refs[:4]
    u_refs = refs[4:]
    assert len(u_refs) == 2 * len(chunks)

    @pl.when(pl.program_id(1) == 0)
    def _():
        carry_ref[...] = jnp.zeros(carry_ref.shape, F32)

    x = x_ref[0]
    if has_attn:
        x = x + _dot(a_ref[0], wo_ref[...])
    h_ref[...] = (_rms_scale(x) * g_ref[...]).astype(BF16)

    def up(buf_ref, cols):
        w = cols.stop - cols.start
        buf_ref[0:CARRY_ROWS, 0:w] = carry_ref[:, cols]
        buf_ref[CARRY_ROWS:CARRY_ROWS + tm, 0:w] = _dot(h_ref[...], wup_ref[:, cols])
        carry_ref[:, cols] = buf_ref[tm:tm + CARRY_ROWS, 0:w]

    def conv(buf_ref, cols):
        w = cols.stop - cols.start
        c = CARRY_ROWS
        cw = cw_ref[:, cols]
        return (cw[0:1, :] * buf_ref[c - 2:c - 2 + tm, 0:w] + cw[1:2, :] * buf_ref[c - 1:c - 1 + tm, 0:w]
                + cw[2:3, :] * buf_ref[c:c + tm, 0:w])

    for c, (lo, hi) in enumerate(chunks):
        up(u_refs[2 * c], slice(lo, hi))
        up(u_refs[2 * c + 1], slice(D_FF + lo, D_FF + hi))

    for c, (lo, hi) in enumerate(chunks):
        a = conv(u_refs[2 * c], slice(lo, hi))
        gt = conv(u_refs[2 * c + 1], slice(D_FF + lo, D_FF + hi))
        act = (a * jax.nn.sigmoid(a) * gt).astype(BF16)
        part = _dot(act, wdn_ref[lo:hi, :])
        if c == 0:
            acc_ref[...] = x + part
        else:
            acc_ref[...] += part
    y = acc_ref[...]
    if final_norm:
        y = _rms_scale(y) * gf_ref[...]
    o_ref[0] = y


def _ffn(x, g, w_up, conv_w, w_down, *, attn=None, w_o=None, final_g=None, tm=512):
    b, s, d = x.shape
    has_attn = attn is not None
    final_norm = final_g is not None
    const = lambda *_: (0, 0)
    row = lambda i, j: (i, j, 0)
    one = pl.Buffered(1)
    args = [x]
    in_specs = [pl.BlockSpec((1, tm, d), row)]
    if has_attn:
        ka = attn.shape[-1]
        args += [attn, w_o]
        in_specs += [pl.BlockSpec((1, tm, ka), row),
                     pl.BlockSpec((ka, d), const, pipeline_mode=one)]
    args += [g.reshape(1, d), w_up.astype(BF16), conv_w, w_down.astype(BF16)]
    in_specs += [pl.BlockSpec((1, d), const),
                 pl.BlockSpec((d, 2 * D_FF), const, pipeline_mode=one),
                 pl.BlockSpec((CONV_W, 2 * D_FF), const),
                 pl.BlockSpec((D_FF, d), const, pipeline_mode=one)]
    if final_norm:
        args.append(final_g.reshape(1, d))
        in_specs.append(pl.BlockSpec((1, d), const))
    return pl.pallas_call(
        functools.partial(_ffn_kernel, tm=tm, has_attn=has_attn, final_norm=final_norm),
        grid=(b, s // tm),
        in_specs=in_specs,
        out_specs=pl.BlockSpec((1, tm, d), row),
        out_shape=jax.ShapeDtypeStruct((b, s, d), F32),
        scratch_shapes=[
            pltpu.VMEM((tm, d), BF16),
            pltpu.VMEM((CARRY_ROWS, 2 * D_FF), F32),
            pltpu.VMEM((tm, d), F32),
        ] + [pltpu.VMEM((CARRY_ROWS + tm, hi - lo), F32) for lo, hi in _ffn_chunks() for _ in range(2)],
        compiler_params=pltpu.CompilerParams(
            dimension_semantics=("arbitrary", "arbitrary"), vmem_limit_bytes=VMEM_LIMIT),
        name="ffn_attn" if has_attn else "ffn",
    )(*args)


def _proj_kernel(x_ref, gm_ref, gk_ref, wq_ref, wg_ref, wkc_ref, wkr_ref,
                 q_ref, gate_ref, kvc_ref, kvr_ref):
    xn = _rms_scale(x_ref[0])
    hq = (xn * gm_ref[...]).astype(BF16)
    hs = (xn * gk_ref[...]).astype(BF16)
    q_ref[0] = (_dot(hq, wq_ref[...]) * (HEAD_DIM ** -0.5)).astype(BF16)
    gates = jax.nn.sigmoid(_dot(hq, wg_ref[...]))
    gate_ref[0] = gates
    kvc = _dot(hs, wkc_ref[...])
    for sl in range(2 * N_GROUPS * HEAD_DIM // LANES):
        kvc_ref[0, sl] = kvc[:, sl * LANES:(sl + 1) * LANES]
    kvr_ref[0] = _dot(hs, wkr_ref[...]).astype(BF16)


def _proj(x, g_mix, g_kv, w_qg, w_kv, *, tm=512):
    b, s, d = x.shape
    hd = N_HEADS * HEAD_DIM
    gd = N_GROUPS * HEAD_DIM
    w_q = w_qg[:, :hd].astype(BF16)
    w_g = jnp.pad(w_qg[:, hd:], ((0, 0), (0, LANES - 3 * N_HEADS))).astype(BF16)
    w_kc = w_kv[:, :2 * gd].astype(BF16)
    w_kr = (w_kv[:, 2 * gd:].reshape(d, 4, N_GROUPS, HEAD_DIM)
            .transpose(0, 2, 1, 3).reshape(d, 4 * gd).astype(BF16))
    const = lambda *_: (0, 0)
    row = lambda i, j: (i, j, 0)
    one = pl.Buffered(1)
    return pl.pallas_call(
        _proj_kernel,
        grid=(b, s // tm),
        in_specs=[
            pl.BlockSpec((1, tm, d), row),
            pl.BlockSpec((1, d), const),
            pl.BlockSpec((1, d), const),
            pl.BlockSpec((d, hd), const, pipeline_mode=one),
            pl.BlockSpec((d, LANES), const, pipeline_mode=one),
            pl.BlockSpec((d, 2 * gd), const, pipeline_mode=one),
            pl.BlockSpec((d, 4 * gd), const, pipeline_mode=one),
        ],
        out_specs=[
            pl.BlockSpec((1, tm, hd), row),
            pl.BlockSpec((1, tm, LANES), row),
            pl.BlockSpec((1, 2 * gd // LANES, tm, LANES), lambda i, j: (i, 0, j, 0)),
            pl.BlockSpec((1, tm, 4 * gd), row),
        ],
        out_shape=[
            jax.ShapeDtypeStruct((b, s, hd), BF16),
            jax.ShapeDtypeStruct((b, s, LANES), F32),
            jax.ShapeDtypeStruct((b, 2 * gd // LANES, s, LANES), F32),
            jax.ShapeDtypeStruct((b, s, 4 * gd), BF16),
        ],
        compiler_params=pltpu.CompilerParams(
            dimension_semantics=("arbitrary", "arbitrary"), vmem_limit_bytes=VMEM_LIMIT),
        name="proj",
    )(x, g_mix.reshape(1, d), g_kv.reshape(1, d), w_q, w_g, w_kc, w_kr)


def _gelu_tanh(x):
    return 0.5 * x * (1.0 + jnp.tanh(math.sqrt(2.0 / math.pi) * (x + 0.044715 * (x * x * x))))


def _compress_kernel(k_ref, v_ref, pe_ref, w1_ref, b1_ref, w2_ref, o_ref):
    n_chunks = int(o_ref.shape[2])
    outs = [None, None]
    for j, src in enumerate((k_ref, v_ref)):
        first = None
        second = None
        for l in range(CMP_STRIDE):
            x = src.at[0, 0][pl.ds(l, n_chunks, stride=CMP_STRIDE), :]
            pa = _dot((x + pe_ref[j, l:l + 1, :]).astype(BF16), w1_ref[j, l])
            pb = _dot((x + pe_ref[j, CMP_STRIDE + l:CMP_STRIDE + l + 1, :]).astype(BF16),
                      w1_ref[j, CMP_STRIDE + l])
            first = pa if first is None else first + pa
            second = pb if second is None else second + pb
        pre = first + pltpu.roll(second, n_chunks - 1, 0) + b1_ref[j]
        hid = _gelu_tanh(pre).astype(BF16)
        for e in range(2):
            part = _dot(hid[:, e * CMP_HIDDEN:(e + 1) * CMP_HIDDEN], w2_ref[j])
            outs[e] = part if outs[e] is None else outs[e] + part
    for e in range(2):
        o_ref[0, e] = outs[e].astype(BF16)


def _compress(kvc, cmp_pe, cmp_w1, cmp_b1, cmp_w2):
    b, n_slabs, s, _ = kvc.shape
    n_chunks = s // CMP_STRIDE
    pairs = N_GROUPS // 2
    assert n_slabs == 2 * pairs and 2 * HEAD_DIM == LANES
    pe = jnp.concatenate([cmp_pe, cmp_pe], axis=-1)
    w1 = cmp_w1.reshape(2, CMP_LEN, HEAD_DIM, CMP_HIDDEN)
    zero = jnp.zeros_like(w1)
    w1 = jnp.concatenate([jnp.concatenate([w1, zero], axis=-1),
                          jnp.concatenate([zero, w1], axis=-1)], axis=2).astype(BF16)
    b1 = jnp.concatenate([cmp_b1, cmp_b1], axis=-1).reshape(2, 1, 2 * CMP_HIDDEN)
    w2 = jnp.stack([jnp.pad(cmp_w2[0], ((0, 0), (0, HEAD_DIM))),
                    jnp.pad(cmp_w2[1], ((0, 0), (HEAD_DIM, 0)))]).astype(BF16)
    const = lambda *_: (0, 0, 0)
    return pl.pallas_call(
        _compress_kernel,
        grid=(b, pairs),
        in_specs=[
            pl.BlockSpec((1, 1, s, LANES), lambda i, p: (i, p, 0, 0)),
            pl.BlockSpec((1, 1, s, LANES), lambda i, p: (i, pairs + p, 0, 0)),
            pl.BlockSpec((2, CMP_LEN, LANES), const),
            pl.BlockSpec((2, CMP_LEN, LANES, 2 * CMP_HIDDEN), lambda *_: (0, 0, 0, 0)),
            pl.BlockSpec((2, 1, 2 * CMP_HIDDEN), const),
            pl.BlockSpec((2, CMP_HIDDEN, LANES), const),
        ],
        out_specs=pl.BlockSpec((1, 2, n_chunks, LANES), lambda i, p: (i, p, 0, 0)),
        out_shape=jax.ShapeDtypeStruct((b, N_GROUPS, n_chunks, LANES), BF16),
        compiler_params=pltpu.CompilerParams(
            dimension_semantics=("arbitrary", "arbitrary"), vmem_limit_bytes=VMEM_LIMIT),
        name="compress",
    )(kvc, kvc, pe, w1, b1, w2)


TILE_PREV2, TILE_PREV, TILE_DIAG = 0, 1, 2
N_KINDS = 3
CMP_WIN = 32
LOG2E = 1.0 / math.log(2.0)


def _rel_bucket_np(dist):
    max_exact = N_BUCKETS // 2
    d = np.maximum(dist, 0)
    df = np.maximum(d, 1).astype(np.float32)
    large = max_exact + (np.log(df / max_exact) / np.float32(math.log(MAX_DISTANCE / max_exact))
                         * (N_BUCKETS - max_exact)).astype(np.int32)
    return np.where(d < max_exact, d, np.minimum(large, N_BUCKETS - 1))


def _toeplitz(w):
    t = w.shape[-1] // 2
    lead = w.shape[:-1]
    a = jnp.broadcast_to(w[..., None, :], lead + (t, 2 * t)).reshape(lead + (2 * t * t,))
    return a[..., :t * (2 * t - 1)].reshape(lead + (t, 2 * t - 1))[..., :t]


def _attn_tables(rel_bias, s):
    nq = s // TQ
    assert _rel_bucket_np(np.arange(TQ // 2, 4 * TQ)).min() == N_BUCKETS - 1
    relb = (rel_bias.astype(F32) - rel_bias[N_BUCKETS - 1].astype(F32)).T
    f = jnp.take(relb, jnp.asarray(_rel_bucket_np(np.arange(2 * TQ))), axis=1)
    neg = jnp.full((N_HEADS, TQ), NEG, F32)
    w_diag = jnp.concatenate([f[:, 0:1], neg, jnp.flip(f[:, 1:TQ], axis=1)], axis=1)
    w_prev = jnp.concatenate([jnp.flip(f[:, 1:TQ + 1], axis=1), jnp.zeros((N_HEADS, 1), F32),
                              jnp.flip(f[:, TQ + 1:2 * TQ], axis=1)], axis=1)
    ti = np.arange(TQ)[:, None]
    ki = np.arange(TQ)[None, :]
    prev2 = jnp.broadcast_to(jnp.asarray(np.where(ki > ti, 0.0, NEG), F32), (N_HEADS, TQ, TQ))
    b3 = jnp.stack([prev2, _toeplitz(w_prev), _toeplitz(w_diag)], axis=1)
    b3 = (b3.reshape(N_GROUPS, HPG, N_KINDS, TQ, TQ).transpose(0, 2, 1, 3, 4)
          .reshape(N_GROUPS, N_KINDS, HPG * TQ, TQ))
    n_cmp = s // CMP_STRIDE
    lead = CMP_WIN // 4
    dist = ti - CMP_STRIDE * (np.arange(CMP_WIN)[None, :] - lead) - (CMP_LEN - 1)
    pw = jnp.take(relb, jnp.asarray(_rel_bucket_np(dist)), axis=1)
    pw = jnp.where(jnp.asarray(dist >= 0), pw, NEG)
    full = jnp.concatenate([jnp.zeros((N_HEADS, TQ, n_cmp - lead), F32), pw,
                            jnp.full((N_HEADS, TQ, n_cmp), NEG, F32)], axis=-1)
    per_tile = TQ // CMP_STRIDE
    bc = jnp.stack([full[..., n_cmp - per_tile * t:2 * n_cmp - per_tile * t] for t in range(nq)])
    bc = bc.reshape(nq, N_GROUPS, HPG * TQ, n_cmp)
    return b3, bc


def _attn_consts(s):
    n_slc = s // SLC_LEN
    n_cmp = s // CMP_STRIDE
    key = np.arange(s)[:, None]
    lane = np.arange(LANES)[None, :]
    ceneg = np.where((key // SLC_LEN == lane) & (lane < n_slc), NEG, 0.0)
    j = np.arange(n_slc)[:, None]
    i = np.arange(n_cmp)[None, :]
    ov = ((i * CMP_STRIDE < j * SLC_LEN + SLC_LEN) & (i * CMP_STRIDE + CMP_LEN > j * SLC_LEN)
          & (i < n_cmp - 1))
    place = np.eye(n_slc, LANES)
    return (jnp.asarray(ceneg, BF16), jnp.asarray(np.concatenate([ov] * 3, axis=1), BF16),
            jnp.asarray(place, BF16))


BATCH_PER_STEP = 2


def _attn_kernel(q_ref, gate_ref, kv_ref, kvc_ref, b3_ref, bc_ref, ceneg_ref, ov_ref, place_ref,
                 prev_ref, o_ref, *scratch, qb):
    del prev_ref
    streams = [
        _attn_stream(q_ref.at[pl.ds(e, 1)], gate_ref.at[pl.ds(e, 1)], kv_ref.at[pl.ds(e, 1)],
                     kvc_ref.at[pl.ds(e, 1)], b3_ref, bc_ref, ceneg_ref, ov_ref, place_ref,
                     o_ref.at[pl.ds(e, 1)], *[s.at[e] for s in scratch], qb=qb)
        for e in range(BATCH_PER_STEP)]
    while streams:
        for stream in list(streams):
            if next(stream, True):
                streams.remove(stream)


def _attn_stream(q_ref, gate_ref, kv_ref, kvc_ref, b3_ref, bc_ref, ceneg_ref, ov_ref, place_ref,
                 o_ref, q4_ref, sc_ref, nsc_ref, macc_ref, mb_ref, oacc_ref, comb_ref, *, qb):
    n_slc = ov_ref.shape[0]
    rows = HPG * TQ
    lane = lax.broadcasted_iota(jnp.int32, (TQ, LANES), 1)
    low = lane < HEAD_DIM

    for pair in range(HPG // 2):
        qp = q_ref[0, :, pair * LANES:(pair + 1) * LANES].astype(F32)
        q4_ref[(2 * pair) * TQ:(2 * pair + 1) * TQ, 0:LANES] = jnp.where(low, qp, 0.0).astype(BF16)
        q4_ref[(2 * pair + 1) * TQ:(2 * pair + 2) * TQ, 0:LANES] = (
            jnp.where(low, pltpu.roll(qp, HEAD_DIM, 1), 0.0).astype(BF16))
    q4 = q4_ref[:, 0:LANES]

    def key_rows(kt, col0, n=1):
        return kv_ref[0, kt * TQ:(kt + n) * TQ, col0:col0 + LANES]

    def lane_tiles(x):
        return [x[:, i:i + LANES] for i in range(0, x.shape[1], LANES)]

    def tile_max(x):
        return functools.reduce(jnp.maximum, lane_tiles(x))

    def exp2_rel(sv, mb):
        return jnp.concatenate([jnp.exp2(t - mb) for t in lane_tiles(sv)], axis=1).astype(BF16)

    win_tiles = [(kt, kind) for kt, kind in
                 ((qb - 2, TILE_PREV2), (qb - 1, TILE_PREV), (qb, TILE_DIAG)) if kt >= 0]
    near_tiles = win_tiles[-2:]
    n_far = max(qb - 1, 0)
    far_groups = [(t, min(2, n_far - t)) for t in range(0, n_far, 2)]

    def split3(x):
        hi = x.astype(BF16)
        r1 = x - hi.astype(F32)
        mid = r1.astype(BF16)
        lo = (r1 - mid.astype(F32)).astype(BF16)
        return jnp.concatenate([hi, mid, lo], axis=1)

    def ones_v(tile):
        return jnp.concatenate([tile, jnp.ones_like(tile)], axis=1)

    def weighted(o, hh, br):
        r = slice(hh * TQ, (hh + 1) * TQ)
        l = o[r, LANES:2 * LANES]
        gate = gate_ref[0, 0, :, 3 * hh + br:3 * hh + br + 1]
        return o[r, 0:LANES] * (gate / jnp.where(l > 0.0, l, 1.0))

    tiles_w = [key_rows(kt, LANES) for kt, _ in win_tiles]
    s_w = []

    def win_scores():
        i = len(s_w)
        if i < len(win_tiles):
            s_w.append((_split_rows(_dot_nt, q4, tiles_w[i]) + b3_ref[0, win_tiles[i][1]]) * LOG2E)

    kvc = kvc_ref[0, 0]
    bias_c = bc_ref[0, 0]
    s_c = _split_rows(_dot_nt, q4, kvc) + bias_c
    yield
    win_scores()
    yield
    m_c = jnp.max(s_c, axis=1, keepdims=True)
    e_c = jnp.where(bias_c > 0.5 * NEG, jnp.exp(s_c - m_c), 0.0)
    o_c = _split_rows(_dot, e_c.astype(BF16), ones_v(kvc))
    yield
    win_scores()
    yield
    l_c = o_c[:, LANES:2 * LANES]
    p_c = e_c / jnp.where(l_c > 0.0, l_c, 1.0)
    psum = p_c[0:TQ]
    for hh in range(1, HPG):
        psum = psum + p_c[hh * TQ:(hh + 1) * TQ]

    imp_t = _dot_nt(ov_ref[...], split3(psum))
    yield
    win_scores()
    yield
    s_w = jnp.concatenate(s_w, axis=1)
    m_w = jnp.max(tile_max(s_w), axis=1, keepdims=True)
    p_w = exp2_rel(s_w, jnp.broadcast_to(m_w, (rows, LANES)))
    o_w = _split_rows(_dot, p_w, ones_v(jnp.concatenate(tiles_w, axis=0)))
    yield

    jrow = lax.broadcasted_iota(jnp.int32, (n_slc, TQ), 0)
    tpos = qb * TQ + lax.broadcasted_iota(jnp.int32, (n_slc, TQ), 1)
    cur = lax.shift_right_logical(tpos, int(math.log2(SLC_LEN)))
    forced = (jrow == 0) | (jrow == cur) | (jrow == cur - 1)
    score = jnp.where(forced, 3e38, jnp.where(jrow <= cur, imp_t, -1.0))
    rank = jnp.zeros((n_slc, TQ), F32)
    for k in range(n_slc):
        sk = score[k:k + 1, :]
        beats = (sk > score) | ((sk == score) & (jrow > k))
        rank = rank + beats.astype(F32)
    notsel_t = (rank >= float(N_SEL)).astype(BF16)
    q_mask = lax.dot_general(notsel_t, place_ref[...], (((0,), (0,)), ((), ())),
                             preferred_element_type=F32).astype(BF16)
    yield
    for hh in range(HPG):
        q4_ref[hh * TQ:(hh + 1) * TQ, LANES:2 * LANES] = q_mask

    for hh in range(HPG):
        comb_ref[hh * TQ:(hh + 1) * TQ, :] = weighted(o_c, hh, 0) + weighted(o_w, hh, 2)

    kt_n, n_near = near_tiles[0][0], len(near_tiles)
    ks_n = slice(kt_n * TQ, (kt_n + n_near) * TQ)
    bias_n = jnp.concatenate([b3_ref[0, kind] for _, kind in near_tiles], axis=1)
    rhs_n = jnp.concatenate([key_rows(kt_n, 0, n_near), ceneg_ref[ks_n, :]], axis=1)
    dot_n = _dot_nt if n_near == 2 else functools.partial(_split_rows, _dot_nt)
    s_n = (dot_n(q4_ref[...], rhs_n) + bias_n) * LOG2E
    yield
    nsc_ref[...] = s_n
    macc_ref[len(far_groups)] = tile_max(s_n)

    for i, (kt, n) in enumerate(far_groups):
        ks = slice(kt * TQ, (kt + n) * TQ)
        rhs = jnp.concatenate([key_rows(kt, 0, n), ceneg_ref[ks, :]], axis=1)
        dot_f = _dot_nt if n == 2 else functools.partial(_split_rows, _dot_nt)
        sv = dot_f(q4_ref[...], rhs) * LOG2E
        yield
        sc_ref[i, :, 0:n * TQ] = sv
        macc_ref[i] = tile_max(sv)

    m_el = macc_ref[0]
    for i in range(1, len(far_groups) + 1):
        m_el = jnp.maximum(m_el, macc_ref[i])
    m_s = jnp.max(m_el, axis=1, keepdims=True)
    mb_ref[...] = jnp.broadcast_to(m_s, (rows, LANES))
    oacc_ref[...] = jnp.zeros((rows, 2 * LANES), F32)

    for i, (kt, n) in enumerate(far_groups):
        rhs = ones_v(key_rows(kt, 0, n))
        for r in (slice(0, rows // 2), slice(rows // 2, rows)):
            oacc_ref[r, :] += _dot(exp2_rel(sc_ref[i, r, 0:n * TQ], mb_ref[r, :]), rhs)
            yield

    p_n = exp2_rel(nsc_ref[...], mb_ref[...])
    o_s = oacc_ref[...] + _split_rows(_dot, p_n, ones_v(key_rows(kt_n, 0, n_near)))
    yield

    for hp in range(HPG // 2):
        even, odd = [comb_ref[hh * TQ:(hh + 1) * TQ, :] + weighted(o_s, hh, 1)
                     for hh in (2 * hp, 2 * hp + 1)]
        o_ref[0, :, hp * LANES:(hp + 1) * LANES] = (
            jnp.where(low, pltpu.roll(even, HEAD_DIM, 1), odd).astype(BF16))


def _attention(q, gates, kvr, kvcmp, rel_bias):
    b, s, hd = q.shape
    nq = s // TQ
    n_slc = s // SLC_LEN
    n_cmp = s // CMP_STRIDE
    rows = HPG * TQ
    b3, bc = _attn_tables(rel_bias, s)
    ceneg, ov, place = _attn_consts(s)
    gw = 3 * HPG
    gates_g = (gates[:, :, :3 * N_HEADS].reshape(b, s, N_GROUPS, gw).transpose(0, 2, 1, 3))
    gl = HPG * HEAD_DIM
    out = jnp.zeros((b, s, N_HEADS * HEAD_DIM), BF16)
    for qb in range(nq):
        n_groups = (max(qb - 1, 0) + 1) // 2
        n_near = min(qb + 1, 2)
        nb = BATCH_PER_STEP
        in_specs = [
            pl.BlockSpec((nb, TQ, gl), lambda g, i, qb=qb: (i, qb, g)),
            pl.BlockSpec((nb, 1, TQ, gw), lambda g, i, qb=qb: (i, g, qb, 0)),
            pl.BlockSpec((nb, s, 4 * HEAD_DIM), lambda g, i: (i, 0, g)),
            pl.BlockSpec((nb, 1, n_cmp, LANES), lambda g, i: (i, g, 0, 0)),
            pl.BlockSpec((1, N_KINDS, rows, TQ), lambda g, i: (g, 0, 0, 0)),
            pl.BlockSpec((1, 1, rows, n_cmp), lambda g, i, qb=qb: (qb, g, 0, 0)),
            pl.BlockSpec((s, LANES), lambda *_: (0, 0)),
            pl.BlockSpec((n_slc, 3 * n_cmp), lambda *_: (0, 0)),
            pl.BlockSpec((n_slc, LANES), lambda *_: (0, 0)),
            pl.BlockSpec(memory_space=pl.ANY),
        ]
        args = [q, gates_g, kvr, kvcmp, b3, bc, ceneg, ov, place, out]
        out = pl.pallas_call(
            functools.partial(_attn_kernel, qb=qb),
            grid=(N_GROUPS, b // nb),
            in_specs=in_specs,
            out_specs=pl.BlockSpec((nb, TQ, gl), lambda g, i, qb=qb: (i, qb, g)),
            out_shape=jax.ShapeDtypeStruct((b, s, N_HEADS * HEAD_DIM), BF16),
            scratch_shapes=[
                pltpu.VMEM((nb, rows, 2 * LANES), BF16),
                pltpu.VMEM((nb, max(n_groups, 1), rows, 2 * TQ), F32),
                pltpu.VMEM((nb, rows, n_near * TQ), F32),
                pltpu.VMEM((nb, n_groups + 1, rows, LANES), F32),
                pltpu.VMEM((nb, rows, LANES), F32),
                pltpu.VMEM((nb, rows, 2 * LANES), F32),
                pltpu.VMEM((nb, rows, LANES), F32),
            ],
            input_output_aliases={len(args) - 1: 0},
            compiler_params=pltpu.CompilerParams(
                dimension_semantics=("arbitrary", "arbitrary"), vmem_limit_bytes=VMEM_LIMIT),
            name=f"nsa_attn{qb}",
        )(*args)
    return out


def kernel(x, mix_norm, a_w_in, a_conv, a_w_out, ffn_norm, ffn_up, ffn_conv, ffn_down,
           kv_norm, w_kv, cmp_pe, cmp_w1, cmp_b1, cmp_w2, b_w_qg, b_w_o, rel_bias, final_norm):
    b, s, d = x.shape
    assert d == D_MODEL and s % 512 == 0 and mix_norm.shape[0] == 2 and b % BATCH_PER_STEP == 0
    x = _mixer(x, mix_norm[0], a_w_in[0], a_conv[0], a_w_out[0])
    x = _ffn(x, ffn_norm[0], ffn_up[0], ffn_conv[0], ffn_down[0])
    q, gates, kvc, kvr = _proj(x, mix_norm[1], kv_norm, b_w_qg[0], w_kv)
    kvcmp = _compress(kvc, cmp_pe, cmp_w1, cmp_b1, cmp_w2)
    attn = _attention(q, gates, kvr, kvcmp, rel_bias)
    return _ffn(x, ffn_norm[1], ffn_up[1], ffn_conv[1], ffn_down[1],
                attn=attn, w_o=b_w_o[0].astype(BF16), final_g=final_norm)
```

```python
import functools
import math

import jax
import jax.numpy as jnp
import numpy as np
from jax import lax
from jax.experimental import pallas as pl
from jax.experimental.pallas import tpu as pltpu

D_MODEL = 1024
CONV_W = 3
D_FF = 2816
N_HEADS = 16
N_GROUPS = 4
HPG = N_HEADS // N_GROUPS
HEAD_DIM = 64
CMP_LEN = 32
CMP_STRIDE = 16
CMP_HIDDEN = 128
SLC_LEN = 64
N_SEL = 16
WINDOW = 512
N_BUCKETS = 32
MAX_DISTANCE = 128
EPS = 1e-6
NEG = -1e30

LANES = 128
MXU_COLS = 256
V7X_VMEM_BYTES = 64 * 2**20
VMEM_LIMIT = V7X_VMEM_BYTES - 8 * 2**20

TQ = 256
CARRY_ROWS = 8
F32 = jnp.float32
BF16 = jnp.bfloat16


def _dot(a, b):
    return jnp.dot(a, b, preferred_element_type=F32)


def _dot_nt(a, b, precision=None):
    return lax.dot_general(a, b, (((1,), (1,)), ((), ())),
                           preferred_element_type=F32, precision=precision)


def _split_rows(dot, a, b):
    h = a.shape[0] // 2
    return jnp.concatenate([dot(a[:h], b), dot(a[h:], b)], axis=0)


def _rms_scale(x):
    return x * lax.rsqrt(jnp.mean(x * x, axis=-1, keepdims=True) + EPS)


def _conv3(buf_ref, cw, rows):
    c = CARRY_ROWS
    return (cw[0:1, :] * buf_ref[c - 2:c - 2 + rows, :]
            + cw[1:2, :] * buf_ref[c - 1:c - 1 + rows, :]
            + cw[2:3, :] * buf_ref[c:c + rows, :])


def _mixer_kernel(x_ref, g_ref, win_ref, cw_ref, wout_ref, o_ref, ubuf_ref, *, tm):
    d = D_MODEL

    @pl.when(pl.program_id(1) == 0)
    def _():
        ubuf_ref[0:CARRY_ROWS, :] = jnp.zeros((CARRY_ROWS, d), F32)

    x = x_ref[0]
    h = (_rms_scale(x) * g_ref[...]).astype(BF16)
    cg = _dot(h, win_ref[:, d:2 * d])
    v = _dot(h, win_ref[:, 2 * d:3 * d])
    ubuf_ref[CARRY_ROWS:CARRY_ROWS + tm, :] = cg * v
    conv = _conv3(ubuf_ref, cw_ref[...], tm)
    ubuf_ref[0:CARRY_ROWS, :] = ubuf_ref[tm:tm + CARRY_ROWS, :]
    bg = _dot(h, win_ref[:, 0:d])
    y = (bg * conv).astype(BF16)
    o_ref[0] = x + _dot(y, wout_ref[...])


def _mixer(x, g, w_in, conv_w, w_out, *, tm=512):
    b, s, d = x.shape
    const = lambda *_: (0, 0)
    one = pl.Buffered(1)
    return pl.pallas_call(
        functools.partial(_mixer_kernel, tm=tm),
        grid=(b, s // tm),
        in_specs=[
            pl.BlockSpec((1, tm, d), lambda i, j: (i, j, 0)),
            pl.BlockSpec((1, d), const),
            pl.BlockSpec((d, 3 * d), const, pipeline_mode=one),
            pl.BlockSpec((CONV_W, d), const),
            pl.BlockSpec((d, d), const, pipeline_mode=one),
        ],
        out_specs=pl.BlockSpec((1, tm, d), lambda i, j: (i, j, 0)),
        out_shape=jax.ShapeDtypeStruct((b, s, d), F32),
        scratch_shapes=[pltpu.VMEM((CARRY_ROWS + tm, d), F32)],
        compiler_params=pltpu.CompilerParams(
            dimension_semantics=("arbitrary", "arbitrary"), vmem_limit_bytes=VMEM_LIMIT),
        name="mixer",
    )(x, g.reshape(1, d), w_in.astype(BF16), conv_w, w_out.astype(BF16))


def _ffn_chunks():
    tiles = D_FF // MXU_COLS
    assert tiles * MXU_COLS == D_FF
    first = (tiles + 1) // 2 * MXU_COLS
    return [(0, first), (first, D_FF)]


def _ffn_kernel(*refs, tm, has_attn, final_norm):
    refs = list(refs)
    x_ref = refs.pop(0)
    a_ref = refs.pop(0) if has_attn else None
    wo_ref = refs.pop(0) if has_attn else None
    g_ref, wup_ref, cw_ref, wdn_ref = refs[:4]
    refs = refs[4:]
    gf_ref = refs.pop(0) if final_norm else None
    chunks = _ffn_chunks()
    o_ref, h_ref, carry_ref, acc_ref = refs[:4]
    u_refs = refs[4:]
    assert len(u_refs) == 2 * len(chunks)

    @pl.when(pl.program_id(1) == 0)
    def _():
        carry_ref[...] = jnp.zeros(carry_ref.shape, F32)

    x = x_ref[0]
    if has_attn:
        x = x + _dot(a_ref[0], wo_ref[...])
    h_ref[...] = (_rms_scale(x) * g_ref[...]).astype(BF16)

    def up(buf_ref, cols):
        w = cols.stop - cols.start
        buf_ref[0:CARRY_ROWS, 0:w] = carry_ref[:, cols]
        buf_ref[CARRY_ROWS:CARRY_ROWS + tm, 0:w] = _dot(h_ref[...], wup_ref[:, cols])
        carry_ref[:, cols] = buf_ref[tm:tm + CARRY_ROWS, 0:w]

    def conv(buf_ref, cols):
        w = cols.stop - cols.start
        c = CARRY_ROWS
        cw = cw_ref[:, cols]
        return (cw[0:1, :] * buf_ref[c - 2:c - 2 + tm, 0:w] + cw[1:2, :] * buf_ref[c - 1:c - 1 + tm, 0:w]
                + cw[2:3, :] * buf_ref[c:c + tm, 0:w])

    for c, (lo, hi) in enumerate(chunks):
        up(u_refs[2 * c], slice(lo, hi))
        up(u_refs[2 * c + 1], slice(D_FF + lo, D_FF + hi))

    for c, (lo, hi) in enumerate(chunks):
        a = conv(u_refs[2 * c], slice(lo, hi))
        gt = conv(u_refs[2 * c + 1], slice(D_FF + lo, D_FF + hi))
        act = (a * jax.nn.sigmoid(a) * gt).astype(BF16)
        part = _dot(act, wdn_ref[lo:hi, :])
        if c == 0:
            acc_ref[...] = x + part
        else:
            acc_ref[...] += part
    y = acc_ref[...]
    if final_norm:
        y = _rms_scale(y) * gf_ref[...]
    o_ref[0] = y


def _ffn(x, g, w_up, conv_w, w_down, *, attn=None, w_o=None, final_g=None, tm=512):
    b, s, d = x.shape
    has_attn = attn is not None
    final_norm = final_g is not None
    const = lambda *_: (0, 0)
    row = lambda i, j: (i, j, 0)
    one = pl.Buffered(1)
    args = [x]
    in_specs = [pl.BlockSpec((1, tm, d), row)]
    if has_attn:
        ka = attn.shape[-1]
        args += [attn, w_o]
        in_specs += [pl.BlockSpec((1, tm, ka), row),
                     pl.BlockSpec((ka, d), const, pipeline_mode=one)]
    args += [g.reshape(1, d), w_up.astype(BF16), conv_w, w_down.astype(BF16)]
    in_specs += [pl.BlockSpec((1, d), const),
                 pl.BlockSpec((d, 2 * D_FF), const, pipeline_mode=one),
                 pl.BlockSpec((CONV_W, 2 * D_FF), const),
                 pl.BlockSpec((D_FF, d), const, pipeline_mode=one)]
    if final_norm:
        args.append(final_g.reshape(1, d))
        in_specs.append(pl.BlockSpec((1, d), const))
    return pl.pallas_call(
        functools.partial(_ffn_kernel, tm=tm, has_attn=has_attn, final_norm=final_norm),
        grid=(b, s // tm),
        in_specs=in_specs,
        out_specs=pl.BlockSpec((1, tm, d), row),
        out_shape=jax.ShapeDtypeStruct((b, s, d), F32),
        scratch_shapes=[
            pltpu.VMEM((tm, d), BF16),
            pltpu.VMEM((CARRY_ROWS, 2 * D_FF), F32),
            pltpu.VMEM((tm, d), F32),
        ] + [pltpu.VMEM((CARRY_ROWS + tm, hi - lo), F32) for lo, hi in _ffn_chunks() for _ in range(2)],
        compiler_params=pltpu.CompilerParams(
            dimension_semantics=("arbitrary", "arbitrary"), vmem_limit_bytes=VMEM_LIMIT),
        name="ffn_attn" if has_attn else "ffn",
    )(*args)


def _proj_kernel(x_ref, gm_ref, gk_ref, wq_ref, wg_ref, wkc_ref, wkr_ref,
                 q_ref, gate_ref, kvc_ref, kvr_ref):
    xn = _rms_scale(x_ref[0])
    hq = (xn * gm_ref[...]).astype(BF16)
    hs = (xn * gk_ref[...]).astype(BF16)
    q_ref[0] = (_dot(hq, wq_ref[...]) * (HEAD_DIM ** -0.5)).astype(BF16)
    gates = jax.nn.sigmoid(_dot(hq, wg_ref[...]))
    gate_ref[0] = gates
    kvc = _dot(hs, wkc_ref[...])
    for sl in range(2 * N_GROUPS * HEAD_DIM // LANES):
        kvc_ref[0, sl] = kvc[:, sl * LANES:(sl + 1) * LANES]
    kvr_ref[0] = _dot(hs, wkr_ref[...]).astype(BF16)


def _proj(x, g_mix, g_kv, w_qg, w_kv, *, tm=512):
    b, s, d = x.shape
    hd = N_HEADS * HEAD_DIM
    gd = N_GROUPS * HEAD_DIM
    w_q = w_qg[:, :hd].astype(BF16)
    w_g = jnp.pad(w_qg[:, hd:], ((0, 0), (0, LANES - 3 * N_HEADS))).astype(BF16)
    w_kc = w_kv[:, :2 * gd].astype(BF16)
    w_kr = (w_kv[:, 2 * gd:].reshape(d, 4, N_GROUPS, HEAD_DIM)
            .transpose(0, 2, 1, 3).reshape(d, 4 * gd).astype(BF16))
    const = lambda *_: (0, 0)
    row = lambda i, j: (i, j, 0)
    one = pl.Buffered(1)
    return pl.pallas_call(
        _proj_kernel,
        grid=(b, s // tm),
        in_specs=[
            pl.BlockSpec((1, tm, d), row),
            pl.BlockSpec((1, d), const),
            pl.BlockSpec((1, d), const),
            pl.BlockSpec((d, hd), const, pipeline_mode=one),
            pl.BlockSpec((d, LANES), const, pipeline_mode=one),
            pl.BlockSpec((d, 2 * gd), const, pipeline_mode=one),
            pl.BlockSpec((d, 4 * gd), const, pipeline_mode=one),
        ],
        out_specs=[
            pl.BlockSpec((1, tm, hd), row),
            pl.BlockSpec((1, tm, LANES), row),
            pl.BlockSpec((1, 2 * gd // LANES, tm, LANES), lambda i, j: (i, 0, j, 0)),
            pl.BlockSpec((1, tm, 4 * gd), row),
        ],
        out_shape=[
            jax.ShapeDtypeStruct((b, s, hd), BF16),
            jax.ShapeDtypeStruct((b, s, LANES), F32),
            jax.ShapeDtypeStruct((b, 2 * gd // LANES, s, LANES), F32),
            jax.ShapeDtypeStruct((b, s, 4 * gd), BF16),
        ],
        compiler_params=pltpu.CompilerParams(
            dimension_semantics=("arbitrary", "arbitrary"), vmem_limit_bytes=VMEM_LIMIT),
        name="proj",
    )(x, g_mix.reshape(1, d), g_kv.reshape(1, d), w_q, w_g, w_kc, w_kr)


def _gelu_tanh(x):
    return 0.5 * x * (1.0 + jnp.tanh(math.sqrt(2.0 / math.pi) * (x + 0.044715 * (x * x * x))))


def _compress_kernel(k_ref, v_ref, pe_ref, w1_ref, b1_ref, w2_ref, o_ref):
    n_chunks = int(o_ref.shape[2])
    outs = [None, None]
    for j, src in enumerate((k_ref, v_ref)):
        first = None
        second = None
        for l in range(CMP_STRIDE):
            x = src.at[0, 0][pl.ds(l, n_chunks, stride=CMP_STRIDE), :]
            pa = _dot((x + pe_ref[j, l:l + 1, :]).astype(BF16), w1_ref[j, l])
            pb = _dot((x + pe_ref[j, CMP_STRIDE + l:CMP_STRIDE + l + 1, :]).astype(BF16),
                      w1_ref[j, CMP_STRIDE + l])
            first = pa if first is None else first + pa
            second = pb if second is None else second + pb
        pre = first + pltpu.roll(second, n_chunks - 1, 0) + b1_ref[j]
        hid = _gelu_tanh(pre).astype(BF16)
        for e in range(2):
            part = _dot(hid[:, e * CMP_HIDDEN:(e + 1) * CMP_HIDDEN], w2_ref[j])
            outs[e] = part if outs[e] is None else outs[e] + part
    for e in range(2):
        o_ref[0, e] = outs[e].astype(BF16)


def _compress(kvc, cmp_pe, cmp_w1, cmp_b1, cmp_w2):
    b, n_slabs, s, _ = kvc.shape
    n_chunks = s // CMP_STRIDE
    pairs = N_GROUPS // 2
    assert n_slabs == 2 * pairs and 2 * HEAD_DIM == LANES
    pe = jnp.concatenate([cmp_pe, cmp_pe], axis=-1)
    w1 = cmp_w1.reshape(2, CMP_LEN, HEAD_DIM, CMP_HIDDEN)
    zero = jnp.zeros_like(w1)
    w1 = jnp.concatenate([jnp.concatenate([w1, zero], axis=-1),
                          jnp.concatenate([zero, w1], axis=-1)], axis=2).astype(BF16)
    b1 = jnp.concatenate([cmp_b1, cmp_b1], axis=-1).reshape(2, 1, 2 * CMP_HIDDEN)
    w2 = jnp.stack([jnp.pad(cmp_w2[0], ((0, 0), (0, HEAD_DIM))),
                    jnp.pad(cmp_w2[1], ((0, 0), (HEAD_DIM, 0)))]).astype(BF16)
    const = lambda *_: (0, 0, 0)
    return pl.pallas_call(
        _compress_kernel,
        grid=(b, pairs),
        in_specs=[
            pl.BlockSpec((1, 1, s, LANES), lambda i, p: (i, p, 0, 0)),
            pl.BlockSpec((1, 1, s, LANES), lambda i, p: (i, pairs + p, 0, 0)),
            pl.BlockSpec((2, CMP_LEN, LANES), const),
            pl.BlockSpec((2, CMP_LEN, LANES, 2 * CMP_HIDDEN), lambda *_: (0, 0, 0, 0)),
            pl.BlockSpec((2, 1, 2 * CMP_HIDDEN), const),
            pl.BlockSpec((2, CMP_HIDDEN, LANES), const),
        ],
        out_specs=pl.BlockSpec((1, 2, n_chunks, LANES), lambda i, p: (i, p, 0, 0)),
        out_shape=jax.ShapeDtypeStruct((b, N_GROUPS, n_chunks, LANES), BF16),
        compiler_params=pltpu.CompilerParams(
            dimension_semantics=("arbitrary", "arbitrary"), vmem_limit_bytes=VMEM_LIMIT),
        name="compress",
    )(kvc, kvc, pe, w1, b1, w2)


TILE_PREV2, TILE_PREV, TILE_DIAG = 0, 1, 2
N_KINDS = 3
CMP_WIN = 32
LOG2E = 1.0 / math.log(2.0)


def _rel_bucket_np(dist):
    max_exact = N_BUCKETS // 2
    d = np.maximum(dist, 0)
    df = np.maximum(d, 1).astype(np.float32)
    large = max_exact + (np.log(df / max_exact) / np.float32(math.log(MAX_DISTANCE / max_exact))
                         * (N_BUCKETS - max_exact)).astype(np.int32)
    return np.where(d < max_exact, d, np.minimum(large, N_BUCKETS - 1))


def _toeplitz(w):
    t = w.shape[-1] // 2
    lead = w.shape[:-1]
    a = jnp.broadcast_to(w[..., None, :], lead + (t, 2 * t)).reshape(lead + (2 * t * t,))
    return a[..., :t * (2 * t - 1)].reshape(lead + (t, 2 * t - 1))[..., :t]


def _attn_tables(rel_bias, s):
    nq = s // TQ
    assert _rel_bucket_np(np.arange(TQ // 2, 4 * TQ)).min() == N_BUCKETS - 1
    relb = (rel_bias.astype(F32) - rel_bias[N_BUCKETS - 1].astype(F32)).T
    f = jnp.take(relb, jnp.asarray(_rel_bucket_np(np.arange(2 * TQ))), axis=1)
    neg = jnp.full((N_HEADS, TQ), NEG, F32)
    w_diag = jnp.concatenate([f[:, 0:1], neg, jnp.flip(f[:, 1:TQ], axis=1)], axis=1)
    w_prev = jnp.concatenate([jnp.flip(f[:, 1:TQ + 1], axis=1), jnp.zeros((N_HEADS, 1), F32),
                              jnp.flip(f[:, TQ + 1:2 * TQ], axis=1)], axis=1)
    ti = np.arange(TQ)[:, None]
    ki = np.arange(TQ)[None, :]
    prev2 = jnp.broadcast_to(jnp.asarray(np.where(ki > ti, 0.0, NEG), F32), (N_HEADS, TQ, TQ))
    b3 = jnp.stack([prev2, _toeplitz(w_prev), _toeplitz(w_diag)], axis=1)
    b3 = (b3.reshape(N_GROUPS, HPG, N_KINDS, TQ, TQ).transpose(0, 2, 1, 3, 4)
          .reshape(N_GROUPS, N_KINDS, HPG * TQ, TQ))
    n_cmp = s // CMP_STRIDE
    lead = CMP_WIN // 4
    dist = ti - CMP_STRIDE * (np.arange(CMP_WIN)[None, :] - lead) - (CMP_LEN - 1)
    pw = jnp.take(relb, jnp.asarray(_rel_bucket_np(dist)), axis=1)
    pw = jnp.where(jnp.asarray(dist >= 0), pw, NEG)
    full = jnp.concatenate([jnp.zeros((N_HEADS, TQ, n_cmp - lead), F32), pw,
                            jnp.full((N_HEADS, TQ, n_cmp), NEG, F32)], axis=-1)
    per_tile = TQ // CMP_STRIDE
    bc = jnp.stack([full[..., n_cmp - per_tile * t:2 * n_cmp - per_tile * t] for t in range(nq)])
    bc = bc.reshape(nq, N_GROUPS, HPG * TQ, n_cmp)
    return b3, bc


def _attn_consts(s):
    n_slc = s // SLC_LEN
    n_cmp = s // CMP_STRIDE
    key = np.arange(s)[:, None]
    lane = np.arange(LANES)[None, :]
    ceneg = np.where((key // SLC_LEN == lane) & (lane < n_slc), NEG, 0.0)
    j = np.arange(n_slc)[:, None]
    i = np.arange(n_cmp)[None, :]
    ov = ((i * CMP_STRIDE < j * SLC_LEN + SLC_LEN) & (i * CMP_STRIDE + CMP_LEN > j * SLC_LEN)
          & (i < n_cmp - 1))
    place = np.eye(n_slc, LANES)
    return (jnp.asarray(ceneg, BF16), jnp.asarray(np.concatenate([ov] * 3, axis=1), BF16),
            jnp.asarray(place, BF16))


BATCH_PER_STEP = 2
STAGGER, DONE = "stagger", "done"


def _attn_kernel(q_ref, gate_ref, kv_ref, kvc_ref, b3_ref, bc_ref, ceneg_ref, ov_ref, place_ref,
                 prev_ref, o_ref, *scratch, qb):
    del prev_ref
    streams = [
        _attn_stream(q_ref.at[pl.ds(e, 1)], gate_ref.at[pl.ds(e, 1)], kv_ref.at[pl.ds(e, 1)],
                     kvc_ref.at[pl.ds(e, 1)], b3_ref, bc_ref, ceneg_ref, ov_ref, place_ref,
                     o_ref.at[pl.ds(e, 1)], *[s.at[e] for s in scratch], qb=qb)
        for e in range(BATCH_PER_STEP)]
    active = []
    for stream in streams:
        active.append(stream)
        while next(stream, STAGGER) is not STAGGER:
            for other in active[:-1]:
                next(other, None)
    while active:
        for stream in list(active):
            if next(stream, DONE) is DONE:
                active.remove(stream)


def _attn_stream(q_ref, gate_ref, kv_ref, kvc_ref, b3_ref, bc_ref, ceneg_ref, ov_ref, place_ref,
                 o_ref, q4_ref, sc_ref, nsc_ref, macc_ref, mb_ref, oacc_ref, comb_ref, *, qb):
    n_slc = ov_ref.shape[0]
    rows = HPG * TQ
    lane = lax.broadcasted_iota(jnp.int32, (TQ, LANES), 1)
    low = lane < HEAD_DIM

    for pair in range(HPG // 2):
        qp = q_ref[0, :, pair * LANES:(pair + 1) * LANES].astype(F32)
        q4_ref[(2 * pair) * TQ:(2 * pair + 1) * TQ, 0:LANES] = jnp.where(low, qp, 0.0).astype(BF16)
        q4_ref[(2 * pair + 1) * TQ:(2 * pair + 2) * TQ, 0:LANES] = (
            jnp.where(low, pltpu.roll(qp, HEAD_DIM, 1), 0.0).astype(BF16))
    q4 = q4_ref[:, 0:LANES]

    def key_rows(kt, col0, n=1):
        return kv_ref[0, kt * TQ:(kt + n) * TQ, col0:col0 + LANES]

    def lane_tiles(x):
        return [x[:, i:i + LANES] for i in range(0, x.shape[1], LANES)]

    def tile_max(x):
        return functools.reduce(jnp.maximum, lane_tiles(x))

    def exp2_rel(sv, mb):
        return jnp.concatenate([jnp.exp2(t - mb) for t in lane_tiles(sv)], axis=1).astype(BF16)

    win_tiles = [(kt, kind) for kt, kind in
                 ((qb - 2, TILE_PREV2), (qb - 1, TILE_PREV), (qb, TILE_DIAG)) if kt >= 0]
    near_tiles = win_tiles[-2:]
    n_far = max(qb - 1, 0)
    far_groups = [(t, min(2, n_far - t)) for t in range(0, n_far, 2)]

    def split3(x):
        hi = x.astype(BF16)
        r1 = x - hi.astype(F32)
        mid = r1.astype(BF16)
        lo = (r1 - mid.astype(F32)).astype(BF16)
        return jnp.concatenate([hi, mid, lo], axis=1)

    def ones_v(tile):
        return jnp.concatenate([tile, jnp.ones_like(tile)], axis=1)

    def weighted(o, hh, br):
        r = slice(hh * TQ, (hh + 1) * TQ)
        l = o[r, LANES:2 * LANES]
        if br == 0:
            l = jnp.where(l > 0.0, l, 1.0)
        gate = gate_ref[0, 0, :, 3 * hh + br:3 * hh + br + 1]
        return o[r, 0:LANES] * (gate / l)

    tiles_w = [key_rows(kt, LANES) for kt, _ in win_tiles]
    s_w = []

    def win_scores():
        i = len(s_w)
        if i < len(win_tiles):
            s_w.append((_split_rows(_dot_nt, q4, tiles_w[i]) + b3_ref[0, win_tiles[i][1]]) * LOG2E)

    kvc = kvc_ref[0, 0]
    bias_c = bc_ref[0, 0]
    s_c = _split_rows(_dot_nt, q4, kvc) + bias_c
    yield
    win_scores()
    yield
    m_c = jnp.maximum(jnp.max(s_c, axis=1, keepdims=True), 0.5 * NEG)
    e_c = jnp.exp(s_c - m_c)
    o_c = _split_rows(_dot, e_c.astype(BF16), ones_v(kvc))
    yield
    win_scores()
    yield
    l_c = o_c[:, LANES:2 * LANES]
    p_c = e_c / jnp.where(l_c > 0.0, l_c, 1.0)
    psum = p_c[0:TQ]
    for hh in range(1, HPG):
        psum = psum + p_c[hh * TQ:(hh + 1) * TQ]

    imp_t = _dot_nt(ov_ref[...], split3(psum))
    yield
    win_scores()
    yield
    s_w = jnp.concatenate(s_w, axis=1)
    m_w = jnp.max(tile_max(s_w), axis=1, keepdims=True)
    p_w = exp2_rel(s_w, jnp.broadcast_to(m_w, (rows, LANES)))
    o_w = _split_rows(_dot, p_w, ones_v(jnp.concatenate(tiles_w, axis=0)))
    yield

    jrow = lax.broadcasted_iota(jnp.int32, (n_slc, TQ), 0)
    tpos = qb * TQ + lax.broadcasted_iota(jnp.int32, (n_slc, TQ), 1)
    cur = lax.shift_right_logical(tpos, int(math.log2(SLC_LEN)))
    forced = (jrow == 0) | (jrow == cur) | (jrow == cur - 1)
    score = jnp.where(forced, 3e38, jnp.where(jrow <= cur, imp_t, -1.0))
    rank = jnp.zeros((n_slc, TQ), F32)
    for k in range(n_slc):
        sk = score[k:k + 1, :]
        beats = (sk > score) | ((sk == score) & (jrow > k))
        rank = rank + beats.astype(F32)
    notsel_t = (rank >= float(N_SEL)).astype(BF16)
    q_mask = lax.dot_general(notsel_t, place_ref[...], (((0,), (0,)), ((), ())),
                             preferred_element_type=F32).astype(BF16)
    yield
    for hh in range(HPG):
        q4_ref[hh * TQ:(hh + 1) * TQ, LANES:2 * LANES] = q_mask

    for hh in range(HPG):
        comb_ref[hh * TQ:(hh + 1) * TQ, :] = weighted(o_c, hh, 0) + weighted(o_w, hh, 2)

    kt_n, n_near = near_tiles[0][0], len(near_tiles)
    ks_n = slice(kt_n * TQ, (kt_n + n_near) * TQ)
    bias_n = jnp.concatenate([b3_ref[0, kind] for _, kind in near_tiles], axis=1)
    rhs_n = jnp.concatenate([key_rows(kt_n, 0, n_near), ceneg_ref[ks_n, :]], axis=1)
    dot_n = _dot_nt if n_near == 2 else functools.partial(_split_rows, _dot_nt)
    s_n = (dot_n(q4_ref[...], rhs_n) + bias_n) * LOG2E
    yield
    nsc_ref[...] = s_n
    macc_ref[len(far_groups)] = tile_max(s_n)

    for i, (kt, n) in enumerate(far_groups):
        ks = slice(kt * TQ, (kt + n) * TQ)
        rhs = jnp.concatenate([key_rows(kt, 0, n), ceneg_ref[ks, :]], axis=1)
        dot_f = _dot_nt if n == 2 else functools.partial(_split_rows, _dot_nt)
        sv = dot_f(q4_ref[...], rhs) * LOG2E
        yield
        sc_ref[i, :, 0:n * TQ] = sv
        macc_ref[i] = tile_max(sv)

    yield STAGGER
    m_el = macc_ref[0]
    for i in range(1, len(far_groups) + 1):
        m_el = jnp.maximum(m_el, macc_ref[i])
    m_s = jnp.max(m_el, axis=1, keepdims=True)
    mb_ref[...] = jnp.broadcast_to(m_s, (rows, LANES))
    oacc_ref[...] = jnp.zeros((rows, 2 * LANES), F32)

    for i, (kt, n) in enumerate(far_groups):
        rhs = ones_v(key_rows(kt, 0, n))
        for r in (slice(0, rows // 2), slice(rows // 2, rows)):
            oacc_ref[r, :] += _dot(exp2_rel(sc_ref[i, r, 0:n * TQ], mb_ref[r, :]), rhs)
            yield

    p_n = exp2_rel(nsc_ref[...], mb_ref[...])
    o_s = oacc_ref[...] + _split_rows(_dot, p_n, ones_v(key_rows(kt_n, 0, n_near)))
    yield

    for hp in range(HPG // 2):
        even, odd = [comb_ref[hh * TQ:(hh + 1) * TQ, :] + weighted(o_s, hh, 1)
                     for hh in (2 * hp, 2 * hp + 1)]
        o_ref[0, :, hp * LANES:(hp + 1) * LANES] = (
            jnp.where(low, pltpu.roll(even, HEAD_DIM, 1), odd).astype(BF16))


def _attention(q, gates, kvr, kvcmp, rel_bias):
    b, s, hd = q.shape
    nq = s // TQ
    n_slc = s // SLC_LEN
    n_cmp = s // CMP_STRIDE
    rows = HPG * TQ
    b3, bc = _attn_tables(rel_bias, s)
    ceneg, ov, place = _attn_consts(s)
    gw = 3 * HPG
    gates_g = (gates[:, :, :3 * N_HEADS].reshape(b, s, N_GROUPS, gw).transpose(0, 2, 1, 3))
    gl = HPG * HEAD_DIM
    out = jnp.zeros((b, s, N_HEADS * HEAD_DIM), BF16)
    for qb in range(nq):
        n_groups = (max(qb - 1, 0) + 1) // 2
        n_near = min(qb + 1, 2)
        nb = BATCH_PER_STEP
        in_specs = [
            pl.BlockSpec((nb, TQ, gl), lambda g, i, qb=qb: (i, qb, g)),
            pl.BlockSpec((nb, 1, TQ, gw), lambda g, i, qb=qb: (i, g, qb, 0)),
            pl.BlockSpec((nb, s, 4 * HEAD_DIM), lambda g, i: (i, 0, g)),
            pl.BlockSpec((nb, 1, n_cmp, LANES), lambda g, i: (i, g, 0, 0)),
            pl.BlockSpec((1, N_KINDS, rows, TQ), lambda g, i: (g, 0, 0, 0)),
            pl.BlockSpec((1, 1, rows, n_cmp), lambda g, i, qb=qb: (qb, g, 0, 0)),
            pl.BlockSpec((s, LANES), lambda *_: (0, 0)),
            pl.BlockSpec((n_slc, 3 * n_cmp), lambda *_: (0, 0)),
            pl.BlockSpec((n_slc, LANES), lambda *_: (0, 0)),
            pl.BlockSpec(memory_space=pl.ANY),
        ]
        args = [q, gates_g, kvr, kvcmp, b3, bc, ceneg, ov, place, out]
        out = pl.pallas_call(
            functools.partial(_attn_kernel, qb=qb),
            grid=(N_GROUPS, b // nb),
            in_specs=in_specs,
            out_specs=pl.BlockSpec((nb, TQ, gl), lambda g, i, qb=qb: (i, qb, g)),
            out_shape=jax.ShapeDtypeStruct((b, s, N_HEADS * HEAD_DIM), BF16),
            scratch_shapes=[
                pltpu.VMEM((nb, rows, 2 * LANES), BF16),
                pltpu.VMEM((nb, max(n_groups, 1), rows, 2 * TQ), F32),
                pltpu.VMEM((nb, rows, n_near * TQ), F32),
                pltpu.VMEM((nb, n_groups + 1, rows, LANES), F32),
                pltpu.VMEM((nb, rows, LANES), F32),
                pltpu.VMEM((nb, rows, 2 * LANES), F32),
                pltpu.VMEM((nb, rows, LANES), F32),
            ],
            input_output_aliases={len(args) - 1: 0},
            compiler_params=pltpu.CompilerParams(
                dimension_semantics=("arbitrary", "arbitrary"), vmem_limit_bytes=VMEM_LIMIT),
            name=f"nsa_attn{qb}",
        )(*args)
    return out


def kernel(x, mix_norm, a_w_in, a_conv, a_w_out, ffn_norm, ffn_up, ffn_conv, ffn_down,
           kv_norm, w_kv, cmp_pe, cmp_w1, cmp_b1, cmp_w2, b_w_qg, b_w_o, rel_bias, final_norm):
    b, s, d = x.shape
    assert d == D_MODEL and s % 512 == 0 and mix_norm.shape[0] == 2 and b % BATCH_PER_STEP == 0
    x = _mixer(x, mix_norm[0], a_w_in[0], a_conv[0], a_w_out[0])
    x = _ffn(x, ffn_norm[0], ffn_up[0], ffn_conv[0], ffn_down[0])
    q, gates, kvc, kvr = _proj(x, mix_norm[1], kv_norm, b_w_qg[0], w_kv)
    kvcmp = _compress(kvc, cmp_pe, cmp_w1, cmp_b1, cmp_w2)
    attn = _attention(q, gates, kvr, kvcmp, rel_bias)
    return _ffn(x, ffn_norm[1], ffn_up[1], ffn_conv[1], ffn_down[1],
                attn=attn, w_o=b_w_o[0].astype(BF16), final_g=final_norm)
```

```python
import functools
import math

import jax
import jax.numpy as jnp
import numpy as np
from jax import lax
from jax.experimental import pallas as pl
from jax.experimental.pallas import tpu as pltpu

D_MODEL = 1024
CONV_W = 3
D_FF = 2816
N_HEADS = 16
N_GROUPS = 4
HPG = N_HEADS // N_GROUPS
HEAD_DIM = 64
CMP_LEN = 32
CMP_STRIDE = 16
CMP_HIDDEN = 128
SLC_LEN = 64
N_SEL = 16
WINDOW = 512
N_BUCKETS = 32
MAX_DISTANCE = 128
EPS = 1e-6
NEG = -1e30

LANES = 128
MXU_COLS = 256
V7X_VMEM_BYTES = 64 * 2**20
VMEM_LIMIT = V7X_VMEM_BYTES - 8 * 2**20

TQ = 256
CARRY_ROWS = 8
F32 = jnp.float32
BF16 = jnp.bfloat16


def _dot(a, b):
    return jnp.dot(a, b, preferred_element_type=F32)


def _dot_nt(a, b, precision=None):
    return lax.dot_general(a, b, (((1,), (1,)), ((), ())),
                           preferred_element_type=F32, precision=precision)


def _split_rows(dot, a, b):
    h = a.shape[0] // 2
    return jnp.concatenate([dot(a[:h], b), dot(a[h:], b)], axis=0)


def _rms_scale(x):
    return x * lax.rsqrt(jnp.mean(x * x, axis=-1, keepdims=True) + EPS)


def _conv3(buf_ref, cw, rows):
    c = CARRY_ROWS
    return (cw[0:1, :] * buf_ref[c - 2:c - 2 + rows, :]
            + cw[1:2, :] * buf_ref[c - 1:c - 1 + rows, :]
            + cw[2:3, :] * buf_ref[c:c + rows, :])


def _mixer_kernel(x_ref, g_ref, win_ref, cw_ref, wout_ref, o_ref, ubuf_ref, *, tm):
    d = D_MODEL

    @pl.when(pl.program_id(1) == 0)
    def _():
        ubuf_ref[0:CARRY_ROWS, :] = jnp.zeros((CARRY_ROWS, d), F32)

    x = x_ref[0]
    h = (_rms_scale(x) * g_ref[...]).astype(BF16)
    cg = _dot(h, win_ref[:, d:2 * d])
    v = _dot(h, win_ref[:, 2 * d:3 * d])
    ubuf_ref[CARRY_ROWS:CARRY_ROWS + tm, :] = cg * v
    conv = _conv3(ubuf_ref, cw_ref[...], tm)
    ubuf_ref[0:CARRY_ROWS, :] = ubuf_ref[tm:tm + CARRY_ROWS, :]
    bg = _dot(h, win_ref[:, 0:d])
    y = (bg * conv).astype(BF16)
    o_ref[0] = x + _dot(y, wout_ref[...])


def _mixer(x, g, w_in, conv_w, w_out, *, tm=1024):
    b, s, d = x.shape
    const = lambda *_: (0, 0)
    one = pl.Buffered(1)
    return pl.pallas_call(
        functools.partial(_mixer_kernel, tm=tm),
        grid=(b, s // tm),
        in_specs=[
            pl.BlockSpec((1, tm, d), lambda i, j: (i, j, 0)),
            pl.BlockSpec((1, d), const),
            pl.BlockSpec((d, 3 * d), const, pipeline_mode=one),
            pl.BlockSpec((CONV_W, d), const),
            pl.BlockSpec((d, d), const, pipeline_mode=one),
        ],
        out_specs=pl.BlockSpec((1, tm, d), lambda i, j: (i, j, 0)),
        out_shape=jax.ShapeDtypeStruct((b, s, d), F32),
        scratch_shapes=[pltpu.VMEM((CARRY_ROWS + tm, d), F32)],
        compiler_params=pltpu.CompilerParams(
            dimension_semantics=("arbitrary", "arbitrary"), vmem_limit_bytes=VMEM_LIMIT),
        name="mixer",
    )(x, g.reshape(1, d), w_in.astype(BF16), conv_w, w_out.astype(BF16))


def _ffn_chunks():
    tiles = D_FF // MXU_COLS
    assert tiles * MXU_COLS == D_FF
    first = (tiles + 1) // 2 * MXU_COLS
    return [(0, first), (first, D_FF)]


def _ffn_kernel(*refs, tm, has_attn, final_norm):
    refs = list(refs)
    x_ref = refs.pop(0)
    a_ref = refs.pop(0) if has_attn else None
    wo_ref = refs.pop(0) if has_attn else None
    g_ref, wup_ref, cw_ref, wdn_ref = refs[:4]
    refs = refs[4:]
    gf_ref = refs.pop(0) if final_norm else None
    chunks = _ffn_chunks()
    o_ref, h_ref, carry_ref, acc_ref = refs[:4]
    u_refs = refs[4:]
    assert len(u_refs) == 2 * len(chunks)

    @pl.when(pl.program_id(1) == 0)
    def _():
        carry_ref[...] = jnp.zeros(carry_ref.shape, F32)

    x = x_ref[0]
    if has_attn:
        x = x + _dot(a_ref[0], wo_ref[...])
    h_ref[...] = (_rms_scale(x) * g_ref[...]).astype(BF16)

    def up(buf_ref, cols):
        w = cols.stop - cols.start
        buf_ref[0:CARRY_ROWS, 0:w] = carry_ref[:, cols]
        buf_ref[CARRY_ROWS:CARRY_ROWS + tm, 0:w] = _dot(h_ref[...], wup_ref[:, cols])
        carry_ref[:, cols] = buf_ref[tm:tm + CARRY_ROWS, 0:w]

    def conv(buf_ref, cols):
        w = cols.stop - cols.start
        c = CARRY_ROWS
        cw = cw_ref[:, cols]
        return (cw[0:1, :] * buf_ref[c - 2:c - 2 + tm, 0:w] + cw[1:2, :] * buf_ref[c - 1:c - 1 + tm, 0:w]
                + cw[2:3, :] * buf_ref[c:c + tm, 0:w])

    for c, (lo, hi) in enumerate(chunks):
        up(u_refs[2 * c], slice(lo, hi))
        up(u_refs[2 * c + 1], slice(D_FF + lo, D_FF + hi))

    for c, (lo, hi) in enumerate(chunks):
        a = conv(u_refs[2 * c], slice(lo, hi))
        gt = conv(u_refs[2 * c + 1], slice(D_FF + lo, D_FF + hi))
        act = (a * jax.nn.sigmoid(a) * gt).astype(BF16)
        part = _dot(act, wdn_ref[lo:hi, :])
        if c == 0:
            acc_ref[...] = x + part
        else:
            acc_ref[...] += part
    y = acc_ref[...]
    if final_norm:
        y = _rms_scale(y) * gf_ref[...]
    o_ref[0] = y


def _ffn(x, g, w_up, conv_w, w_down, *, attn=None, w_o=None, final_g=None, tm=512):
    b, s, d = x.shape
    has_attn = attn is not None
    final_norm = final_g is not None
    const = lambda *_: (0, 0)
    row = lambda i, j: (i, j, 0)
    one = pl.Buffered(1)
    args = [x]
    in_specs = [pl.BlockSpec((1, tm, d), row)]
    if has_attn:
        ka = attn.shape[-1]
        args += [attn, w_o]
        in_specs += [pl.BlockSpec((1, tm, ka), row),
                     pl.BlockSpec((ka, d), const, pipeline_mode=one)]
    args += [g.reshape(1, d), w_up.astype(BF16), conv_w, w_down.astype(BF16)]
    in_specs += [pl.BlockSpec((1, d), const),
                 pl.BlockSpec((d, 2 * D_FF), const, pipeline_mode=one),
                 pl.BlockSpec((CONV_W, 2 * D_FF), const),
                 pl.BlockSpec((D_FF, d), const, pipeline_mode=one)]
    if final_norm:
        args.append(final_g.reshape(1, d))
        in_specs.append(pl.BlockSpec((1, d), const))
    return pl.pallas_call(
        functools.partial(_ffn_kernel, tm=tm, has_attn=has_attn, final_norm=final_norm),
        grid=(b, s // tm),
        in_specs=in_specs,
        out_specs=pl.BlockSpec((1, tm, d), row),
        out_shape=jax.ShapeDtypeStruct((b, s, d), F32),
        scratch_shapes=[
            pltpu.VMEM((tm, d), BF16),
            pltpu.VMEM((CARRY_ROWS, 2 * D_FF), F32),
            pltpu.VMEM((tm, d), F32),
        ] + [pltpu.VMEM((CARRY_ROWS + tm, hi - lo), F32) for lo, hi in _ffn_chunks() for _ in range(2)],
        compiler_params=pltpu.CompilerParams(
            dimension_semantics=("arbitrary", "arbitrary"), vmem_limit_bytes=VMEM_LIMIT),
        name="ffn_attn" if has_attn else "ffn",
    )(*args)


def _proj_kernel(x_ref, gm_ref, gk_ref, wq_ref, wg_ref, wkc_ref, wkr_ref,
                 q_ref, gate_ref, kvc_ref, kvr_ref):
    xn = _rms_scale(x_ref[0])
    hq = (xn * gm_ref[...]).astype(BF16)
    hs = (xn * gk_ref[...]).astype(BF16)
    q_ref[0] = (_dot(hq, wq_ref[...]) * (HEAD_DIM ** -0.5)).astype(BF16)
    gates = jax.nn.sigmoid(_dot(hq, wg_ref[...]))
    gate_ref[0] = gates
    kvc = _dot(hs, wkc_ref[...])
    for sl in range(2 * N_GROUPS * HEAD_DIM // LANES):
        kvc_ref[0, sl] = kvc[:, sl * LANES:(sl + 1) * LANES]
    kvr_ref[0] = _dot(hs, wkr_ref[...]).astype(BF16)


def _proj(x, g_mix, g_kv, w_qg, w_kv, *, tm=1024):
    b, s, d = x.shape
    hd = N_HEADS * HEAD_DIM
    gd = N_GROUPS * HEAD_DIM
    w_q = w_qg[:, :hd].astype(BF16)
    w_g = jnp.pad(w_qg[:, hd:], ((0, 0), (0, LANES - 3 * N_HEADS))).astype(BF16)
    w_kc = w_kv[:, :2 * gd].astype(BF16)
    w_kr = (w_kv[:, 2 * gd:].reshape(d, 4, N_GROUPS, HEAD_DIM)
            .transpose(0, 2, 1, 3).reshape(d, 4 * gd).astype(BF16))
    const = lambda *_: (0, 0)
    row = lambda i, j: (i, j, 0)
    one = pl.Buffered(1)
    return pl.pallas_call(
        _proj_kernel,
        grid=(b, s // tm),
        in_specs=[
            pl.BlockSpec((1, tm, d), row),
            pl.BlockSpec((1, d), const),
            pl.BlockSpec((1, d), const),
            pl.BlockSpec((d, hd), const, pipeline_mode=one),
            pl.BlockSpec((d, LANES), const, pipeline_mode=one),
            pl.BlockSpec((d, 2 * gd), const, pipeline_mode=one),
            pl.BlockSpec((d, 4 * gd), const, pipeline_mode=one),
        ],
        out_specs=[
            pl.BlockSpec((1, tm, hd), row),
            pl.BlockSpec((1, tm, LANES), row),
            pl.BlockSpec((1, 2 * gd // LANES, tm, LANES), lambda i, j: (i, 0, j, 0)),
            pl.BlockSpec((1, tm, 4 * gd), row),
        ],
        out_shape=[
            jax.ShapeDtypeStruct((b, s, hd), BF16),
            jax.ShapeDtypeStruct((b, s, LANES), F32),
            jax.ShapeDtypeStruct((b, 2 * gd // LANES, s, LANES), F32),
            jax.ShapeDtypeStruct((b, s, 4 * gd), BF16),
        ],
        compiler_params=pltpu.CompilerParams(
            dimension_semantics=("arbitrary", "arbitrary"), vmem_limit_bytes=VMEM_LIMIT),
        name="proj",
    )(x, g_mix.reshape(1, d), g_kv.reshape(1, d), w_q, w_g, w_kc, w_kr)


def _gelu_tanh(x):
    return 0.5 * x * (1.0 + jnp.tanh(math.sqrt(2.0 / math.pi) * (x + 0.044715 * (x * x * x))))


def _compress_kernel(k_ref, v_ref, pe_ref, w1_ref, b1_ref, w2_ref, o_ref):
    n_chunks = int(o_ref.shape[2])
    outs = [None, None]
    for j, src in enumerate((k_ref, v_ref)):
        first = None
        second = None
        for l in range(CMP_STRIDE):
            x = src.at[0, 0][pl.ds(l, n_chunks, stride=CMP_STRIDE), :]
            pa = _dot((x + pe_ref[j, l:l + 1, :]).astype(BF16), w1_ref[j, l])
            pb = _dot((x + pe_ref[j, CMP_STRIDE + l:CMP_STRIDE + l + 1, :]).astype(BF16),
                      w1_ref[j, CMP_STRIDE + l])
            first = pa if first is None else first + pa
            second = pb if second is None else second + pb
        pre = first + pltpu.roll(second, n_chunks - 1, 0) + b1_ref[j]
        hid = _gelu_tanh(pre).astype(BF16)
        for e in range(2):
            part = _dot(hid[:, e * CMP_HIDDEN:(e + 1) * CMP_HIDDEN], w2_ref[j])
            outs[e] = part if outs[e] is None else outs[e] + part
    for e in range(2):
        o_ref[0, e] = outs[e].astype(BF16)


def _compress(kvc, cmp_pe, cmp_w1, cmp_b1, cmp_w2):
    b, n_slabs, s, _ = kvc.shape
    n_chunks = s // CMP_STRIDE
    pairs = N_GROUPS // 2
    assert n_slabs == 2 * pairs and 2 * HEAD_DIM == LANES
    pe = jnp.concatenate([cmp_pe, cmp_pe], axis=-1)
    w1 = cmp_w1.reshape(2, CMP_LEN, HEAD_DIM, CMP_HIDDEN)
    zero = jnp.zeros_like(w1)
    w1 = jnp.concatenate([jnp.concatenate([w1, zero], axis=-1),
                          jnp.concatenate([zero, w1], axis=-1)], axis=2).astype(BF16)
    b1 = jnp.concatenate([cmp_b1, cmp_b1], axis=-1).reshape(2, 1, 2 * CMP_HIDDEN)
    w2 = jnp.stack([jnp.pad(cmp_w2[0], ((0, 0), (0, HEAD_DIM))),
                    jnp.pad(cmp_w2[1], ((0, 0), (HEAD_DIM, 0)))]).astype(BF16)
    const = lambda *_: (0, 0, 0)
    return pl.pallas_call(
        _compress_kernel,
        grid=(b, pairs),
        in_specs=[
            pl.BlockSpec((1, 1, s, LANES), lambda i, p: (i, p, 0, 0)),
            pl.BlockSpec((1, 1, s, LANES), lambda i, p: (i, pairs + p, 0, 0)),
            pl.BlockSpec((2, CMP_LEN, LANES), const),
            pl.BlockSpec((2, CMP_LEN, LANES, 2 * CMP_HIDDEN), lambda *_: (0, 0, 0, 0)),
            pl.BlockSpec((2, 1, 2 * CMP_HIDDEN), const),
            pl.BlockSpec((2, CMP_HIDDEN, LANES), const),
        ],
        out_specs=pl.BlockSpec((1, 2, n_chunks, LANES), lambda i, p: (i, p, 0, 0)),
        out_shape=jax.ShapeDtypeStruct((b, N_GROUPS, n_chunks, LANES), BF16),
        compiler_params=pltpu.CompilerParams(
            dimension_semantics=("arbitrary", "arbitrary"), vmem_limit_bytes=VMEM_LIMIT),
        name="compress",
    )(kvc, kvc, pe, w1, b1, w2)


TILE_PREV2, TILE_PREV, TILE_DIAG = 0, 1, 2
N_KINDS = 3
CMP_WIN = 32
LOG2E = 1.0 / math.log(2.0)


def _rel_bucket_np(dist):
    max_exact = N_BUCKETS // 2
    d = np.maximum(dist, 0)
    df = np.maximum(d, 1).astype(np.float32)
    large = max_exact + (np.log(df / max_exact) / np.float32(math.log(MAX_DISTANCE / max_exact))
                         * (N_BUCKETS - max_exact)).astype(np.int32)
    return np.where(d < max_exact, d, np.minimum(large, N_BUCKETS - 1))


def _toeplitz(w):
    t = w.shape[-1] // 2
    lead = w.shape[:-1]
    a = jnp.broadcast_to(w[..., None, :], lead + (t, 2 * t)).reshape(lead + (2 * t * t,))
    return a[..., :t * (2 * t - 1)].reshape(lead + (t, 2 * t - 1))[..., :t]


def _attn_tables(rel_bias, s):
    nq = s // TQ
    assert _rel_bucket_np(np.arange(TQ // 2, 4 * TQ)).min() == N_BUCKETS - 1
    relb = (rel_bias.astype(F32) - rel_bias[N_BUCKETS - 1].astype(F32)).T
    f = jnp.take(relb, jnp.asarray(_rel_bucket_np(np.arange(2 * TQ))), axis=1)
    neg = jnp.full((N_HEADS, TQ), NEG, F32)
    w_diag = jnp.concatenate([f[:, 0:1], neg, jnp.flip(f[:, 1:TQ], axis=1)], axis=1)
    w_prev = jnp.concatenate([jnp.flip(f[:, 1:TQ + 1], axis=1), jnp.zeros((N_HEADS, 1), F32),
                              jnp.flip(f[:, TQ + 1:2 * TQ], axis=1)], axis=1)
    ti = np.arange(TQ)[:, None]
    ki = np.arange(TQ)[None, :]
    prev2 = jnp.broadcast_to(jnp.asarray(np.where(ki > ti, 0.0, NEG), F32), (N_HEADS, TQ, TQ))
    b3 = jnp.stack([prev2, _toeplitz(w_prev), _toeplitz(w_diag)], axis=1)
    b3 = (b3.reshape(N_GROUPS, HPG, N_KINDS, TQ, TQ).transpose(0, 2, 1, 3, 4)
          .reshape(N_GROUPS, N_KINDS, HPG * TQ, TQ))
    n_cmp = s // CMP_STRIDE
    lead = CMP_WIN // 4
    dist = ti - CMP_STRIDE * (np.arange(CMP_WIN)[None, :] - lead) - (CMP_LEN - 1)
    pw = jnp.take(relb, jnp.asarray(_rel_bucket_np(dist)), axis=1)
    pw = jnp.where(jnp.asarray(dist >= 0), pw, NEG)
    full = jnp.concatenate([jnp.zeros((N_HEADS, TQ, n_cmp - lead), F32), pw,
                            jnp.full((N_HEADS, TQ, n_cmp), NEG, F32)], axis=-1)
    per_tile = TQ // CMP_STRIDE
    bc = jnp.stack([full[..., n_cmp - per_tile * t:2 * n_cmp - per_tile * t] for t in range(nq)])
    bc = bc.reshape(nq, N_GROUPS, HPG * TQ, n_cmp)
    return b3, bc


def _attn_consts(s):
    n_slc = s // SLC_LEN
    n_cmp = s // CMP_STRIDE
    key = np.arange(s)[:, None]
    lane = np.arange(LANES)[None, :]
    ceneg = np.where((key // SLC_LEN == lane) & (lane < n_slc), NEG, 0.0)
    j = np.arange(n_slc)[:, None]
    i = np.arange(n_cmp)[None, :]
    ov = ((i * CMP_STRIDE < j * SLC_LEN + SLC_LEN) & (i * CMP_STRIDE + CMP_LEN > j * SLC_LEN)
          & (i < n_cmp - 1))
    place = np.eye(n_slc, LANES)
    return (jnp.asarray(ceneg, BF16), jnp.asarray(np.concatenate([ov] * 3, axis=1), BF16),
            jnp.asarray(place, BF16))


BATCH_PER_STEP = 2
STAGGER, DONE = "stagger", "done"


def _attn_kernel(q_ref, gate_ref, kv_ref, kvc_ref, b3_ref, bc_ref, ceneg_ref, ov_ref, place_ref,
                 prev_ref, o_ref, *scratch, qb):
    del prev_ref
    streams = [
        _attn_stream(q_ref.at[pl.ds(e, 1)], gate_ref.at[pl.ds(e, 1)], kv_ref.at[pl.ds(e, 1)],
                     kvc_ref.at[pl.ds(e, 1)], b3_ref, bc_ref, ceneg_ref, ov_ref, place_ref,
                     o_ref.at[pl.ds(e, 1)], *[s.at[e] for s in scratch], qb=qb)
        for e in range(BATCH_PER_STEP)]
    active = []
    for stream in streams:
        active.append(stream)
        while next(stream, STAGGER) is not STAGGER:
            for other in active[:-1]:
                next(other, None)
    while active:
        for stream in list(active):
            if next(stream, DONE) is DONE:
                active.remove(stream)


def _attn_stream(q_ref, gate_ref, kv_ref, kvc_ref, b3_ref, bc_ref, ceneg_ref, ov_ref, place_ref,
                 o_ref, q4_ref, sc_ref, nsc_ref, macc_ref, mb_ref, oacc_ref, comb_ref, *, qb):
    n_slc = ov_ref.shape[0]
    rows = HPG * TQ
    lane = lax.broadcasted_iota(jnp.int32, (TQ, LANES), 1)
    low = lane < HEAD_DIM

    for pair in range(HPG // 2):
        qp = q_ref[0, :, pair * LANES:(pair + 1) * LANES].astype(F32)
        q4_ref[(2 * pair) * TQ:(2 * pair + 1) * TQ, 0:LANES] = jnp.where(low, qp, 0.0).astype(BF16)
        q4_ref[(2 * pair + 1) * TQ:(2 * pair + 2) * TQ, 0:LANES] = (
            jnp.where(low, pltpu.roll(qp, HEAD_DIM, 1), 0.0).astype(BF16))
    q4 = q4_ref[:, 0:LANES]

    def key_rows(kt, col0, n=1):
        return kv_ref[0, kt * TQ:(kt + n) * TQ, col0:col0 + LANES]

    def lane_tiles(x):
        return [x[:, i:i + LANES] for i in range(0, x.shape[1], LANES)]

    def tile_max(x):
        return functools.reduce(jnp.maximum, lane_tiles(x))

    def exp2_rel(sv, mb):
        return jnp.concatenate([jnp.exp2(t - mb) for t in lane_tiles(sv)], axis=1).astype(BF16)

    win_tiles = [(kt, kind) for kt, kind in
                 ((qb - 2, TILE_PREV2), (qb - 1, TILE_PREV), (qb, TILE_DIAG)) if kt >= 0]
    near_tiles = win_tiles[-2:]
    n_far = max(qb - 1, 0)
    far_groups = [(t, min(2, n_far - t)) for t in range(0, n_far, 2)]

    def split3(x):
        hi = x.astype(BF16)
        r1 = x - hi.astype(F32)
        mid = r1.astype(BF16)
        lo = (r1 - mid.astype(F32)).astype(BF16)
        return jnp.concatenate([hi, mid, lo], axis=1)

    def ones_v(tile):
        return jnp.concatenate([tile, jnp.ones_like(tile)], axis=1)

    def weighted(o, hh, br):
        r = slice(hh * TQ, (hh + 1) * TQ)
        l = o[r, LANES:2 * LANES]
        if br == 0:
            l = jnp.where(l > 0.0, l, 1.0)
        gate = gate_ref[0, 0, :, 3 * hh + br:3 * hh + br + 1]
        return o[r, 0:LANES] * (gate / l)

    tiles_w = [key_rows(kt, LANES) for kt, _ in win_tiles]
    s_w = []

    def win_scores():
        i = len(s_w)
        if i < len(win_tiles):
            s_w.append((_split_rows(_dot_nt, q4, tiles_w[i]) + b3_ref[0, win_tiles[i][1]]) * LOG2E)

    kvc = kvc_ref[0, 0]
    bias_c = bc_ref[0, 0]
    s_c = _split_rows(_dot_nt, q4, kvc) + bias_c
    yield
    win_scores()
    yield
    m_c = jnp.maximum(jnp.max(s_c, axis=1, keepdims=True), 0.5 * NEG)
    e_c = jnp.exp(s_c - m_c)
    o_c = _split_rows(_dot, e_c.astype(BF16), ones_v(kvc))
    yield
    win_scores()
    yield
    l_c = o_c[:, LANES:2 * LANES]
    p_c = e_c / jnp.where(l_c > 0.0, l_c, 1.0)
    psum = p_c[0:TQ]
    for hh in range(1, HPG):
        psum = psum + p_c[hh * TQ:(hh + 1) * TQ]

    imp_t = _dot_nt(ov_ref[...], split3(psum))
    yield
    win_scores()
    yield
    s_w = jnp.concatenate(s_w, axis=1)
    m_w = jnp.max(tile_max(s_w), axis=1, keepdims=True)
    p_w = exp2_rel(s_w, jnp.broadcast_to(m_w, (rows, LANES)))
    o_w = _split_rows(_dot, p_w, ones_v(jnp.concatenate(tiles_w, axis=0)))
    yield

    jrow = lax.broadcasted_iota(jnp.int32, (n_slc, TQ), 0)
    tpos = qb * TQ + lax.broadcasted_iota(jnp.int32, (n_slc, TQ), 1)
    cur = lax.shift_right_logical(tpos, int(math.log2(SLC_LEN)))
    forced = (jrow == 0) | (jrow == cur) | (jrow == cur - 1)
    score = jnp.where(forced, 3e38, jnp.where(jrow <= cur, imp_t, -1.0))
    rank = jnp.zeros((n_slc, TQ), F32)
    for k in range(n_slc):
        sk = score[k:k + 1, :]
        beats = (sk > score) | ((sk == score) & (jrow > k))
        rank = rank + beats.astype(F32)
    notsel_t = (rank >= float(N_SEL)).astype(BF16)
    q_mask = lax.dot_general(notsel_t, place_ref[...], (((0,), (0,)), ((), ())),
                             preferred_element_type=F32).astype(BF16)
    yield
    for hh in range(HPG):
        q4_ref[hh * TQ:(hh + 1) * TQ, LANES:2 * LANES] = q_mask

    for hh in range(HPG):
        comb_ref[hh * TQ:(hh + 1) * TQ, :] = weighted(o_c, hh, 0) + weighted(o_w, hh, 2)

    kt_n, n_near = near_tiles[0][0], len(near_tiles)
    ks_n = slice(kt_n * TQ, (kt_n + n_near) * TQ)
    bias_n = jnp.concatenate([b3_ref[0, kind] for _, kind in near_tiles], axis=1)
    rhs_n = jnp.concatenate([key_rows(kt_n, 0, n_near), ceneg_ref[ks_n, :]], axis=1)
    dot_n = _dot_nt if n_near == 2 else functools.partial(_split_rows, _dot_nt)
    s_n = (dot_n(q4_ref[...], rhs_n) + bias_n) * LOG2E
    yield
    nsc_ref[...] = s_n
    macc_ref[len(far_groups)] = tile_max(s_n)

    for i, (kt, n) in enumerate(far_groups):
        ks = slice(kt * TQ, (kt + n) * TQ)
        rhs = jnp.concatenate([key_rows(kt, 0, n), ceneg_ref[ks, :]], axis=1)
        dot_f = _dot_nt if n == 2 else functools.partial(_split_rows, _dot_nt)
        sv = dot_f(q4_ref[...], rhs) * LOG2E
        yield
        sc_ref[i, :, 0:n * TQ] = sv
        macc_ref[i] = tile_max(sv)

    yield STAGGER
    m_el = macc_ref[0]
    for i in range(1, len(far_groups) + 1):
        m_el = jnp.maximum(m_el, macc_ref[i])
    m_s = jnp.max(m_el, axis=1, keepdims=True)
    mb_ref[...] = jnp.broadcast_to(m_s, (rows, LANES))
    oacc_ref[...] = jnp.zeros((rows, 2 * LANES), F32)

    for i, (kt, n) in enumerate(far_groups):
        rhs = ones_v(key_rows(kt, 0, n))
        for r in (slice(0, rows // 2), slice(rows // 2, rows)):
            oacc_ref[r, :] += _dot(exp2_rel(sc_ref[i, r, 0:n * TQ], mb_ref[r, :]), rhs)
            yield

    p_n = exp2_rel(nsc_ref[...], mb_ref[...])
    o_s = oacc_ref[...] + _split_rows(_dot, p_n, ones_v(key_rows(kt_n, 0, n_near)))
    yield

    for hp in range(HPG // 2):
        even, odd = [comb_ref[hh * TQ:(hh + 1) * TQ, :] + weighted(o_s, hh, 1)
                     for hh in (2 * hp, 2 * hp + 1)]
        o_ref[0, :, hp * LANES:(hp + 1) * LANES] = (
            jnp.where(low, pltpu.roll(even, HEAD_DIM, 1), odd).astype(BF16))


def _attention(q, gates, kvr, kvcmp, rel_bias):
    b, s, hd = q.shape
    nq = s // TQ
    n_slc = s // SLC_LEN
    n_cmp = s // CMP_STRIDE
    rows = HPG * TQ
    b3, bc = _attn_tables(rel_bias, s)
    ceneg, ov, place = _attn_consts(s)
    gw = 3 * HPG
    gates_g = (gates[:, :, :3 * N_HEADS].reshape(b, s, N_GROUPS, gw).transpose(0, 2, 1, 3))
    gl = HPG * HEAD_DIM
    out = jnp.zeros((b, s, N_HEADS * HEAD_DIM), BF16)
    for qb in range(nq):
        n_groups = (max(qb - 1, 0) + 1) // 2
        n_near = min(qb + 1, 2)
        n_keys = (qb + 1) * TQ
        nb = BATCH_PER_STEP
        in_specs = [
            pl.BlockSpec((nb, TQ, gl), lambda g, i, qb=qb: (i, qb, g)),
            pl.BlockSpec((nb, 1, TQ, gw), lambda g, i, qb=qb: (i, g, qb, 0)),
            pl.BlockSpec((nb, n_keys, 4 * HEAD_DIM), lambda g, i: (i, 0, g)),
            pl.BlockSpec((nb, 1, n_cmp, LANES), lambda g, i: (i, g, 0, 0)),
            pl.BlockSpec((1, N_KINDS, rows, TQ), lambda g, i: (g, 0, 0, 0)),
            pl.BlockSpec((1, 1, rows, n_cmp), lambda g, i, qb=qb: (qb, g, 0, 0)),
            pl.BlockSpec((n_keys, LANES), lambda *_: (0, 0)),
            pl.BlockSpec((n_slc, 3 * n_cmp), lambda *_: (0, 0)),
            pl.BlockSpec((n_slc, LANES), lambda *_: (0, 0)),
            pl.BlockSpec(memory_space=pl.ANY),
        ]
        args = [q, gates_g, kvr, kvcmp, b3, bc, ceneg, ov, place, out]
        out = pl.pallas_call(
            functools.partial(_attn_kernel, qb=qb),
            grid=(N_GROUPS, b // nb),
            in_specs=in_specs,
            out_specs=pl.BlockSpec((nb, TQ, gl), lambda g, i, qb=qb: (i, qb, g)),
            out_shape=jax.ShapeDtypeStruct((b, s, N_HEADS * HEAD_DIM), BF16),
            scratch_shapes=[
                pltpu.VMEM((nb, rows, 2 * LANES), BF16),
                pltpu.VMEM((nb, max(n_groups, 1), rows, 2 * TQ), F32),
                pltpu.VMEM((nb, rows, n_near * TQ), F32),
                pltpu.VMEM((nb, n_groups + 1, rows, LANES), F32),
                pltpu.VMEM((nb, rows, LANES), F32),
                pltpu.VMEM((nb, rows, 2 * LANES), F32),
                pltpu.VMEM((nb, rows, LANES), F32),
            ],
            input_output_aliases={len(args) - 1: 0},
            compiler_params=pltpu.CompilerParams(
                dimension_semantics=("arbitrary", "arbitrary"), vmem_limit_bytes=VMEM_LIMIT),
            name=f"nsa_attn{qb}",
        )(*args)
    return out


def kernel(x, mix_norm, a_w_in, a_conv, a_w_out, ffn_norm, ffn_up, ffn_conv, ffn_down,
           kv_norm, w_kv, cmp_pe, cmp_w1, cmp_b1, cmp_w2, b_w_qg, b_w_o, rel_bias, final_norm):
    b, s, d = x.shape
    assert d == D_MODEL and s % 512 == 0 and mix_norm.shape[0] == 2 and b % BATCH_PER_STEP == 0
    x = _mixer(x, mix_norm[0], a_w_in[0], a_conv[0], a_w_out[0])
    x = _ffn(x, ffn_norm[0], ffn_up[0], ffn_conv[0], ffn_down[0])
    q, gates, kvc, kvr = _proj(x, mix_norm[1], kv_norm, b_w_qg[0], w_kv)
    kvcmp = _compress(kvc, cmp_pe, cmp_w1, cmp_b1, cmp_w2)
    attn = _attention(q, gates, kvr, kvcmp, rel_bias)
    return _ffn(x, ffn_norm[1], ffn_up[1], ffn_conv[1], ffn_down[1],
                attn=attn, w_o=b_w_o[0].astype(BF16), final_g=final_norm)
```

```python
import functools
import math

import jax
import jax.numpy as jnp
import numpy as np
from jax import lax
from jax.experimental import pallas as pl
from jax.experimental.pallas import tpu as pltpu

D_MODEL = 1024
CONV_W = 3
D_FF = 2816
N_HEADS = 16
N_GROUPS = 4
HPG = N_HEADS // N_GROUPS
HEAD_DIM = 64
CMP_LEN = 32
CMP_STRIDE = 16
CMP_HIDDEN = 128
SLC_LEN = 64
N_SEL = 16
WINDOW = 512
N_BUCKETS = 32
MAX_DISTANCE = 128
EPS = 1e-6
NEG = -1e30

LANES = 128
MXU_COLS = 256
V7X_VMEM_BYTES = 64 * 2**20
VMEM_LIMIT = V7X_VMEM_BYTES - 8 * 2**20

TQ = 256
CARRY_ROWS = 8
F32 = jnp.float32
BF16 = jnp.bfloat16


def _dot(a, b):
    return jnp.dot(a, b, preferred_element_type=F32)


def _dot_nt(a, b):
    return lax.dot_general(a, b, (((1,), (1,)), ((), ())), preferred_element_type=F32)


def _split_rows(dot, a, b):
    h = a.shape[0] // 2
    return jnp.concatenate([dot(a[:h], b), dot(a[h:], b)], axis=0)


def _rms_scale(x):
    return x * lax.rsqrt(jnp.mean(x * x, axis=-1, keepdims=True) + EPS)


def _conv3(buf_ref, cw, rows):
    c = CARRY_ROWS
    return (cw[0:1, :] * buf_ref[c - 2:c - 2 + rows, :]
            + cw[1:2, :] * buf_ref[c - 1:c - 1 + rows, :]
            + cw[2:3, :] * buf_ref[c:c + rows, :])


def _mixer_kernel(x_ref, g_ref, win_ref, cw_ref, wout_ref, o_ref, ubuf_ref, *, tm):
    d = D_MODEL

    @pl.when(pl.program_id(1) == 0)
    def _():
        ubuf_ref[0:CARRY_ROWS, :] = jnp.zeros((CARRY_ROWS, d), F32)

    x = x_ref[0]
    h = (_rms_scale(x) * g_ref[...]).astype(BF16)
    cg = _dot(h, win_ref[:, d:2 * d])
    v = _dot(h, win_ref[:, 2 * d:3 * d])
    ubuf_ref[CARRY_ROWS:CARRY_ROWS + tm, :] = cg * v
    conv = _conv3(ubuf_ref, cw_ref[...], tm)
    ubuf_ref[0:CARRY_ROWS, :] = ubuf_ref[tm:tm + CARRY_ROWS, :]
    bg = _dot(h, win_ref[:, 0:d])
    y = (bg * conv).astype(BF16)
    o_ref[0] = x + _dot(y, wout_ref[...])


def _mixer(x, g, w_in, conv_w, w_out, *, tm=1024):
    b, s, d = x.shape
    const = lambda *_: (0, 0)
    one = pl.Buffered(1)
    return pl.pallas_call(
        functools.partial(_mixer_kernel, tm=tm),
        grid=(b, s // tm),
        in_specs=[
            pl.BlockSpec((1, tm, d), lambda i, j: (i, j, 0)),
            pl.BlockSpec((1, d), const),
            pl.BlockSpec((d, 3 * d), const, pipeline_mode=one),
            pl.BlockSpec((CONV_W, d), const),
            pl.BlockSpec((d, d), const, pipeline_mode=one),
        ],
        out_specs=pl.BlockSpec((1, tm, d), lambda i, j: (i, j, 0)),
        out_shape=jax.ShapeDtypeStruct((b, s, d), F32),
        scratch_shapes=[pltpu.VMEM((CARRY_ROWS + tm, d), F32)],
        compiler_params=pltpu.CompilerParams(
            dimension_semantics=("arbitrary", "arbitrary"), vmem_limit_bytes=VMEM_LIMIT),
        name="mixer",
    )(x, g.reshape(1, d), w_in.astype(BF16), conv_w, w_out.astype(BF16))


def _ffn_chunks():
    tiles = D_FF // MXU_COLS
    assert tiles * MXU_COLS == D_FF
    first = (tiles + 1) // 2 * MXU_COLS
    return [(0, first), (first, D_FF)]


def _ffn_kernel(*refs, tm, has_attn, final_norm):
    refs = list(refs)
    x_ref = refs.pop(0)
    a_ref = refs.pop(0) if has_attn else None
    wo_ref = refs.pop(0) if has_attn else None
    g_ref, wup_ref, cw_ref, wdn_ref = refs[:4]
    refs = refs[4:]
    gf_ref = refs.pop(0) if final_norm else None
    chunks = _ffn_chunks()
    o_ref, h_ref, carry_ref, acc_ref = refs[:4]
    u_refs = refs[4:]
    assert len(u_refs) == 2 * len(chunks)

    @pl.when(pl.program_id(1) == 0)
    def _():
        carry_ref[...] = jnp.zeros(carry_ref.shape, F32)

    x = x_ref[0]
    if has_attn:
        x = x + _dot(a_ref[0], wo_ref[...])
    h_ref[...] = (_rms_scale(x) * g_ref[...]).astype(BF16)

    def up(buf_ref, cols):
        buf_ref[0:CARRY_ROWS, :] = carry_ref[:, cols]
        buf_ref[CARRY_ROWS:CARRY_ROWS + tm, :] = _dot(h_ref[...], wup_ref[:, cols])
        carry_ref[:, cols] = buf_ref[tm:tm + CARRY_ROWS, :]

    for c, (lo, hi) in enumerate(chunks):
        up(u_refs[2 * c], slice(lo, hi))
        up(u_refs[2 * c + 1], slice(D_FF + lo, D_FF + hi))

    for c, (lo, hi) in enumerate(chunks):
        a = _conv3(u_refs[2 * c], cw_ref[:, lo:hi], tm)
        gt = _conv3(u_refs[2 * c + 1], cw_ref[:, D_FF + lo:D_FF + hi], tm)
        act = (a * jax.nn.sigmoid(a) * gt).astype(BF16)
        part = _dot(act, wdn_ref[lo:hi, :])
        if c == 0:
            acc_ref[...] = x + part
        else:
            acc_ref[...] += part
    y = acc_ref[...]
    if final_norm:
        y = _rms_scale(y) * gf_ref[...]
    o_ref[0] = y


def _ffn(x, g, w_up, conv_w, w_down, *, attn=None, w_o=None, final_g=None, tm=512):
    b, s, d = x.shape
    has_attn = attn is not None
    final_norm = final_g is not None
    const = lambda *_: (0, 0)
    row = lambda i, j: (i, j, 0)
    one = pl.Buffered(1)
    args = [x]
    in_specs = [pl.BlockSpec((1, tm, d), row)]
    if has_attn:
        ka = attn.shape[-1]
        args += [attn, w_o]
        in_specs += [pl.BlockSpec((1, tm, ka), row),
                     pl.BlockSpec((ka, d), const, pipeline_mode=one)]
    args += [g.reshape(1, d), w_up.astype(BF16), conv_w, w_down.astype(BF16)]
    in_specs += [pl.BlockSpec((1, d), const),
                 pl.BlockSpec((d, 2 * D_FF), const, pipeline_mode=one),
                 pl.BlockSpec((CONV_W, 2 * D_FF), const),
                 pl.BlockSpec((D_FF, d), const, pipeline_mode=one)]
    if final_norm:
        args.append(final_g.reshape(1, d))
        in_specs.append(pl.BlockSpec((1, d), const))
    return pl.pallas_call(
        functools.partial(_ffn_kernel, tm=tm, has_attn=has_attn, final_norm=final_norm),
        grid=(b, s // tm),
        in_specs=in_specs,
        out_specs=pl.BlockSpec((1, tm, d), row),
        out_shape=jax.ShapeDtypeStruct((b, s, d), F32),
        scratch_shapes=[
            pltpu.VMEM((tm, d), BF16),
            pltpu.VMEM((CARRY_ROWS, 2 * D_FF), F32),
            pltpu.VMEM((tm, d), F32),
        ] + [pltpu.VMEM((CARRY_ROWS + tm, hi - lo), F32) for lo, hi in _ffn_chunks() for _ in range(2)],
        compiler_params=pltpu.CompilerParams(
            dimension_semantics=("arbitrary", "arbitrary"), vmem_limit_bytes=VMEM_LIMIT),
        name="ffn_attn" if has_attn else "ffn",
    )(*args)


def _proj_kernel(x_ref, gm_ref, gk_ref, wq_ref, wg_ref, wkc_ref, wkr_ref,
                 q_ref, gate_ref, kvc_ref, kvr_ref):
    xn = _rms_scale(x_ref[0])
    hq = (xn * gm_ref[...]).astype(BF16)
    hs = (xn * gk_ref[...]).astype(BF16)
    q_ref[0] = (_dot(hq, wq_ref[...]) * (HEAD_DIM ** -0.5)).astype(BF16)
    gates = jax.nn.sigmoid(_dot(hq, wg_ref[...]))
    gate_ref[0] = gates
    kvc = _dot(hs, wkc_ref[...])
    for sl in range(2 * N_GROUPS * HEAD_DIM // LANES):
        kvc_ref[0, sl] = kvc[:, sl * LANES:(sl + 1) * LANES]
    kvr_ref[0] = _dot(hs, wkr_ref[...]).astype(BF16)


def _proj(x, g_mix, g_kv, w_qg, w_kv, *, tm=1024):
    b, s, d = x.shape
    hd = N_HEADS * HEAD_DIM
    gd = N_GROUPS * HEAD_DIM
    w_q = w_qg[:, :hd].astype(BF16)
    w_g = jnp.pad(w_qg[:, hd:], ((0, 0), (0, LANES - 3 * N_HEADS))).astype(BF16)
    w_kc = w_kv[:, :2 * gd].astype(BF16)
    w_kr = (w_kv[:, 2 * gd:].reshape(d, 4, N_GROUPS, HEAD_DIM)
            .transpose(0, 2, 1, 3).reshape(d, 4 * gd).astype(BF16))
    const = lambda *_: (0, 0)
    row = lambda i, j: (i, j, 0)
    one = pl.Buffered(1)
    return pl.pallas_call(
        _proj_kernel,
        grid=(b, s // tm),
        in_specs=[
            pl.BlockSpec((1, tm, d), row),
            pl.BlockSpec((1, d), const),
            pl.BlockSpec((1, d), const),
            pl.BlockSpec((d, hd), const, pipeline_mode=one),
            pl.BlockSpec((d, LANES), const, pipeline_mode=one),
            pl.BlockSpec((d, 2 * gd), const, pipeline_mode=one),
            pl.BlockSpec((d, 4 * gd), const, pipeline_mode=one),
        ],
        out_specs=[
            pl.BlockSpec((1, tm, hd), row),
            pl.BlockSpec((1, tm, LANES), row),
            pl.BlockSpec((1, 2 * gd // LANES, tm, LANES), lambda i, j: (i, 0, j, 0)),
            pl.BlockSpec((1, tm, 4 * gd), row),
        ],
        out_shape=[
            jax.ShapeDtypeStruct((b, s, hd), BF16),
            jax.ShapeDtypeStruct((b, s, LANES), F32),
            jax.ShapeDtypeStruct((b, 2 * gd // LANES, s, LANES), F32),
            jax.ShapeDtypeStruct((b, s, 4 * gd), BF16),
        ],
        compiler_params=pltpu.CompilerParams(
            dimension_semantics=("arbitrary", "arbitrary"), vmem_limit_bytes=VMEM_LIMIT),
        name="proj",
    )(x, g_mix.reshape(1, d), g_kv.reshape(1, d), w_q, w_g, w_kc, w_kr)


def _gelu_tanh(x):
    return 0.5 * x * (1.0 + jnp.tanh(math.sqrt(2.0 / math.pi) * (x + 0.044715 * (x * x * x))))


def _compress_kernel(k_ref, v_ref, pe_ref, w1_ref, b1_ref, w2_ref, o_ref):
    n_chunks = int(o_ref.shape[2])
    outs = [None, None]
    for j, src in enumerate((k_ref, v_ref)):
        first = None
        second = None
        for l in range(CMP_STRIDE):
            x = src.at[0, 0][pl.ds(l, n_chunks, stride=CMP_STRIDE), :]
            pa = _dot((x + pe_ref[j, l:l + 1, :]).astype(BF16), w1_ref[j, l])
            pb = _dot((x + pe_ref[j, CMP_STRIDE + l:CMP_STRIDE + l + 1, :]).astype(BF16),
                      w1_ref[j, CMP_STRIDE + l])
            first = pa if first is None else first + pa
            second = pb if second is None else second + pb
        pre = first + pltpu.roll(second, n_chunks - 1, 0) + b1_ref[j]
        hid = _gelu_tanh(pre).astype(BF16)
        for e in range(2):
            part = _dot(hid[:, e * CMP_HIDDEN:(e + 1) * CMP_HIDDEN], w2_ref[j])
            outs[e] = part if outs[e] is None else outs[e] + part
    for e in range(2):
        o_ref[0, e] = outs[e].astype(BF16)


def _compress(kvc, cmp_pe, cmp_w1, cmp_b1, cmp_w2):
    b, n_slabs, s, _ = kvc.shape
    n_chunks = s // CMP_STRIDE
    pairs = N_GROUPS // 2
    assert n_slabs == 2 * pairs and 2 * HEAD_DIM == LANES
    pe = jnp.concatenate([cmp_pe, cmp_pe], axis=-1)
    w1 = cmp_w1.reshape(2, CMP_LEN, HEAD_DIM, CMP_HIDDEN)
    zero = jnp.zeros_like(w1)
    w1 = jnp.concatenate([jnp.concatenate([w1, zero], axis=-1),
                          jnp.concatenate([zero, w1], axis=-1)], axis=2).astype(BF16)
    b1 = jnp.concatenate([cmp_b1, cmp_b1], axis=-1).reshape(2, 1, 2 * CMP_HIDDEN)
    w2 = jnp.stack([jnp.pad(cmp_w2[0], ((0, 0), (0, HEAD_DIM))),
                    jnp.pad(cmp_w2[1], ((0, 0), (HEAD_DIM, 0)))]).astype(BF16)
    const = lambda *_: (0, 0, 0)
    return pl.pallas_call(
        _compress_kernel,
        grid=(b, pairs),
        in_specs=[
            pl.BlockSpec((1, 1, s, LANES), lambda i, p: (i, p, 0, 0)),
            pl.BlockSpec((1, 1, s, LANES), lambda i, p: (i, pairs + p, 0, 0)),
            pl.BlockSpec((2, CMP_LEN, LANES), const),
            pl.BlockSpec((2, CMP_LEN, LANES, 2 * CMP_HIDDEN), lambda *_: (0, 0, 0, 0)),
            pl.BlockSpec((2, 1, 2 * CMP_HIDDEN), const),
            pl.BlockSpec((2, CMP_HIDDEN, LANES), const),
        ],
        out_specs=pl.BlockSpec((1, 2, n_chunks, LANES), lambda i, p: (i, p, 0, 0)),
        out_shape=jax.ShapeDtypeStruct((b, N_GROUPS, n_chunks, LANES), BF16),
        compiler_params=pltpu.CompilerParams(
            dimension_semantics=("arbitrary", "arbitrary"), vmem_limit_bytes=VMEM_LIMIT),
        name="compress",
    )(kvc, kvc, pe, w1, b1, w2)


TILE_PREV2, TILE_PREV, TILE_DIAG = 0, 1, 2
N_KINDS = 3
CMP_WIN = 32
LOG2E = 1.0 / math.log(2.0)


def _rel_bucket_np(dist):
    max_exact = N_BUCKETS // 2
    d = np.maximum(dist, 0)
    df = np.maximum(d, 1).astype(np.float32)
    large = max_exact + (np.log(df / max_exact) / np.float32(math.log(MAX_DISTANCE / max_exact))
                         * (N_BUCKETS - max_exact)).astype(np.int32)
    return np.where(d < max_exact, d, np.minimum(large, N_BUCKETS - 1))


def _toeplitz(w):
    t = w.shape[-1] // 2
    lead = w.shape[:-1]
    a = jnp.broadcast_to(w[..., None, :], lead + (t, 2 * t)).reshape(lead + (2 * t * t,))
    return a[..., :t * (2 * t - 1)].reshape(lead + (t, 2 * t - 1))[..., :t]


def _attn_tables(rel_bias, s):
    nq = s // TQ
    lead = CMP_WIN // 4
    far_dist = min(CMP_STRIDE * (lead + 1) - (CMP_LEN - 1), TQ + 1)
    assert _rel_bucket_np(np.arange(far_dist, s + TQ)).min() == N_BUCKETS - 1
    assert TQ - 1 - CMP_STRIDE * (CMP_WIN - lead) - (CMP_LEN - 1) < 0
    assert WINDOW == 2 * TQ and TQ % SLC_LEN == 0 and TQ % CMP_STRIDE == 0
    relb = (rel_bias.astype(F32) - rel_bias[N_BUCKETS - 1].astype(F32)).T
    f = jnp.take(relb, jnp.asarray(_rel_bucket_np(np.arange(2 * TQ))), axis=1)
    neg = jnp.full((N_HEADS, TQ), NEG, F32)
    w_diag = jnp.concatenate([f[:, 0:1], neg, jnp.flip(f[:, 1:TQ], axis=1)], axis=1)
    w_prev = jnp.concatenate([jnp.flip(f[:, 1:TQ + 1], axis=1), jnp.zeros((N_HEADS, 1), F32),
                              jnp.flip(f[:, TQ + 1:2 * TQ], axis=1)], axis=1)
    ti = np.arange(TQ)[:, None]
    ki = np.arange(TQ)[None, :]
    prev2 = jnp.broadcast_to(jnp.asarray(np.where(ki > ti, 0.0, NEG), F32), (N_HEADS, TQ, TQ))
    b3 = jnp.stack([prev2, _toeplitz(w_prev), _toeplitz(w_diag)], axis=1)
    b3 = (b3.reshape(N_GROUPS, HPG, N_KINDS, TQ, TQ).transpose(0, 2, 1, 3, 4)
          .reshape(N_GROUPS, N_KINDS, HPG * TQ, TQ))
    n_cmp = s // CMP_STRIDE
    dist = ti - CMP_STRIDE * (np.arange(CMP_WIN)[None, :] - lead) - (CMP_LEN - 1)
    pw = jnp.take(relb, jnp.asarray(_rel_bucket_np(dist)), axis=1)
    pw = jnp.where(jnp.asarray(dist >= 0), pw, NEG)
    full = jnp.concatenate([jnp.zeros((N_HEADS, TQ, n_cmp - lead), F32), pw,
                            jnp.full((N_HEADS, TQ, n_cmp), NEG, F32)], axis=-1)
    per_tile = TQ // CMP_STRIDE
    bc = jnp.stack([full[..., n_cmp - per_tile * t:2 * n_cmp - per_tile * t] for t in range(nq)])
    bc = bc.reshape(nq, N_GROUPS, HPG * TQ, n_cmp)
    return b3, bc


def _attn_consts(s):
    n_slc = s // SLC_LEN
    n_cmp = s // CMP_STRIDE
    key = np.arange(s)[:, None]
    lane = np.arange(LANES)[None, :]
    ceneg = np.where((key // SLC_LEN == lane) & (lane < n_slc), NEG, 0.0)
    j = np.arange(n_slc)[:, None]
    i = np.arange(n_cmp)[None, :]
    ov = ((i * CMP_STRIDE < j * SLC_LEN + SLC_LEN) & (i * CMP_STRIDE + CMP_LEN > j * SLC_LEN)
          & (i < n_cmp - 1))
    place = np.eye(n_slc, LANES)
    return (jnp.asarray(ceneg, BF16), jnp.asarray(np.concatenate([ov] * 3, axis=1), BF16),
            jnp.asarray(place, BF16))


BATCH_PER_STEP = 2
STAGGER, DONE = "stagger", "done"


def _attn_kernel(q_ref, gate_ref, kv_ref, kvc_ref, b3_ref, bc_ref, ceneg_ref, ov_ref, place_ref,
                 prev_ref, o_ref, *scratch, qb):
    del prev_ref
    streams = [
        _attn_stream(q_ref.at[pl.ds(e, 1)], gate_ref.at[pl.ds(e, 1)], kv_ref.at[pl.ds(e, 1)],
                     kvc_ref.at[pl.ds(e, 1)], b3_ref, bc_ref, ceneg_ref, ov_ref, place_ref,
                     o_ref.at[pl.ds(e, 1)], *[s.at[e] for s in scratch], qb=qb)
        for e in range(BATCH_PER_STEP)]
    active = []
    for stream in streams:
        active.append(stream)
        while next(stream, STAGGER) is not STAGGER:
            for other in active[:-1]:
                next(other, None)
    while active:
        for stream in list(active):
            if next(stream, DONE) is DONE:
                active.remove(stream)


def _attn_stream(q_ref, gate_ref, kv_ref, kvc_ref, b3_ref, bc_ref, ceneg_ref, ov_ref, place_ref,
                 o_ref, q4_ref, sc_ref, nsc_ref, macc_ref, mb_ref, oacc_ref, comb_ref, *, qb):
    n_slc = ov_ref.shape[0]
    rows = HPG * TQ
    lane = lax.broadcasted_iota(jnp.int32, (TQ, LANES), 1)
    low = lane < HEAD_DIM

    for pair in range(HPG // 2):
        qp = q_ref[0, :, pair * LANES:(pair + 1) * LANES].astype(F32)
        q4_ref[(2 * pair) * TQ:(2 * pair + 1) * TQ, 0:LANES] = jnp.where(low, qp, 0.0).astype(BF16)
        q4_ref[(2 * pair + 1) * TQ:(2 * pair + 2) * TQ, 0:LANES] = (
            jnp.where(low, pltpu.roll(qp, HEAD_DIM, 1), 0.0).astype(BF16))
    q4 = q4_ref[:, 0:LANES]

    def key_rows(kt, col0, n=1):
        return kv_ref[0, kt * TQ:(kt + n) * TQ, col0:col0 + LANES]

    def lane_tiles(x):
        return [x[:, i:i + LANES] for i in range(0, x.shape[1], LANES)]

    def tile_max(x):
        return functools.reduce(jnp.maximum, lane_tiles(x))

    def exp2_rel(sv, mb):
        return jnp.concatenate([jnp.exp2(t - mb) for t in lane_tiles(sv)], axis=1).astype(BF16)

    win_tiles = [(kt, kind) for kt, kind in
                 ((qb - 2, TILE_PREV2), (qb - 1, TILE_PREV), (qb, TILE_DIAG)) if kt >= 0]
    near_tiles = win_tiles[-2:]
    n_far = max(qb - 1, 0)
    far_groups = [(t, min(2, n_far - t)) for t in range(0, n_far, 2)]

    def split3(x):
        hi = x.astype(BF16)
        r1 = x - hi.astype(F32)
        mid = r1.astype(BF16)
        lo = (r1 - mid.astype(F32)).astype(BF16)
        return jnp.concatenate([hi, mid, lo], axis=1)

    def ones_v(tile):
        return jnp.concatenate([tile, jnp.ones_like(tile)], axis=1)

    def weighted(o, hh, br):
        r = slice(hh * TQ, (hh + 1) * TQ)
        l = o[r, LANES:2 * LANES]
        if br == 0:
            l = jnp.where(l > 0.0, l, 1.0)
        gate = gate_ref[0, 0, :, 3 * hh + br:3 * hh + br + 1]
        return o[r, 0:LANES] * (gate / l)

    tiles_w = [key_rows(kt, LANES) for kt, _ in win_tiles]
    s_w = []

    def win_scores():
        i = len(s_w)
        if i < len(win_tiles):
            s_w.append((_split_rows(_dot_nt, q4, tiles_w[i]) + b3_ref[0, win_tiles[i][1]]) * LOG2E)

    kvc = kvc_ref[0, 0]
    bias_c = bc_ref[0, 0]
    s_c = _split_rows(_dot_nt, q4, kvc) + bias_c
    yield
    win_scores()
    yield
    m_c = jnp.maximum(jnp.max(s_c, axis=1, keepdims=True), 0.5 * NEG)
    e_c = jnp.exp(s_c - m_c)
    o_c = _split_rows(_dot, e_c.astype(BF16), ones_v(kvc))
    yield
    win_scores()
    yield
    l_c = o_c[:, LANES:2 * LANES]
    p_c = e_c / jnp.where(l_c > 0.0, l_c, 1.0)
    psum = p_c[0:TQ]
    for hh in range(1, HPG):
        psum = psum + p_c[hh * TQ:(hh + 1) * TQ]

    imp_t = _dot_nt(ov_ref[...], split3(psum))
    yield
    win_scores()
    yield
    s_w = jnp.concatenate(s_w, axis=1)
    m_w = jnp.max(tile_max(s_w), axis=1, keepdims=True)
    p_w = exp2_rel(s_w, jnp.broadcast_to(m_w, (rows, LANES)))
    o_w = _split_rows(_dot, p_w, ones_v(jnp.concatenate(tiles_w, axis=0)))
    yield

    jrow = lax.broadcasted_iota(jnp.int32, (n_slc, TQ), 0)
    tpos = qb * TQ + lax.broadcasted_iota(jnp.int32, (n_slc, TQ), 1)
    cur = lax.shift_right_logical(tpos, int(math.log2(SLC_LEN)))
    forced = (jrow == 0) | (jrow == cur) | (jrow == cur - 1)
    score = jnp.where(forced, 3e38, jnp.where(jrow <= cur, imp_t, -1.0))
    rank = jnp.zeros((n_slc, TQ), F32)
    for k in range(n_slc):
        sk = score[k:k + 1, :]
        beats = (sk > score) | ((sk == score) & (jrow > k))
        rank = rank + beats.astype(F32)
    notsel_t = (rank >= float(N_SEL)).astype(BF16)
    q_mask = lax.dot_general(notsel_t, place_ref[...], (((0,), (0,)), ((), ())),
                             preferred_element_type=F32).astype(BF16)
    yield
    for hh in range(HPG):
        q4_ref[hh * TQ:(hh + 1) * TQ, LANES:2 * LANES] = q_mask

    for hh in range(HPG):
        comb_ref[hh * TQ:(hh + 1) * TQ, :] = weighted(o_c, hh, 0) + weighted(o_w, hh, 2)

    kt_n, n_near = near_tiles[0][0], len(near_tiles)
    ks_n = slice(kt_n * TQ, (kt_n + n_near) * TQ)
    bias_n = jnp.concatenate([b3_ref[0, kind] for _, kind in near_tiles], axis=1)
    rhs_n = jnp.concatenate([key_rows(kt_n, 0, n_near), ceneg_ref[ks_n, :]], axis=1)
    dot_n = _dot_nt if n_near == 2 else functools.partial(_split_rows, _dot_nt)
    s_n = (dot_n(q4_ref[...], rhs_n) + bias_n) * LOG2E
    yield
    nsc_ref[...] = s_n
    macc_ref[len(far_groups)] = tile_max(s_n)

    for i, (kt, n) in enumerate(far_groups):
        ks = slice(kt * TQ, (kt + n) * TQ)
        rhs = jnp.concatenate([key_rows(kt, 0, n), ceneg_ref[ks, :]], axis=1)
        dot_f = _dot_nt if n == 2 else functools.partial(_split_rows, _dot_nt)
        sv = dot_f(q4_ref[...], rhs) * LOG2E
        yield
        sc_ref[i, :, 0:n * TQ] = sv
        macc_ref[i] = tile_max(sv)

    yield STAGGER
    m_el = macc_ref[0]
    for i in range(1, len(far_groups) + 1):
        m_el = jnp.maximum(m_el, macc_ref[i])
    m_s = jnp.max(m_el, axis=1, keepdims=True)
    mb_ref[...] = jnp.broadcast_to(m_s, (rows, LANES))
    oacc_ref[...] = jnp.zeros((rows, 2 * LANES), F32)

    for i, (kt, n) in enumerate(far_groups):
        rhs = ones_v(key_rows(kt, 0, n))
        for r in (slice(0, rows // 2), slice(rows // 2, rows)):
            oacc_ref[r, :] += _dot(exp2_rel(sc_ref[i, r, 0:n * TQ], mb_ref[r, :]), rhs)
            yield

    p_n = exp2_rel(nsc_ref[...], mb_ref[...])
    o_s = oacc_ref[...] + _split_rows(_dot, p_n, ones_v(key_rows(kt_n, 0, n_near)))
    yield

    for hp in range(HPG // 2):
        even, odd = [comb_ref[hh * TQ:(hh + 1) * TQ, :] + weighted(o_s, hh, 1)
                     for hh in (2 * hp, 2 * hp + 1)]
        o_ref[0, :, hp * LANES:(hp + 1) * LANES] = (
            jnp.where(low, pltpu.roll(even, HEAD_DIM, 1), odd).astype(BF16))


def _attention(q, gates, kvr, kvcmp, rel_bias):
    b, s, hd = q.shape
    nq = s // TQ
    n_slc = s // SLC_LEN
    n_cmp = s // CMP_STRIDE
    rows = HPG * TQ
    b3, bc = _attn_tables(rel_bias, s)
    ceneg, ov, place = _attn_consts(s)
    gw = 3 * HPG
    gates_g = (gates[:, :, :3 * N_HEADS].reshape(b, s, N_GROUPS, gw).transpose(0, 2, 1, 3))
    gl = HPG * HEAD_DIM
    out = jnp.zeros((b, s, N_HEADS * HEAD_DIM), BF16)
    for qb in range(nq):
        n_groups = (max(qb - 1, 0) + 1) // 2
        n_near = min(qb + 1, 2)
        n_keys = (qb + 1) * TQ
        nb = BATCH_PER_STEP
        in_specs = [
            pl.BlockSpec((nb, TQ, gl), lambda g, i, qb=qb: (i, qb, g)),
            pl.BlockSpec((nb, 1, TQ, gw), lambda g, i, qb=qb: (i, g, qb, 0)),
            pl.BlockSpec((nb, n_keys, 4 * HEAD_DIM), lambda g, i: (i, 0, g)),
            pl.BlockSpec((nb, 1, n_cmp, LANES), lambda g, i: (i, g, 0, 0)),
            pl.BlockSpec((1, N_KINDS, rows, TQ), lambda g, i: (g, 0, 0, 0)),
            pl.BlockSpec((1, 1, rows, n_cmp), lambda g, i, qb=qb: (qb, g, 0, 0)),
            pl.BlockSpec((n_keys, LANES), lambda *_: (0, 0)),
            pl.BlockSpec((n_slc, 3 * n_cmp), lambda *_: (0, 0)),
            pl.BlockSpec((n_slc, LANES), lambda *_: (0, 0)),
            pl.BlockSpec(memory_space=pl.ANY),
        ]
        args = [q, gates_g, kvr, kvcmp, b3, bc, ceneg, ov, place, out]
        out = pl.pallas_call(
            functools.partial(_attn_kernel, qb=qb),
            grid=(N_GROUPS, b // nb),
            in_specs=in_specs,
            out_specs=pl.BlockSpec((nb, TQ, gl), lambda g, i, qb=qb: (i, qb, g)),
            out_shape=jax.ShapeDtypeStruct((b, s, N_HEADS * HEAD_DIM), BF16),
            scratch_shapes=[
                pltpu.VMEM((nb, rows, 2 * LANES), BF16),
                pltpu.VMEM((nb, max(n_groups, 1), rows, 2 * TQ), F32),
                pltpu.VMEM((nb, rows, n_near * TQ), F32),
                pltpu.VMEM((nb, n_groups + 1, rows, LANES), F32),
                pltpu.VMEM((nb, rows, LANES), F32),
                pltpu.VMEM((nb, rows, 2 * LANES), F32),
                pltpu.VMEM((nb, rows, LANES), F32),
            ],
            input_output_aliases={len(args) - 1: 0},
            compiler_params=pltpu.CompilerParams(
                dimension_semantics=("arbitrary", "arbitrary"), vmem_limit_bytes=VMEM_LIMIT),
            name=f"nsa_attn{qb}",
        )(*args)
    return out


def kernel(x, mix_norm, a_w_in, a_conv, a_w_out, ffn_norm, ffn_up, ffn_conv, ffn_down,
           kv_norm, w_kv, cmp_pe, cmp_w1, cmp_b1, cmp_w2, b_w_qg, b_w_o, rel_bias, final_norm):
    b, s, d = x.shape
    assert d == D_MODEL and s % 512 == 0 and mix_norm.shape[0] == 2 and b % BATCH_PER_STEP == 0
    x = _mixer(x, mix_norm[0], a_w_in[0], a_conv[0], a_w_out[0])
    x = _ffn(x, ffn_norm[0], ffn_up[0], ffn_conv[0], ffn_down[0])
    q, gates, kvc, kvr = _proj(x, mix_norm[1], kv_norm, b_w_qg[0], w_kv)
    kvcmp = _compress(kvc, cmp_pe, cmp_w1, cmp_b1, cmp_w2)
    attn = _attention(q, gates, kvr, kvcmp, rel_bias)
    return _ffn(x, ffn_norm[1], ffn_up[1], ffn_conv[1], ffn_down[1],
                attn=attn, w_o=b_w_o[0].astype(BF16), final_g=final_norm)
```

```python
import functools
import math

import jax
import jax.numpy as jnp
import numpy as np
from jax import lax
from jax.experimental import pallas as pl
from jax.experimental.pallas import tpu as pltpu

D_MODEL = 1024
CONV_W = 3
D_FF = 2816
N_HEADS = 16
N_GROUPS = 4
HPG = N_HEADS // N_GROUPS
HEAD_DIM = 64
CMP_LEN = 32
CMP_STRIDE = 16
CMP_HIDDEN = 128
SLC_LEN = 64
N_SEL = 16
WINDOW = 512
N_BUCKETS = 32
MAX_DISTANCE = 128
EPS = 1e-6
NEG = -1e30

LANES = 128
MXU_COLS = 256
V7X_VMEM_BYTES = 64 * 2**20
VMEM_LIMIT = V7X_VMEM_BYTES - 8 * 2**20

TQ = 256
CARRY_ROWS = 8
FFN_CHUNKS = 5
F32 = jnp.float32
BF16 = jnp.bfloat16


def _dot(a, b):
    return jnp.dot(a, b, preferred_element_type=F32)


def _dot_nt(a, b):
    return lax.dot_general(a, b, (((1,), (1,)), ((), ())), preferred_element_type=F32)


def _split_rows(dot, a, b):
    h = a.shape[0] // 2
    return jnp.concatenate([dot(a[:h], b), dot(a[h:], b)], axis=0)


def _rms_scale(x):
    return x * lax.rsqrt(jnp.mean(x * x, axis=-1, keepdims=True) + EPS)


def _conv3(buf_ref, cw, rows):
    c = CARRY_ROWS
    return (cw[0:1, :] * buf_ref[c - 2:c - 2 + rows, :]
            + cw[1:2, :] * buf_ref[c - 1:c - 1 + rows, :]
            + cw[2:3, :] * buf_ref[c:c + rows, :])


def _mixer_kernel(x_ref, g_ref, win_ref, cw_ref, wout_ref, o_ref, ubuf_ref, *, tm):
    d = D_MODEL

    @pl.when(pl.program_id(1) == 0)
    def _():
        ubuf_ref[0:CARRY_ROWS, :] = jnp.zeros((CARRY_ROWS, d), F32)

    x = x_ref[0]
    h = (_rms_scale(x) * g_ref[...]).astype(BF16)
    cg = _dot(h, win_ref[:, d:2 * d])
    v = _dot(h, win_ref[:, 2 * d:3 * d])
    ubuf_ref[CARRY_ROWS:CARRY_ROWS + tm, :] = cg * v
    conv = _conv3(ubuf_ref, cw_ref[...], tm)
    ubuf_ref[0:CARRY_ROWS, :] = ubuf_ref[tm:tm + CARRY_ROWS, :]
    bg = _dot(h, win_ref[:, 0:d])
    y = (bg * conv).astype(BF16)
    o_ref[0] = x + _dot(y, wout_ref[...])


def _mixer(x, g, w_in, conv_w, w_out, *, tm=1024):
    b, s, d = x.shape
    const = lambda *_: (0, 0)
    one = pl.Buffered(1)
    return pl.pallas_call(
        functools.partial(_mixer_kernel, tm=tm),
        grid=(b, s // tm),
        in_specs=[
            pl.BlockSpec((1, tm, d), lambda i, j: (i, j, 0)),
            pl.BlockSpec((1, d), const),
            pl.BlockSpec((d, 3 * d), const, pipeline_mode=one),
            pl.BlockSpec((CONV_W, d), const),
            pl.BlockSpec((d, d), const, pipeline_mode=one),
        ],
        out_specs=pl.BlockSpec((1, tm, d), lambda i, j: (i, j, 0)),
        out_shape=jax.ShapeDtypeStruct((b, s, d), F32),
        scratch_shapes=[pltpu.VMEM((CARRY_ROWS + tm, d), F32)],
        compiler_params=pltpu.CompilerParams(
            dimension_semantics=("arbitrary", "arbitrary"), vmem_limit_bytes=VMEM_LIMIT),
        name="mixer",
    )(x, g.reshape(1, d), w_in.astype(BF16), conv_w, w_out.astype(BF16))


def _ffn_chunks():
    tiles = D_FF // MXU_COLS
    assert tiles * MXU_COLS == D_FF
    bounds = [round(tiles * c / FFN_CHUNKS) * MXU_COLS for c in range(FFN_CHUNKS + 1)]
    return list(zip(bounds[:-1], bounds[1:]))


def _ffn_kernel(*refs, tm, has_attn, final_norm):
    refs = list(refs)
    x_ref = refs.pop(0)
    a_ref = refs.pop(0) if has_attn else None
    wo_ref = refs.pop(0) if has_attn else None
    g_ref, wup_ref, cw_ref, wdn_ref = refs[:4]
    refs = refs[4:]
    gf_ref = refs.pop(0) if final_norm else None
    chunks = _ffn_chunks()
    o_ref, h_ref, carry_ref, acc_ref = refs[:4]
    u_refs = refs[4:]
    assert len(u_refs) == 2 * len(chunks)

    @pl.when(pl.program_id(1) == 0)
    def _():
        carry_ref[...] = jnp.zeros(carry_ref.shape, F32)

    x = x_ref[0]
    if has_attn:
        x = x + _dot(a_ref[0], wo_ref[...])
    h_ref[...] = (_rms_scale(x) * g_ref[...]).astype(BF16)

    def up(buf_ref, cols):
        buf_ref[0:CARRY_ROWS, :] = carry_ref[:, cols]
        buf_ref[CARRY_ROWS:CARRY_ROWS + tm, :] = _dot(h_ref[...], wup_ref[:, cols])
        carry_ref[:, cols] = buf_ref[tm:tm + CARRY_ROWS, :]

    for c, (lo, hi) in enumerate(chunks):
        up(u_refs[2 * c], slice(lo, hi))
        up(u_refs[2 * c + 1], slice(D_FF + lo, D_FF + hi))

    for c, (lo, hi) in enumerate(chunks):
        a = _conv3(u_refs[2 * c], cw_ref[:, lo:hi], tm)
        gt = _conv3(u_refs[2 * c + 1], cw_ref[:, D_FF + lo:D_FF + hi], tm)
        act = (a * jax.nn.sigmoid(a) * gt).astype(BF16)
        part = _dot(act, wdn_ref[lo:hi, :])
        if c == 0:
            acc_ref[...] = x + part
        else:
            acc_ref[...] += part
    y = acc_ref[...]
    if final_norm:
        y = _rms_scale(y) * gf_ref[...]
    o_ref[0] = y


def _ffn(x, g, w_up, conv_w, w_down, *, attn=None, w_o=None, final_g=None, tm=512):
    b, s, d = x.shape
    has_attn = attn is not None
    final_norm = final_g is not None
    const = lambda *_: (0, 0)
    row = lambda i, j: (i, j, 0)
    one = pl.Buffered(1)
    args = [x]
    in_specs = [pl.BlockSpec((1, tm, d), row)]
    if has_attn:
        ka = attn.shape[-1]
        args += [attn, w_o]
        in_specs += [pl.BlockSpec((1, tm, ka), row),
                     pl.BlockSpec((ka, d), const, pipeline_mode=one)]
    args += [g.reshape(1, d), w_up.astype(BF16), conv_w, w_down.astype(BF16)]
    in_specs += [pl.BlockSpec((1, d), const),
                 pl.BlockSpec((d, 2 * D_FF), const, pipeline_mode=one),
                 pl.BlockSpec((CONV_W, 2 * D_FF), const),
                 pl.BlockSpec((D_FF, d), const, pipeline_mode=one)]
    if final_norm:
        args.append(final_g.reshape(1, d))
        in_specs.append(pl.BlockSpec((1, d), const))
    return pl.pallas_call(
        functools.partial(_ffn_kernel, tm=tm, has_attn=has_attn, final_norm=final_norm),
        grid=(b, s // tm),
        in_specs=in_specs,
        out_specs=pl.BlockSpec((1, tm, d), row),
        out_shape=jax.ShapeDtypeStruct((b, s, d), F32),
        scratch_shapes=[
            pltpu.VMEM((tm, d), BF16),
            pltpu.VMEM((CARRY_ROWS, 2 * D_FF), F32),
            pltpu.VMEM((tm, d), F32),
        ] + [pltpu.VMEM((CARRY_ROWS + tm, hi - lo), F32) for lo, hi in _ffn_chunks() for _ in range(2)],
        compiler_params=pltpu.CompilerParams(
            dimension_semantics=("arbitrary", "arbitrary"), vmem_limit_bytes=VMEM_LIMIT),
        name="ffn_attn" if has_attn else "ffn",
    )(*args)


def _proj_kernel(x_ref, gm_ref, gk_ref, wq_ref, wg_ref, wkc_ref, wkr_ref,
                 q_ref, gate_ref, kvc_ref, kvr_ref):
    xn = _rms_scale(x_ref[0])
    hq = (xn * gm_ref[...]).astype(BF16)
    hs = (xn * gk_ref[...]).astype(BF16)
    q_ref[0] = (_dot(hq, wq_ref[...]) * (HEAD_DIM ** -0.5)).astype(BF16)
    gates = jax.nn.sigmoid(_dot(hq, wg_ref[...]))
    gate_ref[0] = gates
    kvc = _dot(hs, wkc_ref[...])
    for sl in range(2 * N_GROUPS * HEAD_DIM // LANES):
        kvc_ref[0, sl] = kvc[:, sl * LANES:(sl + 1) * LANES]
    kvr_ref[0] = _dot(hs, wkr_ref[...]).astype(BF16)


def _proj(x, g_mix, g_kv, w_qg, w_kv, *, tm=1024):
    b, s, d = x.shape
    hd = N_HEADS * HEAD_DIM
    gd = N_GROUPS * HEAD_DIM
    w_q = w_qg[:, :hd].astype(BF16)
    w_g = jnp.pad(w_qg[:, hd:], ((0, 0), (0, LANES - 3 * N_HEADS))).astype(BF16)
    w_kc = w_kv[:, :2 * gd].astype(BF16)
    w_kr = (w_kv[:, 2 * gd:].reshape(d, 4, N_GROUPS, HEAD_DIM)
            .transpose(0, 2, 1, 3).reshape(d, 4 * gd).astype(BF16))
    const = lambda *_: (0, 0)
    row = lambda i, j: (i, j, 0)
    one = pl.Buffered(1)
    return pl.pallas_call(
        _proj_kernel,
        grid=(b, s // tm),
        in_specs=[
            pl.BlockSpec((1, tm, d), row),
            pl.BlockSpec((1, d), const),
            pl.BlockSpec((1, d), const),
            pl.BlockSpec((d, hd), const, pipeline_mode=one),
            pl.BlockSpec((d, LANES), const, pipeline_mode=one),
            pl.BlockSpec((d, 2 * gd), const, pipeline_mode=one),
            pl.BlockSpec((d, 4 * gd), const, pipeline_mode=one),
        ],
        out_specs=[
            pl.BlockSpec((1, tm, hd), row),
            pl.BlockSpec((1, tm, LANES), row),
            pl.BlockSpec((1, 2 * gd // LANES, tm, LANES), lambda i, j: (i, 0, j, 0)),
            pl.BlockSpec((1, tm, 4 * gd), row),
        ],
        out_shape=[
            jax.ShapeDtypeStruct((b, s, hd), BF16),
            jax.ShapeDtypeStruct((b, s, LANES), F32),
            jax.ShapeDtypeStruct((b, 2 * gd // LANES, s, LANES), F32),
            jax.ShapeDtypeStruct((b, s, 4 * gd), BF16),
        ],
        compiler_params=pltpu.CompilerParams(
            dimension_semantics=("arbitrary", "arbitrary"), vmem_limit_bytes=VMEM_LIMIT),
        name="proj",
    )(x, g_mix.reshape(1, d), g_kv.reshape(1, d), w_q, w_g, w_kc, w_kr)


def _gelu_tanh(x):
    return 0.5 * x * (1.0 + jnp.tanh(math.sqrt(2.0 / math.pi) * (x + 0.044715 * (x * x * x))))


def _compress_kernel(k_ref, v_ref, pe_ref, w1_ref, b1_ref, w2_ref, o_ref):
    n_chunks = int(o_ref.shape[2])
    outs = [None, None]
    for j, src in enumerate((k_ref, v_ref)):
        first = None
        second = None
        for l in range(CMP_STRIDE):
            x = src.at[0, 0][pl.ds(l, n_chunks, stride=CMP_STRIDE), :]
            pa = _dot((x + pe_ref[j, l:l + 1, :]).astype(BF16), w1_ref[j, l])
            pb = _dot((x + pe_ref[j, CMP_STRIDE + l:CMP_STRIDE + l + 1, :]).astype(BF16),
                      w1_ref[j, CMP_STRIDE + l])
            first = pa if first is None else first + pa
            second = pb if second is None else second + pb
        pre = first + pltpu.roll(second, n_chunks - 1, 0) + b1_ref[j]
        hid = _gelu_tanh(pre).astype(BF16)
        for e in range(2):
            part = _dot(hid[:, e * CMP_HIDDEN:(e + 1) * CMP_HIDDEN], w2_ref[j])
            outs[e] = part if outs[e] is None else outs[e] + part
    for e in range(2):
        o_ref[0, e] = outs[e].astype(BF16)


def _compress(kvc, cmp_pe, cmp_w1, cmp_b1, cmp_w2):
    b, n_slabs, s, _ = kvc.shape
    n_chunks = s // CMP_STRIDE
    pairs = N_GROUPS // 2
    assert n_slabs == 2 * pairs and 2 * HEAD_DIM == LANES
    pe = jnp.concatenate([cmp_pe, cmp_pe], axis=-1)
    w1 = cmp_w1.reshape(2, CMP_LEN, HEAD_DIM, CMP_HIDDEN)
    zero = jnp.zeros_like(w1)
    w1 = jnp.concatenate([jnp.concatenate([w1, zero], axis=-1),
                          jnp.concatenate([zero, w1], axis=-1)], axis=2).astype(BF16)
    b1 = jnp.concatenate([cmp_b1, cmp_b1], axis=-1).reshape(2, 1, 2 * CMP_HIDDEN)
    w2 = jnp.stack([jnp.pad(cmp_w2[0], ((0, 0), (0, HEAD_DIM))),
                    jnp.pad(cmp_w2[1], ((0, 0), (HEAD_DIM, 0)))]).astype(BF16)
    const = lambda *_: (0, 0, 0)
    return pl.pallas_call(
        _compress_kernel,
        grid=(b, pairs),
        in_specs=[
            pl.BlockSpec((1, 1, s, LANES), lambda i, p: (i, p, 0, 0)),
            pl.BlockSpec((1, 1, s, LANES), lambda i, p: (i, pairs + p, 0, 0)),
            pl.BlockSpec((2, CMP_LEN, LANES), const),
            pl.BlockSpec((2, CMP_LEN, LANES, 2 * CMP_HIDDEN), lambda *_: (0, 0, 0, 0)),
            pl.BlockSpec((2, 1, 2 * CMP_HIDDEN), const),
            pl.BlockSpec((2, CMP_HIDDEN, LANES), const),
        ],
        out_specs=pl.BlockSpec((1, 2, n_chunks, LANES), lambda i, p: (i, p, 0, 0)),
        out_shape=jax.ShapeDtypeStruct((b, N_GROUPS, n_chunks, LANES), BF16),
        compiler_params=pltpu.CompilerParams(
            dimension_semantics=("arbitrary", "arbitrary"), vmem_limit_bytes=VMEM_LIMIT),
        name="compress",
    )(kvc, kvc, pe, w1, b1, w2)


TILE_PREV2, TILE_PREV, TILE_DIAG = 0, 1, 2
N_KINDS = 3
CMP_WIN = 32
LOG2E = 1.0 / math.log(2.0)


def _rel_bucket_np(dist):
    max_exact = N_BUCKETS // 2
    d = np.maximum(dist, 0)
    df = np.maximum(d, 1).astype(np.float32)
    large = max_exact + (np.log(df / max_exact) / np.float32(math.log(MAX_DISTANCE / max_exact))
                         * (N_BUCKETS - max_exact)).astype(np.int32)
    return np.where(d < max_exact, d, np.minimum(large, N_BUCKETS - 1))


def _toeplitz(w):
    t = w.shape[-1] // 2
    lead = w.shape[:-1]
    a = jnp.broadcast_to(w[..., None, :], lead + (t, 2 * t)).reshape(lead + (2 * t * t,))
    return a[..., :t * (2 * t - 1)].reshape(lead + (t, 2 * t - 1))[..., :t]


def _attn_tables(rel_bias, s):
    nq = s // TQ
    lead = CMP_WIN // 4
    far_dist = min(CMP_STRIDE * (lead + 1) - (CMP_LEN - 1), TQ + 1)
    assert _rel_bucket_np(np.arange(far_dist, s + TQ)).min() == N_BUCKETS - 1
    assert TQ - 1 - CMP_STRIDE * (CMP_WIN - lead) - (CMP_LEN - 1) < 0
    assert WINDOW == 2 * TQ and TQ % SLC_LEN == 0 and TQ % CMP_STRIDE == 0
    relb = (rel_bias.astype(F32) - rel_bias[N_BUCKETS - 1].astype(F32)).T
    f = jnp.take(relb, jnp.asarray(_rel_bucket_np(np.arange(2 * TQ))), axis=1)
    neg = jnp.full((N_HEADS, TQ), NEG, F32)
    w_diag = jnp.concatenate([f[:, 0:1], neg, jnp.flip(f[:, 1:TQ], axis=1)], axis=1)
    w_prev = jnp.concatenate([jnp.flip(f[:, 1:TQ + 1], axis=1), jnp.zeros((N_HEADS, 1), F32),
                              jnp.flip(f[:, TQ + 1:2 * TQ], axis=1)], axis=1)
    ti = np.arange(TQ)[:, None]
    ki = np.arange(TQ)[None, :]
    prev2 = jnp.broadcast_to(jnp.asarray(np.where(ki > ti, 0.0, NEG), F32), (N_HEADS, TQ, TQ))
    b3 = jnp.stack([prev2, _toeplitz(w_prev), _toeplitz(w_diag)], axis=1)
    b3 = (b3.reshape(N_GROUPS, HPG, N_KINDS, TQ, TQ).transpose(0, 2, 1, 3, 4)
          .reshape(N_GROUPS, N_KINDS, HPG * TQ, TQ))
    n_cmp = s // CMP_STRIDE
    dist = ti - CMP_STRIDE * (np.arange(CMP_WIN)[None, :] - lead) - (CMP_LEN - 1)
    pw = jnp.take(relb, jnp.asarray(_rel_bucket_np(dist)), axis=1)
    pw = jnp.where(jnp.asarray(dist >= 0), pw, NEG)
    full = jnp.concatenate([jnp.zeros((N_HEADS, TQ, n_cmp - lead), F32), pw,
                            jnp.full((N_HEADS, TQ, n_cmp), NEG, F32)], axis=-1)
    per_tile = TQ // CMP_STRIDE
    bc = jnp.stack([full[..., n_cmp - per_tile * t:2 * n_cmp - per_tile * t] for t in range(nq)])
    bc = bc.reshape(nq, N_GROUPS, HPG * TQ, n_cmp)
    return b3, bc


def _attn_consts(s):
    n_slc = s // SLC_LEN
    n_cmp = s // CMP_STRIDE
    key = np.arange(s)[:, None]
    lane = np.arange(LANES)[None, :]
    ceneg = np.where((key // SLC_LEN == lane) & (lane < n_slc), NEG, 0.0)
    j = np.arange(n_slc)[:, None]
    i = np.arange(n_cmp)[None, :]
    ov = ((i * CMP_STRIDE < j * SLC_LEN + SLC_LEN) & (i * CMP_STRIDE + CMP_LEN > j * SLC_LEN)
          & (i < n_cmp - 1))
    place = np.eye(n_slc, LANES)
    return (jnp.asarray(ceneg, BF16), jnp.asarray(np.concatenate([ov] * 3, axis=1), BF16),
            jnp.asarray(place, BF16))


BATCH_PER_STEP = 2
STAGGER, DONE = "stagger", "done"


def _attn_kernel(q_ref, gate_ref, kv_ref, kvc_ref, b3_ref, bc_ref, ceneg_ref, ov_ref, place_ref,
                 prev_ref, o_ref, *scratch, qb):
    del prev_ref
    streams = [
        _attn_stream(q_ref.at[pl.ds(e, 1)], gate_ref.at[pl.ds(e, 1)], kv_ref.at[pl.ds(e, 1)],
                     kvc_ref.at[pl.ds(e, 1)], b3_ref, bc_ref, ceneg_ref, ov_ref, place_ref,
                     o_ref.at[pl.ds(e, 1)], *[s.at[e] for s in scratch], qb=qb)
        for e in range(BATCH_PER_STEP)]
    active = []
    for stream in streams:
        active.append(stream)
        while next(stream, STAGGER) is not STAGGER:
            for other in active[:-1]:
                next(other, None)
    while active:
        for stream in list(active):
            if next(stream, DONE) is DONE:
                active.remove(stream)


def _attn_stream(q_ref, gate_ref, kv_ref, kvc_ref, b3_ref, bc_ref, ceneg_ref, ov_ref, place_ref,
                 o_ref, q4_ref, sc_ref, nsc_ref, macc_ref, mb_ref, oacc_ref, comb_ref, *, qb):
    n_slc = ov_ref.shape[0]
    rows = HPG * TQ
    lane = lax.broadcasted_iota(jnp.int32, (TQ, LANES), 1)
    low = lane < HEAD_DIM

    for pair in range(HPG // 2):
        qp = q_ref[0, :, pair * LANES:(pair + 1) * LANES].astype(F32)
        q4_ref[(2 * pair) * TQ:(2 * pair + 1) * TQ, 0:LANES] = jnp.where(low, qp, 0.0).astype(BF16)
        q4_ref[(2 * pair + 1) * TQ:(2 * pair + 2) * TQ, 0:LANES] = (
            jnp.where(low, pltpu.roll(qp, HEAD_DIM, 1), 0.0).astype(BF16))
    q4 = q4_ref[:, 0:LANES]

    def key_rows(kt, col0, n=1):
        return kv_ref[0, kt * TQ:(kt + n) * TQ, col0:col0 + LANES]

    def lane_tiles(x):
        return [x[:, i:i + LANES] for i in range(0, x.shape[1], LANES)]

    def tile_max(x):
        return functools.reduce(jnp.maximum, lane_tiles(x))

    def exp2_rel(sv, mb):
        return jnp.concatenate([jnp.exp2(t - mb) for t in lane_tiles(sv)], axis=1).astype(BF16)

    win_tiles = [(kt, kind) for kt, kind in
                 ((qb - 2, TILE_PREV2), (qb - 1, TILE_PREV), (qb, TILE_DIAG)) if kt >= 0]
    near_tiles = win_tiles[-2:]
    n_far = max(qb - 1, 0)
    far_groups = [(t, min(2, n_far - t)) for t in range(0, n_far, 2)]

    def split3(x):
        hi = x.astype(BF16)
        r1 = x - hi.astype(F32)
        mid = r1.astype(BF16)
        lo = (r1 - mid.astype(F32)).astype(BF16)
        return jnp.concatenate([hi, mid, lo], axis=1)

    def ones_v(tile):
        return jnp.concatenate([tile, jnp.ones_like(tile)], axis=1)

    def weighted(o, hh, br):
        r = slice(hh * TQ, (hh + 1) * TQ)
        l = o[r, LANES:2 * LANES]
        if br == 0:
            l = jnp.where(l > 0.0, l, 1.0)
        gate = gate_ref[0, 0, :, 3 * hh + br:3 * hh + br + 1]
        return o[r, 0:LANES] * (gate / l)

    tiles_w = [key_rows(kt, LANES) for kt, _ in win_tiles]
    s_w = []

    def win_scores():
        i = len(s_w)
        if i < len(win_tiles):
            s_w.append((_split_rows(_dot_nt, q4, tiles_w[i]) + b3_ref[0, win_tiles[i][1]]) * LOG2E)

    kvc = kvc_ref[0, 0]
    bias_c = bc_ref[0, 0]
    s_c = _split_rows(_dot_nt, q4, kvc) + bias_c
    yield
    win_scores()
    yield
    m_c = jnp.maximum(jnp.max(s_c, axis=1, keepdims=True), 0.5 * NEG)
    e_c = jnp.exp(s_c - m_c)
    o_c = _split_rows(_dot, e_c.astype(BF16), ones_v(kvc))
    yield
    win_scores()
    yield
    l_c = o_c[:, LANES:2 * LANES]
    p_c = e_c / jnp.where(l_c > 0.0, l_c, 1.0)
    psum = p_c[0:TQ]
    for hh in range(1, HPG):
        psum = psum + p_c[hh * TQ:(hh + 1) * TQ]

    imp_t = _dot_nt(ov_ref[...], split3(psum))
    yield
    win_scores()
    yield
    s_w = jnp.concatenate(s_w, axis=1)
    m_w = jnp.max(tile_max(s_w), axis=1, keepdims=True)
    p_w = exp2_rel(s_w, jnp.broadcast_to(m_w, (rows, LANES)))
    o_w = _split_rows(_dot, p_w, ones_v(jnp.concatenate(tiles_w, axis=0)))
    yield

    jrow = lax.broadcasted_iota(jnp.int32, (n_slc, TQ), 0)
    tpos = qb * TQ + lax.broadcasted_iota(jnp.int32, (n_slc, TQ), 1)
    cur = lax.shift_right_logical(tpos, int(math.log2(SLC_LEN)))
    forced = (jrow == 0) | (jrow == cur) | (jrow == cur - 1)
    score = jnp.where(forced, 3e38, jnp.where(jrow <= cur, imp_t, -1.0))
    rank = jnp.zeros((n_slc, TQ), F32)
    for k in range(n_slc):
        sk = score[k:k + 1, :]
        beats = (sk > score) | ((sk == score) & (jrow > k))
        rank = rank + beats.astype(F32)
    notsel_t = (rank >= float(N_SEL)).astype(BF16)
    q_mask = lax.dot_general(notsel_t, place_ref[...], (((0,), (0,)), ((), ())),
                             preferred_element_type=F32).astype(BF16)
    yield
    for hh in range(HPG):
        q4_ref[hh * TQ:(hh + 1) * TQ, LANES:2 * LANES] = q_mask

    for hh in range(HPG):
        comb_ref[hh * TQ:(hh + 1) * TQ, :] = weighted(o_c, hh, 0) + weighted(o_w, hh, 2)

    kt_n, n_near = near_tiles[0][0], len(near_tiles)
    ks_n = slice(kt_n * TQ, (kt_n + n_near) * TQ)
    bias_n = jnp.concatenate([b3_ref[0, kind] for _, kind in near_tiles], axis=1)
    rhs_n = jnp.concatenate([key_rows(kt_n, 0, n_near), ceneg_ref[ks_n, :]], axis=1)
    dot_n = _dot_nt if n_near == 2 else functools.partial(_split_rows, _dot_nt)
    s_n = (dot_n(q4_ref[...], rhs_n) + bias_n) * LOG2E
    yield
    nsc_ref[...] = s_n
    macc_ref[len(far_groups)] = tile_max(s_n)

    for i, (kt, n) in enumerate(far_groups):
        ks = slice(kt * TQ, (kt + n) * TQ)
        rhs = jnp.concatenate([key_rows(kt, 0, n), ceneg_ref[ks, :]], axis=1)
        dot_f = _dot_nt if n == 2 else functools.partial(_split_rows, _dot_nt)
        sv = dot_f(q4_ref[...], rhs) * LOG2E
        yield
        sc_ref[i, :, 0:n * TQ] = sv
        macc_ref[i] = tile_max(sv)

    yield STAGGER
    m_el = macc_ref[0]
    for i in range(1, len(far_groups) + 1):
        m_el = jnp.maximum(m_el, macc_ref[i])
    m_s = jnp.max(m_el, axis=1, keepdims=True)
    mb_ref[...] = jnp.broadcast_to(m_s, (rows, LANES))
    oacc_ref[...] = jnp.zeros((rows, 2 * LANES), F32)

    for i, (kt, n) in enumerate(far_groups):
        rhs = ones_v(key_rows(kt, 0, n))
        for r in (slice(0, rows // 2), slice(rows // 2, rows)):
            oacc_ref[r, :] += _dot(exp2_rel(sc_ref[i, r, 0:n * TQ], mb_ref[r, :]), rhs)
            yield

    p_n = exp2_rel(nsc_ref[...], mb_ref[...])
    o_s = oacc_ref[...] + _split_rows(_dot, p_n, ones_v(key_rows(kt_n, 0, n_near)))
    yield

    for hp in range(HPG // 2):
        even, odd = [comb_ref[hh * TQ:(hh + 1) * TQ, :] + weighted(o_s, hh, 1)
                     for hh in (2 * hp, 2 * hp + 1)]
        o_ref[0, :, hp * LANES:(hp + 1) * LANES] = (
            jnp.where(low, pltpu.roll(even, HEAD_DIM, 1), odd).astype(BF16))


def _attention(q, gates, kvr, kvcmp, rel_bias):
    b, s, hd = q.shape
    nq = s // TQ
    n_slc = s // SLC_LEN
    n_cmp = s // CMP_STRIDE
    rows = HPG * TQ
    b3, bc = _attn_tables(rel_bias, s)
    ceneg, ov, place = _attn_consts(s)
    gw = 3 * HPG
    gates_g = (gates[:, :, :3 * N_HEADS].reshape(b, s, N_GROUPS, gw).transpose(0, 2, 1, 3))
    gl = HPG * HEAD_DIM
    out = jnp.zeros((b, s, N_HEADS * HEAD_DIM), BF16)
    for qb in range(nq):
        n_groups = (max(qb - 1, 0) + 1) // 2
        n_near = min(qb + 1, 2)
        n_keys = (qb + 1) * TQ
        nb = BATCH_PER_STEP
        in_specs = [
            pl.BlockSpec((nb, TQ, gl), lambda g, i, qb=qb: (i, qb, g)),
            pl.BlockSpec((nb, 1, TQ, gw), lambda g, i, qb=qb: (i, g, qb, 0)),
            pl.BlockSpec((nb, n_keys, 4 * HEAD_DIM), lambda g, i: (i, 0, g)),
            pl.BlockSpec((nb, 1, n_cmp, LANES), lambda g, i: (i, g, 0, 0)),
            pl.BlockSpec((1, N_KINDS, rows, TQ), lambda g, i: (g, 0, 0, 0)),
            pl.BlockSpec((1, 1, rows, n_cmp), lambda g, i, qb=qb: (qb, g, 0, 0)),
            pl.BlockSpec((n_keys, LANES), lambda *_: (0, 0)),
            pl.BlockSpec((n_slc, 3 * n_cmp), lambda *_: (0, 0)),
            pl.BlockSpec((n_slc, LANES), lambda *_: (0, 0)),
            pl.BlockSpec(memory_space=pl.ANY),
        ]
        args = [q, gates_g, kvr, kvcmp, b3, bc, ceneg, ov, place, out]
        out = pl.pallas_call(
            functools.partial(_attn_kernel, qb=qb),
            grid=(N_GROUPS, b // nb),
            in_specs=in_specs,
            out_specs=pl.BlockSpec((nb, TQ, gl), lambda g, i, qb=qb: (i, qb, g)),
            out_shape=jax.ShapeDtypeStruct((b, s, N_HEADS * HEAD_DIM), BF16),
            scratch_shapes=[
                pltpu.VMEM((nb, rows, 2 * LANES), BF16),
                pltpu.VMEM((nb, max(n_groups, 1), rows, 2 * TQ), F32),
                pltpu.VMEM((nb, rows, n_near * TQ), F32),
                pltpu.VMEM((nb, n_groups + 1, rows, LANES), F32),
                pltpu.VMEM((nb, rows, LANES), F32),
                pltpu.VMEM((nb, rows, 2 * LANES), F32),
                pltpu.VMEM((nb, rows, LANES), F32),
            ],
            input_output_aliases={len(args) - 1: 0},
            compiler_params=pltpu.CompilerParams(
                dimension_semantics=("arbitrary", "arbitrary"), vmem_limit_bytes=VMEM_LIMIT),
            name=f"nsa_attn{qb}",
        )(*args)
    return out


def kernel(x, mix_norm, a_w_in, a_conv, a_w_out, ffn_norm, ffn_up, ffn_conv, ffn_down,
           kv_norm, w_kv, cmp_pe, cmp_w1, cmp_b1, cmp_w2, b_w_qg, b_w_o, rel_bias, final_norm):
    b, s, d = x.shape
    assert d == D_MODEL and s % 512 == 0 and mix_norm.shape[0] == 2 and b % BATCH_PER_STEP == 0
    x = _mixer(x, mix_norm[0], a_w_in[0], a_conv[0], a_w_out[0])
    x = _ffn(x, ffn_norm[0], ffn_up[0], ffn_conv[0], ffn_down[0])
    q, gates, kvc, kvr = _proj(x, mix_norm[1], kv_norm, b_w_qg[0], w_kv)
    kvcmp = _compress(kvc, cmp_pe, cmp_w1, cmp_b1, cmp_w2)
    attn = _attention(q, gates, kvr, kvcmp, rel_bias)
    return _ffn(x, ffn_norm[1], ffn_up[1], ffn_conv[1], ffn_down[1],
                attn=attn, w_o=b_w_o[0].astype(BF16), final_g=final_norm)
```

```python
import functools
import math

import jax
import jax.numpy as jnp
import numpy as np
from jax import lax
from jax.experimental import pallas as pl
from jax.experimental.pallas import tpu as pltpu

D_MODEL = 1024
CONV_W = 3
D_FF = 2816
N_HEADS = 16
N_GROUPS = 4
HPG = N_HEADS // N_GROUPS
HEAD_DIM = 64
CMP_LEN = 32
CMP_STRIDE = 16
CMP_HIDDEN = 128
SLC_LEN = 64
N_SEL = 16
WINDOW = 512
N_BUCKETS = 32
MAX_DISTANCE = 128
EPS = 1e-6
NEG = -1e30

LANES = 128
MXU_COLS = 256
V7X_VMEM_BYTES = 64 * 2**20
VMEM_LIMIT = V7X_VMEM_BYTES - 8 * 2**20

TQ = 256
CARRY_ROWS = 8
FFN_CHUNKS = 5
F32 = jnp.float32
BF16 = jnp.bfloat16


def _dot(a, b):
    return jnp.dot(a, b, preferred_element_type=F32)


def _dot_nt(a, b):
    return lax.dot_general(a, b, (((1,), (1,)), ((), ())), preferred_element_type=F32)


def _split_rows(dot, a, b):
    h = a.shape[0] // 2
    return jnp.concatenate([dot(a[:h], b), dot(a[h:], b)], axis=0)


def _rms_scale(x):
    return x * lax.rsqrt(jnp.mean(x * x, axis=-1, keepdims=True) + EPS)


def _conv3(buf_ref, cw, rows):
    c = CARRY_ROWS
    return (cw[0:1, :] * buf_ref[c - 2:c - 2 + rows, :]
            + cw[1:2, :] * buf_ref[c - 1:c - 1 + rows, :]
            + cw[2:3, :] * buf_ref[c:c + rows, :])


def _mixer_kernel(x_ref, g_ref, win_ref, cw_ref, wout_ref, o_ref, ubuf_ref, *, tm):
    d = D_MODEL

    @pl.when(pl.program_id(1) == 0)
    def _():
        ubuf_ref[0:CARRY_ROWS, :] = jnp.zeros((CARRY_ROWS, d), F32)

    x = x_ref[0]
    h = (_rms_scale(x) * g_ref[...]).astype(BF16)
    cg = _dot(h, win_ref[:, d:2 * d])
    v = _dot(h, win_ref[:, 2 * d:3 * d])
    ubuf_ref[CARRY_ROWS:CARRY_ROWS + tm, :] = cg * v
    conv = _conv3(ubuf_ref, cw_ref[...], tm)
    ubuf_ref[0:CARRY_ROWS, :] = ubuf_ref[tm:tm + CARRY_ROWS, :]
    bg = _dot(h, win_ref[:, 0:d])
    y = (bg * conv).astype(BF16)
    o_ref[0] = x + _dot(y, wout_ref[...])


def _mixer(x, g, w_in, conv_w, w_out, *, tm=1024):
    b, s, d = x.shape
    const = lambda *_: (0, 0)
    one = pl.Buffered(1)
    return pl.pallas_call(
        functools.partial(_mixer_kernel, tm=tm),
        grid=(b, s // tm),
        in_specs=[
            pl.BlockSpec((1, tm, d), lambda i, j: (i, j, 0)),
            pl.BlockSpec((1, d), const),
            pl.BlockSpec((d, 3 * d), const, pipeline_mode=one),
            pl.BlockSpec((CONV_W, d), const),
            pl.BlockSpec((d, d), const, pipeline_mode=one),
        ],
        out_specs=pl.BlockSpec((1, tm, d), lambda i, j: (i, j, 0)),
        out_shape=jax.ShapeDtypeStruct((b, s, d), F32),
        scratch_shapes=[pltpu.VMEM((CARRY_ROWS + tm, d), F32)],
        compiler_params=pltpu.CompilerParams(
            dimension_semantics=("arbitrary", "arbitrary"), vmem_limit_bytes=VMEM_LIMIT),
        name="mixer",
    )(x, g.reshape(1, d), w_in.astype(BF16), conv_w, w_out.astype(BF16))


def _ffn_chunks():
    tiles = D_FF // MXU_COLS
    assert tiles * MXU_COLS == D_FF
    bounds = [round(tiles * c / FFN_CHUNKS) * MXU_COLS for c in range(FFN_CHUNKS + 1)]
    return list(zip(bounds[:-1], bounds[1:]))


def _ffn_kernel(*refs, tm, has_attn, final_norm):
    refs = list(refs)
    x_ref = refs.pop(0)
    a_ref = refs.pop(0) if has_attn else None
    wo_ref = refs.pop(0) if has_attn else None
    g_ref, wup_ref, cw_ref, wdn_ref = refs[:4]
    refs = refs[4:]
    gf_ref = refs.pop(0) if final_norm else None
    chunks = _ffn_chunks()
    o_ref, h_ref, carry_ref, acc_ref = refs[:4]
    u_refs = refs[4:]
    assert len(u_refs) == 2 * len(chunks)

    @pl.when(pl.program_id(1) == 0)
    def _():
        carry_ref[...] = jnp.zeros(carry_ref.shape, F32)

    x = x_ref[0]
    if has_attn:
        x = x + _dot(a_ref[0], wo_ref[...])
    h_ref[...] = (_rms_scale(x) * g_ref[...]).astype(BF16)

    def up(buf_ref, cols):
        buf_ref[0:CARRY_ROWS, :] = carry_ref[:, cols]
        buf_ref[CARRY_ROWS:CARRY_ROWS + tm, :] = _dot(h_ref[...], wup_ref[:, cols])
        carry_ref[:, cols] = buf_ref[tm:tm + CARRY_ROWS, :]

    for c, (lo, hi) in enumerate(chunks):
        up(u_refs[2 * c], slice(lo, hi))
        up(u_refs[2 * c + 1], slice(D_FF + lo, D_FF + hi))

    for c, (lo, hi) in enumerate(chunks):
        a = _conv3(u_refs[2 * c], cw_ref[:, lo:hi], tm)
        gt = _conv3(u_refs[2 * c + 1], cw_ref[:, D_FF + lo:D_FF + hi], tm)
        act = (a * jax.nn.sigmoid(a) * gt).astype(BF16)
        part = _dot(act, wdn_ref[lo:hi, :])
        if c == 0:
            acc_ref[...] = x + part
        else:
            acc_ref[...] += part
    y = acc_ref[...]
    if final_norm:
        y = _rms_scale(y) * gf_ref[...]
    o_ref[0] = y


def _ffn(x, g, w_up, conv_w, w_down, *, attn=None, w_o=None, final_g=None, tm=512):
    b, s, d = x.shape
    has_attn = attn is not None
    final_norm = final_g is not None
    const = lambda *_: (0, 0)
    row = lambda i, j: (i, j, 0)
    one = pl.Buffered(1)
    args = [x]
    in_specs = [pl.BlockSpec((1, tm, d), row)]
    if has_attn:
        ka = attn.shape[-1]
        args += [attn, w_o]
        in_specs += [pl.BlockSpec((1, tm, ka), row),
                     pl.BlockSpec((ka, d), const, pipeline_mode=one)]
    args += [g.reshape(1, d), w_up.astype(BF16), conv_w, w_down.astype(BF16)]
    in_specs += [pl.BlockSpec((1, d), const),
                 pl.BlockSpec((d, 2 * D_FF), const, pipeline_mode=one),
                 pl.BlockSpec((CONV_W, 2 * D_FF), const),
                 pl.BlockSpec((D_FF, d), const, pipeline_mode=one)]
    if final_norm:
        args.append(final_g.reshape(1, d))
        in_specs.append(pl.BlockSpec((1, d), const))
    return pl.pallas_call(
        functools.partial(_ffn_kernel, tm=tm, has_attn=has_attn, final_norm=final_norm),
        grid=(b, s // tm),
        in_specs=in_specs,
        out_specs=pl.BlockSpec((1, tm, d), row),
        out_shape=jax.ShapeDtypeStruct((b, s, d), F32),
        scratch_shapes=[
            pltpu.VMEM((tm, d), BF16),
            pltpu.VMEM((CARRY_ROWS, 2 * D_FF), F32),
            pltpu.VMEM((tm, d), F32),
        ] + [pltpu.VMEM((CARRY_ROWS + tm, hi - lo), F32) for lo, hi in _ffn_chunks() for _ in range(2)],
        compiler_params=pltpu.CompilerParams(
            dimension_semantics=("arbitrary", "arbitrary"), vmem_limit_bytes=VMEM_LIMIT),
        name="ffn_attn" if has_attn else "ffn",
    )(*args)


def _proj_kernel(x_ref, gm_ref, gk_ref, wq_ref, wg_ref, wkc_ref, wkr_ref,
                 q_ref, gate_ref, kvc_ref, kvr_ref):
    xn = _rms_scale(x_ref[0])
    hq = (xn * gm_ref[...]).astype(BF16)
    hs = (xn * gk_ref[...]).astype(BF16)
    q_ref[0] = (_dot(hq, wq_ref[...]) * (HEAD_DIM ** -0.5)).astype(BF16)
    gates = jax.nn.sigmoid(_dot(hq, wg_ref[...]))
    gate_ref[0] = gates
    kvc = _dot(hs, wkc_ref[...])
    for sl in range(2 * N_GROUPS * HEAD_DIM // LANES):
        kvc_ref[0, sl] = kvc[:, sl * LANES:(sl + 1) * LANES]
    kvr_ref[0] = _dot(hs, wkr_ref[...]).astype(BF16)


def _proj(x, g_mix, g_kv, w_qg, w_kv, *, tm=1024):
    b, s, d = x.shape
    hd = N_HEADS * HEAD_DIM
    gd = N_GROUPS * HEAD_DIM
    w_q = w_qg[:, :hd].astype(BF16)
    w_g = jnp.pad(w_qg[:, hd:], ((0, 0), (0, LANES - 3 * N_HEADS))).astype(BF16)
    w_kc = w_kv[:, :2 * gd].astype(BF16)
    w_kr = (w_kv[:, 2 * gd:].reshape(d, 4, N_GROUPS, HEAD_DIM)
            .transpose(0, 2, 1, 3).reshape(d, 4 * gd).astype(BF16))
    const = lambda *_: (0, 0)
    row = lambda i, j: (i, j, 0)
    one = pl.Buffered(1)
    return pl.pallas_call(
        _proj_kernel,
        grid=(b, s // tm),
        in_specs=[
            pl.BlockSpec((1, tm, d), row),
            pl.BlockSpec((1, d), const),
            pl.BlockSpec((1, d), const),
            pl.BlockSpec((d, hd), const, pipeline_mode=one),
            pl.BlockSpec((d, LANES), const, pipeline_mode=one),
            pl.BlockSpec((d, 2 * gd), const, pipeline_mode=one),
            pl.BlockSpec((d, 4 * gd), const, pipeline_mode=one),
        ],
        out_specs=[
            pl.BlockSpec((1, tm, hd), row),
            pl.BlockSpec((1, tm, LANES), row),
            pl.BlockSpec((1, 2 * gd // LANES, tm, LANES), lambda i, j: (i, 0, j, 0)),
            pl.BlockSpec((1, tm, 4 * gd), row),
        ],
        out_shape=[
            jax.ShapeDtypeStruct((b, s, hd), BF16),
            jax.ShapeDtypeStruct((b, s, LANES), F32),
            jax.ShapeDtypeStruct((b, 2 * gd // LANES, s, LANES), F32),
            jax.ShapeDtypeStruct((b, s, 4 * gd), BF16),
        ],
        compiler_params=pltpu.CompilerParams(
            dimension_semantics=("arbitrary", "arbitrary"), vmem_limit_bytes=VMEM_LIMIT),
        name="proj",
    )(x, g_mix.reshape(1, d), g_kv.reshape(1, d), w_q, w_g, w_kc, w_kr)


def _gelu_tanh(x):
    return 0.5 * x * (1.0 + jnp.tanh(math.sqrt(2.0 / math.pi) * (x + 0.044715 * (x * x * x))))


def _compress_kernel(k_ref, v_ref, pe_ref, w1_ref, b1_ref, w2_ref, o_ref):
    n_chunks = int(o_ref.shape[2])
    outs = [None, None]
    for j, src in enumerate((k_ref, v_ref)):
        first = None
        second = None
        for l in range(CMP_STRIDE):
            x = src.at[0, 0][pl.ds(l, n_chunks, stride=CMP_STRIDE), :]
            pa = _dot((x + pe_ref[j, l:l + 1, :]).astype(BF16), w1_ref[j, l])
            pb = _dot((x + pe_ref[j, CMP_STRIDE + l:CMP_STRIDE + l + 1, :]).astype(BF16),
                      w1_ref[j, CMP_STRIDE + l])
            first = pa if first is None else first + pa
            second = pb if second is None else second + pb
        pre = first + pltpu.roll(second, n_chunks - 1, 0) + b1_ref[j]
        hid = _gelu_tanh(pre).astype(BF16)
        for e in range(2):
            part = _dot(hid[:, e * CMP_HIDDEN:(e + 1) * CMP_HIDDEN], w2_ref[j])
            outs[e] = part if outs[e] is None else outs[e] + part
    for e in range(2):
        o_ref[0, e] = outs[e].astype(BF16)


def _compress(kvc, cmp_pe, cmp_w1, cmp_b1, cmp_w2):
    b, n_slabs, s, _ = kvc.shape
    n_chunks = s // CMP_STRIDE
    pairs = N_GROUPS // 2
    assert n_slabs == 2 * pairs and 2 * HEAD_DIM == LANES
    pe = jnp.concatenate([cmp_pe, cmp_pe], axis=-1)
    w1 = cmp_w1.reshape(2, CMP_LEN, HEAD_DIM, CMP_HIDDEN)
    zero = jnp.zeros_like(w1)
    w1 = jnp.concatenate([jnp.concatenate([w1, zero], axis=-1),
                          jnp.concatenate([zero, w1], axis=-1)], axis=2).astype(BF16)
    b1 = jnp.concatenate([cmp_b1, cmp_b1], axis=-1).reshape(2, 1, 2 * CMP_HIDDEN)
    w2 = jnp.stack([jnp.pad(cmp_w2[0], ((0, 0), (0, HEAD_DIM))),
                    jnp.pad(cmp_w2[1], ((0, 0), (HEAD_DIM, 0)))]).astype(BF16)
    const = lambda *_: (0, 0, 0)
    return pl.pallas_call(
        _compress_kernel,
        grid=(b, pairs),
        in_specs=[
            pl.BlockSpec((1, 1, s, LANES), lambda i, p: (i, p, 0, 0)),
            pl.BlockSpec((1, 1, s, LANES), lambda i, p: (i, pairs + p, 0, 0)),
            pl.BlockSpec((2, CMP_LEN, LANES), const),
            pl.BlockSpec((2, CMP_LEN, LANES, 2 * CMP_HIDDEN), lambda *_: (0, 0, 0, 0)),
            pl.BlockSpec((2, 1, 2 * CMP_HIDDEN), const),
            pl.BlockSpec((2, CMP_HIDDEN, LANES), const),
        ],
        out_specs=pl.BlockSpec((1, 2, n_chunks, LANES), lambda i, p: (i, p, 0, 0)),
        out_shape=jax.ShapeDtypeStruct((b, N_GROUPS, n_chunks, LANES), BF16),
        compiler_params=pltpu.CompilerParams(
            dimension_semantics=("arbitrary", "arbitrary"), vmem_limit_bytes=VMEM_LIMIT),
        name="compress",
    )(kvc, kvc, pe, w1, b1, w2)


TILE_PREV2, TILE_PREV, TILE_DIAG = 0, 1, 2
N_KINDS = 3
CMP_WIN = 32
LOG2E = 1.0 / math.log(2.0)


def _rel_bucket_np(dist):
    max_exact = N_BUCKETS // 2
    d = np.maximum(dist, 0)
    df = np.maximum(d, 1).astype(np.float32)
    large = max_exact + (np.log(df / max_exact) / np.float32(math.log(MAX_DISTANCE / max_exact))
                         * (N_BUCKETS - max_exact)).astype(np.int32)
    return np.where(d < max_exact, d, np.minimum(large, N_BUCKETS - 1))


def _toeplitz(w):
    t = w.shape[-1] // 2
    lead = w.shape[:-1]
    a = jnp.broadcast_to(w[..., None, :], lead + (t, 2 * t)).reshape(lead + (2 * t * t,))
    return a[..., :t * (2 * t - 1)].reshape(lead + (t, 2 * t - 1))[..., :t]


def _attn_tables(rel_bias, s):
    nq = s // TQ
    lead = CMP_WIN // 4
    far_dist = min(CMP_STRIDE * (lead + 1) - (CMP_LEN - 1), TQ + 1)
    assert _rel_bucket_np(np.arange(far_dist, s + TQ)).min() == N_BUCKETS - 1
    assert TQ - 1 - CMP_STRIDE * (CMP_WIN - lead) - (CMP_LEN - 1) < 0
    assert WINDOW == 2 * TQ and TQ % SLC_LEN == 0 and TQ % CMP_STRIDE == 0
    relb = (rel_bias.astype(F32) - rel_bias[N_BUCKETS - 1].astype(F32)).T
    f = jnp.take(relb, jnp.asarray(_rel_bucket_np(np.arange(2 * TQ))), axis=1)
    neg = jnp.full((N_HEADS, TQ), NEG, F32)
    w_diag = jnp.concatenate([f[:, 0:1], neg, jnp.flip(f[:, 1:TQ], axis=1)], axis=1)
    w_prev = jnp.concatenate([jnp.flip(f[:, 1:TQ + 1], axis=1), jnp.zeros((N_HEADS, 1), F32),
                              jnp.flip(f[:, TQ + 1:2 * TQ], axis=1)], axis=1)
    ti = np.arange(TQ)[:, None]
    ki = np.arange(TQ)[None, :]
    prev2 = jnp.broadcast_to(jnp.asarray(np.where(ki > ti, 0.0, NEG), F32), (N_HEADS, TQ, TQ))
    b3 = jnp.stack([prev2, _toeplitz(w_prev), _toeplitz(w_diag)], axis=1)
    b3 = (b3.reshape(N_GROUPS, HPG, N_KINDS, TQ, TQ).transpose(0, 2, 1, 3, 4)
          .reshape(N_GROUPS, N_KINDS, HPG * TQ, TQ))
    n_cmp = s // CMP_STRIDE
    dist = ti - CMP_STRIDE * (np.arange(CMP_WIN)[None, :] - lead) - (CMP_LEN - 1)
    pw = jnp.take(relb, jnp.asarray(_rel_bucket_np(dist)), axis=1)
    pw = jnp.where(jnp.asarray(dist >= 0), pw, NEG)
    full = jnp.concatenate([jnp.zeros((N_HEADS, TQ, n_cmp - lead), F32), pw,
                            jnp.full((N_HEADS, TQ, n_cmp), NEG, F32)], axis=-1)
    per_tile = TQ // CMP_STRIDE
    bc = jnp.stack([full[..., n_cmp - per_tile * t:2 * n_cmp - per_tile * t] for t in range(nq)])
    bc = bc.reshape(nq, N_GROUPS, HPG * TQ, n_cmp)
    return b3, bc


def _attn_consts(s):
    n_slc = s // SLC_LEN
    n_cmp = s // CMP_STRIDE
    key = np.arange(s)[:, None]
    lane = np.arange(LANES)[None, :]
    ceneg = np.where((key // SLC_LEN == lane) & (lane < n_slc), NEG, 0.0)
    j = np.arange(n_slc)[:, None]
    i = np.arange(n_cmp)[None, :]
    ov = ((i * CMP_STRIDE < j * SLC_LEN + SLC_LEN) & (i * CMP_STRIDE + CMP_LEN > j * SLC_LEN)
          & (i < n_cmp - 1))
    place = np.eye(n_slc, LANES)
    return (jnp.asarray(ceneg, BF16), jnp.asarray(np.concatenate([ov] * 3, axis=1), BF16),
            jnp.asarray(place, BF16))


BATCH_PER_STEP = 2
STAGGER, DONE = "stagger", "done"


def _attn_kernel(q_ref, gate_ref, kv_ref, kvc_ref, b3_ref, bc_ref, ceneg_ref, ov_ref, place_ref,
                 o_ref, *scratch, qb):
    streams = [
        _attn_stream(q_ref.at[pl.ds(e, 1)], gate_ref.at[pl.ds(e, 1)], kv_ref.at[pl.ds(e, 1)],
                     kvc_ref.at[pl.ds(e, 1)], b3_ref, bc_ref, ceneg_ref, ov_ref, place_ref,
                     o_ref.at[pl.ds(e, 1)], *[s.at[e] for s in scratch], qb=qb)
        for e in range(BATCH_PER_STEP)]
    active = []
    for stream in streams:
        active.append(stream)
        while next(stream, STAGGER) is not STAGGER:
            for other in active[:-1]:
                next(other, None)
    while active:
        for stream in list(active):
            if next(stream, DONE) is DONE:
                active.remove(stream)


def _attn_stream(q_ref, gate_ref, kv_ref, kvc_ref, b3_ref, bc_ref, ceneg_ref, ov_ref, place_ref,
                 o_ref, q4_ref, sc_ref, nsc_ref, macc_ref, mb_ref, oacc_ref, comb_ref, *, qb):
    n_slc = ov_ref.shape[0]
    rows = HPG * TQ
    lane = lax.broadcasted_iota(jnp.int32, (TQ, LANES), 1)
    low = lane < HEAD_DIM

    for pair in range(HPG // 2):
        qp = q_ref[0, :, pair * LANES:(pair + 1) * LANES].astype(F32)
        q4_ref[(2 * pair) * TQ:(2 * pair + 1) * TQ, 0:LANES] = jnp.where(low, qp, 0.0).astype(BF16)
        q4_ref[(2 * pair + 1) * TQ:(2 * pair + 2) * TQ, 0:LANES] = (
            jnp.where(low, pltpu.roll(qp, HEAD_DIM, 1), 0.0).astype(BF16))
    q4 = q4_ref[:, 0:LANES]

    def key_rows(kt, col0, n=1):
        return kv_ref[0, kt * TQ:(kt + n) * TQ, col0:col0 + LANES]

    def lane_tiles(x):
        return [x[:, i:i + LANES] for i in range(0, x.shape[1], LANES)]

    def tile_max(x):
        return functools.reduce(jnp.maximum, lane_tiles(x))

    def exp2_rel(sv, mb):
        return jnp.concatenate([jnp.exp2(t - mb) for t in lane_tiles(sv)], axis=1).astype(BF16)

    win_tiles = [(kt, kind) for kt, kind in
                 ((qb - 2, TILE_PREV2), (qb - 1, TILE_PREV), (qb, TILE_DIAG)) if kt >= 0]
    near_tiles = win_tiles[-2:]
    n_far = max(qb - 1, 0)
    far_groups = [(t, min(2, n_far - t)) for t in range(0, n_far, 2)]

    def split3(x):
        hi = x.astype(BF16)
        r1 = x - hi.astype(F32)
        mid = r1.astype(BF16)
        lo = (r1 - mid.astype(F32)).astype(BF16)
        return jnp.concatenate([hi, mid, lo], axis=1)

    def ones_v(tile):
        return jnp.concatenate([tile, jnp.ones_like(tile)], axis=1)

    def weighted(o, hh, br):
        r = slice(hh * TQ, (hh + 1) * TQ)
        l = o[r, LANES:2 * LANES]
        if br == 0:
            l = jnp.where(l > 0.0, l, 1.0)
        gate = gate_ref[0, 0, :, 3 * hh + br:3 * hh + br + 1]
        return o[r, 0:LANES] * (gate / l)

    tiles_w = [key_rows(kt, LANES) for kt, _ in win_tiles]
    s_w = []

    def win_scores():
        i = len(s_w)
        if i < len(win_tiles):
            s_w.append((_split_rows(_dot_nt, q4, tiles_w[i]) + b3_ref[0, win_tiles[i][1]]) * LOG2E)

    kvc = kvc_ref[0, 0]
    bias_c = bc_ref[0, 0]
    s_c = _split_rows(_dot_nt, q4, kvc) + bias_c
    yield
    win_scores()
    yield
    m_c = jnp.maximum(jnp.max(s_c, axis=1, keepdims=True), 0.5 * NEG)
    e_c = jnp.exp(s_c - m_c)
    o_c = _split_rows(_dot, e_c.astype(BF16), ones_v(kvc))
    yield
    win_scores()
    yield
    l_c = o_c[:, LANES:2 * LANES]
    p_c = e_c / jnp.where(l_c > 0.0, l_c, 1.0)
    psum = p_c[0:TQ]
    for hh in range(1, HPG):
        psum = psum + p_c[hh * TQ:(hh + 1) * TQ]

    imp_t = _dot_nt(ov_ref[...], split3(psum))
    yield
    win_scores()
    yield
    s_w = jnp.concatenate(s_w, axis=1)
    m_w = jnp.max(tile_max(s_w), axis=1, keepdims=True)
    p_w = exp2_rel(s_w, jnp.broadcast_to(m_w, (rows, LANES)))
    o_w = _split_rows(_dot, p_w, ones_v(jnp.concatenate(tiles_w, axis=0)))
    yield

    jrow = lax.broadcasted_iota(jnp.int32, (n_slc, TQ), 0)
    tpos = qb * TQ + lax.broadcasted_iota(jnp.int32, (n_slc, TQ), 1)
    cur = lax.shift_right_logical(tpos, int(math.log2(SLC_LEN)))
    forced = (jrow == 0) | (jrow == cur) | (jrow == cur - 1)
    score = jnp.where(forced, 3e38, jnp.where(jrow <= cur, imp_t, -1.0))
    rank = jnp.zeros((n_slc, TQ), F32)
    for k in range(n_slc):
        sk = score[k:k + 1, :]
        beats = (sk > score) | ((sk == score) & (jrow > k))
        rank = rank + beats.astype(F32)
    notsel_t = (rank >= float(N_SEL)).astype(BF16)
    q_mask = lax.dot_general(notsel_t, place_ref[...], (((0,), (0,)), ((), ())),
                             preferred_element_type=F32).astype(BF16)
    yield
    for hh in range(HPG):
        q4_ref[hh * TQ:(hh + 1) * TQ, LANES:2 * LANES] = q_mask

    for hh in range(HPG):
        comb_ref[hh * TQ:(hh + 1) * TQ, :] = weighted(o_c, hh, 0) + weighted(o_w, hh, 2)

    kt_n, n_near = near_tiles[0][0], len(near_tiles)
    ks_n = slice(kt_n * TQ, (kt_n + n_near) * TQ)
    bias_n = jnp.concatenate([b3_ref[0, kind] for _, kind in near_tiles], axis=1)
    rhs_n = jnp.concatenate([key_rows(kt_n, 0, n_near), ceneg_ref[ks_n, :]], axis=1)
    dot_n = _dot_nt if n_near == 2 else functools.partial(_split_rows, _dot_nt)
    s_n = (dot_n(q4_ref[...], rhs_n) + bias_n) * LOG2E
    yield
    nsc_ref[...] = s_n
    macc_ref[len(far_groups)] = tile_max(s_n)

    for i, (kt, n) in enumerate(far_groups):
        ks = slice(kt * TQ, (kt + n) * TQ)
        rhs = jnp.concatenate([key_rows(kt, 0, n), ceneg_ref[ks, :]], axis=1)
        dot_f = _dot_nt if n == 2 else functools.partial(_split_rows, _dot_nt)
        sv = dot_f(q4_ref[...], rhs) * LOG2E
        yield
        sc_ref[i, :, 0:n * TQ] = sv
        macc_ref[i] = tile_max(sv)

    yield STAGGER
    m_el = macc_ref[0]
    for i in range(1, len(far_groups) + 1):
        m_el = jnp.maximum(m_el, macc_ref[i])
    m_s = jnp.max(m_el, axis=1, keepdims=True)
    mb_ref[...] = jnp.broadcast_to(m_s, (rows, LANES))
    oacc_ref[...] = jnp.zeros((rows, 2 * LANES), F32)

    for i, (kt, n) in enumerate(far_groups):
        rhs = ones_v(key_rows(kt, 0, n))
        for r in (slice(0, rows // 2), slice(rows // 2, rows)):
            oacc_ref[r, :] += _dot(exp2_rel(sc_ref[i, r, 0:n * TQ], mb_ref[r, :]), rhs)
            yield

    p_n = exp2_rel(nsc_ref[...], mb_ref[...])
    o_s = oacc_ref[...] + _split_rows(_dot, p_n, ones_v(key_rows(kt_n, 0, n_near)))
    yield

    for hp in range(HPG // 2):
        even, odd = [comb_ref[hh * TQ:(hh + 1) * TQ, :] + weighted(o_s, hh, 1)
                     for hh in (2 * hp, 2 * hp + 1)]
        o_ref[0, :, hp * LANES:(hp + 1) * LANES] = (
            jnp.where(low, pltpu.roll(even, HEAD_DIM, 1), odd).astype(BF16))


def _attention(q, gates, kvr, kvcmp, rel_bias):
    b, s, hd = q.shape
    nq = s // TQ
    n_slc = s // SLC_LEN
    n_cmp = s // CMP_STRIDE
    rows = HPG * TQ
    b3, bc = _attn_tables(rel_bias, s)
    ceneg, ov, place = _attn_consts(s)
    gw = 3 * HPG
    gates_g = (gates[:, :, :3 * N_HEADS].reshape(b, s, N_GROUPS, gw).transpose(0, 2, 1, 3))
    gl = HPG * HEAD_DIM
    out = q
    for qb in range(nq):
        n_groups = (max(qb - 1, 0) + 1) // 2
        n_near = min(qb + 1, 2)
        n_keys = (qb + 1) * TQ
        nb = BATCH_PER_STEP
        in_specs = [
            pl.BlockSpec((nb, TQ, gl), lambda g, i, qb=qb: (i, qb, g)),
            pl.BlockSpec((nb, 1, TQ, gw), lambda g, i, qb=qb: (i, g, qb, 0)),
            pl.BlockSpec((nb, n_keys, 4 * HEAD_DIM), lambda g, i: (i, 0, g)),
            pl.BlockSpec((nb, 1, n_cmp, LANES), lambda g, i: (i, g, 0, 0)),
            pl.BlockSpec((1, N_KINDS, rows, TQ), lambda g, i: (g, 0, 0, 0)),
            pl.BlockSpec((1, 1, rows, n_cmp), lambda g, i, qb=qb: (qb, g, 0, 0)),
            pl.BlockSpec((n_keys, LANES), lambda *_: (0, 0)),
            pl.BlockSpec((n_slc, 3 * n_cmp), lambda *_: (0, 0)),
            pl.BlockSpec((n_slc, LANES), lambda *_: (0, 0)),
        ]
        args = [out, gates_g, kvr, kvcmp, b3, bc, ceneg, ov, place]
        out = pl.pallas_call(
            functools.partial(_attn_kernel, qb=qb),
            grid=(N_GROUPS, b // nb),
            in_specs=in_specs,
            out_specs=pl.BlockSpec((nb, TQ, gl), lambda g, i, qb=qb: (i, qb, g)),
            out_shape=jax.ShapeDtypeStruct((b, s, N_HEADS * HEAD_DIM), BF16),
            scratch_shapes=[
                pltpu.VMEM((nb, rows, 2 * LANES), BF16),
                pltpu.VMEM((nb, max(n_groups, 1), rows, 2 * TQ), F32),
                pltpu.VMEM((nb, rows, n_near * TQ), F32),
                pltpu.VMEM((nb, n_groups + 1, rows, LANES), F32),
                pltpu.VMEM((nb, rows, LANES), F32),
                pltpu.VMEM((nb, rows, 2 * LANES), F32),
                pltpu.VMEM((nb, rows, LANES), F32),
            ],
            input_output_aliases={0: 0},
            compiler_params=pltpu.CompilerParams(
                dimension_semantics=("arbitrary", "arbitrary"), vmem_limit_bytes=VMEM_LIMIT),
            name=f"nsa_attn{qb}",
        )(*args)
    return out


def kernel(x, mix_norm, a_w_in, a_conv, a_w_out, ffn_norm, ffn_up, ffn_conv, ffn_down,
           kv_norm, w_kv, cmp_pe, cmp_w1, cmp_b1, cmp_w2, b_w_qg, b_w_o, rel_bias, final_norm):
    b, s, d = x.shape
    assert d == D_MODEL and s % 512 == 0 and mix_norm.shape[0] == 2 and b % BATCH_PER_STEP == 0
    x = _mixer(x, mix_norm[0], a_w_in[0], a_conv[0], a_w_out[0])
    x = _ffn(x, ffn_norm[0], ffn_up[0], ffn_conv[0], ffn_down[0])
    q, gates, kvc, kvr = _proj(x, mix_norm[1], kv_norm, b_w_qg[0], w_kv)
    kvcmp = _compress(kvc, cmp_pe, cmp_w1, cmp_b1, cmp_w2)
    attn = _attention(q, gates, kvr, kvcmp, rel_bias)
    return _ffn(x, ffn_norm[1], ffn_up[1], ffn_conv[1], ffn_down[1],
                attn=attn, w_o=b_w_o[0].astype(BF16), final_g=final_norm)
```

```python
import functools
import math

import jax
import jax.numpy as jnp
import numpy as np
from jax import lax
from jax.experimental import pallas as pl
from jax.experimental.pallas import tpu as pltpu

D_MODEL = 1024
CONV_W = 3
D_FF = 2816
N_HEADS = 16
N_GROUPS = 4
HPG = N_HEADS // N_GROUPS
HEAD_DIM = 64
CMP_LEN = 32
CMP_STRIDE = 16
CMP_HIDDEN = 128
SLC_LEN = 64
N_SEL = 16
WINDOW = 512
N_BUCKETS = 32
MAX_DISTANCE = 128
EPS = 1e-6
NEG = -1e30

LANES = 128
MXU_COLS = 256
V7X_VMEM_BYTES = 64 * 2**20
VMEM_LIMIT = V7X_VMEM_BYTES - 8 * 2**20

TQ = 256
CARRY_ROWS = 8
FFN_CHUNKS = 5
F32 = jnp.float32
BF16 = jnp.bfloat16


def _dot(a, b):
    return jnp.dot(a, b, preferred_element_type=F32)


def _dot_nt(a, b):
    return lax.dot_general(a, b, (((1,), (1,)), ((), ())), preferred_element_type=F32)


def _split_rows(dot, a, b):
    h = a.shape[0] // 2
    return jnp.concatenate([dot(a[:h], b), dot(a[h:], b)], axis=0)


def _rms_scale(x):
    return x * lax.rsqrt(jnp.mean(x * x, axis=-1, keepdims=True) + EPS)


def _conv3(buf_ref, cw, rows):
    c = CARRY_ROWS
    return (cw[0:1, :] * buf_ref[c - 2:c - 2 + rows, :]
            + cw[1:2, :] * buf_ref[c - 1:c - 1 + rows, :]
            + cw[2:3, :] * buf_ref[c:c + rows, :])


def _mixer_kernel(x_ref, g_ref, win_ref, cw_ref, wout_ref, o_ref, ubuf_ref, *, tm):
    d = D_MODEL

    @pl.when(pl.program_id(1) == 0)
    def _():
        ubuf_ref[0:CARRY_ROWS, :] = jnp.zeros((CARRY_ROWS, d), F32)

    x = x_ref[0]
    h = (_rms_scale(x) * g_ref[...]).astype(BF16)
    cg = _dot(h, win_ref[:, d:2 * d])
    v = _dot(h, win_ref[:, 2 * d:3 * d])
    ubuf_ref[CARRY_ROWS:CARRY_ROWS + tm, :] = cg * v
    conv = _conv3(ubuf_ref, cw_ref[...], tm)
    ubuf_ref[0:CARRY_ROWS, :] = ubuf_ref[tm:tm + CARRY_ROWS, :]
    bg = _dot(h, win_ref[:, 0:d])
    y = (bg * conv).astype(BF16)
    o_ref[0] = x + _dot(y, wout_ref[...])


def _mixer(x, g, w_in, conv_w, w_out, *, tm=1024):
    b, s, d = x.shape
    const = lambda *_: (0, 0)
    one = pl.Buffered(1)
    return pl.pallas_call(
        functools.partial(_mixer_kernel, tm=tm),
        grid=(b, s // tm),
        in_specs=[
            pl.BlockSpec((1, tm, d), lambda i, j: (i, j, 0)),
            pl.BlockSpec((1, d), const),
            pl.BlockSpec((d, 3 * d), const, pipeline_mode=one),
            pl.BlockSpec((CONV_W, d), const),
            pl.BlockSpec((d, d), const, pipeline_mode=one),
        ],
        out_specs=pl.BlockSpec((1, tm, d), lambda i, j: (i, j, 0)),
        out_shape=jax.ShapeDtypeStruct((b, s, d), F32),
        scratch_shapes=[pltpu.VMEM((CARRY_ROWS + tm, d), F32)],
        compiler_params=pltpu.CompilerParams(
            dimension_semantics=("arbitrary", "arbitrary"), vmem_limit_bytes=VMEM_LIMIT),
        name="mixer",
    )(x, g.reshape(1, d), w_in.astype(BF16), conv_w, w_out.astype(BF16))


def _ffn_chunks():
    tiles = D_FF // MXU_COLS
    assert tiles * MXU_COLS == D_FF
    bounds = [round(tiles * c / FFN_CHUNKS) * MXU_COLS for c in range(FFN_CHUNKS + 1)]
    return list(zip(bounds[:-1], bounds[1:]))


def _ffn_kernel(*refs, tm, has_attn, final_norm):
    refs = list(refs)
    x_ref = refs.pop(0)
    a_ref = refs.pop(0) if has_attn else None
    wo_ref = refs.pop(0) if has_attn else None
    g_ref, wup_ref, cw_ref, wdn_ref = refs[:4]
    refs = refs[4:]
    gf_ref = refs.pop(0) if final_norm else None
    chunks = _ffn_chunks()
    o_ref, h_ref, carry_ref, acc_ref = refs[:4]
    u_refs = refs[4:]
    assert len(u_refs) == 2 * len(chunks)

    @pl.when(pl.program_id(1) == 0)
    def _():
        carry_ref[...] = jnp.zeros(carry_ref.shape, F32)

    x = x_ref[0]
    if has_attn:
        x = x + _dot(a_ref[0], wo_ref[...])
    h_ref[...] = (_rms_scale(x) * g_ref[...]).astype(BF16)

    def up(buf_ref, cols):
        buf_ref[0:CARRY_ROWS, :] = carry_ref[:, cols]
        buf_ref[CARRY_ROWS:CARRY_ROWS + tm, :] = _dot(h_ref[...], wup_ref[:, cols])
        carry_ref[:, cols] = buf_ref[tm:tm + CARRY_ROWS, :]

    for c, (lo, hi) in enumerate(chunks):
        up(u_refs[2 * c], slice(lo, hi))
        up(u_refs[2 * c + 1], slice(D_FF + lo, D_FF + hi))

    for c, (lo, hi) in enumerate(chunks):
        a = _conv3(u_refs[2 * c], cw_ref[:, lo:hi], tm)
        gt = _conv3(u_refs[2 * c + 1], cw_ref[:, D_FF + lo:D_FF + hi], tm)
        act = (a * jax.nn.sigmoid(a) * gt).astype(BF16)
        part = _dot(act, wdn_ref[lo:hi, :])
        if c == 0:
            acc_ref[...] = x + part
        else:
            acc_ref[...] += part
    y = acc_ref[...]
    if final_norm:
        y = _rms_scale(y) * gf_ref[...]
    o_ref[0] = y


def _ffn(x, g, w_up, conv_w, w_down, *, attn=None, w_o=None, final_g=None, tm=512):
    b, s, d = x.shape
    has_attn = attn is not None
    final_norm = final_g is not None
    const = lambda *_: (0, 0)
    row = lambda i, j: (i, j, 0)
    one = pl.Buffered(1)
    args = [x]
    in_specs = [pl.BlockSpec((1, tm, d), row)]
    if has_attn:
        ka = attn.shape[-1]
        args += [attn, w_o]
        in_specs += [pl.BlockSpec((1, tm, ka), row),
                     pl.BlockSpec((ka, d), const, pipeline_mode=one)]
    args += [g.reshape(1, d), w_up.astype(BF16), conv_w, w_down.astype(BF16)]
    in_specs += [pl.BlockSpec((1, d), const),
                 pl.BlockSpec((d, 2 * D_FF), const, pipeline_mode=one),
                 pl.BlockSpec((CONV_W, 2 * D_FF), const),
                 pl.BlockSpec((D_FF, d), const, pipeline_mode=one)]
    if final_norm:
        args.append(final_g.reshape(1, d))
        in_specs.append(pl.BlockSpec((1, d), const))
    return pl.pallas_call(
        functools.partial(_ffn_kernel, tm=tm, has_attn=has_attn, final_norm=final_norm),
        grid=(b, s // tm),
        in_specs=in_specs,
        out_specs=pl.BlockSpec((1, tm, d), row),
        out_shape=jax.ShapeDtypeStruct((b, s, d), F32),
        scratch_shapes=[
            pltpu.VMEM((tm, d), BF16),
            pltpu.VMEM((CARRY_ROWS, 2 * D_FF), F32),
            pltpu.VMEM((tm, d), F32),
        ] + [pltpu.VMEM((CARRY_ROWS + tm, hi - lo), F32) for lo, hi in _ffn_chunks() for _ in range(2)],
        compiler_params=pltpu.CompilerParams(
            dimension_semantics=("arbitrary", "arbitrary"), vmem_limit_bytes=VMEM_LIMIT),
        name="ffn_attn" if has_attn else "ffn",
    )(*args)


def _proj_kernel(x_ref, gm_ref, gk_ref, wq_ref, wg_ref, wkc_ref, wkr_ref,
                 q_ref, gate_ref, kvc_ref, kvr_ref):
    xn = _rms_scale(x_ref[0])
    hq = (xn * gm_ref[...]).astype(BF16)
    hs = (xn * gk_ref[...]).astype(BF16)
    q_ref[0] = (_dot(hq, wq_ref[...]) * (HEAD_DIM ** -0.5)).astype(BF16)
    gates = jax.nn.sigmoid(_dot(hq, wg_ref[...]))
    gate_ref[0] = gates
    kvc = _dot(hs, wkc_ref[...])
    for sl in range(2 * N_GROUPS * HEAD_DIM // LANES):
        kvc_ref[0, sl] = kvc[:, sl * LANES:(sl + 1) * LANES]
    kvr_ref[0] = _dot(hs, wkr_ref[...]).astype(BF16)


def _proj(x, g_mix, g_kv, w_qg, w_kv, *, tm=1024):
    b, s, d = x.shape
    hd = N_HEADS * HEAD_DIM
    gd = N_GROUPS * HEAD_DIM
    w_q = w_qg[:, :hd].astype(BF16)
    w_g = jnp.pad(w_qg[:, hd:], ((0, 0), (0, LANES - 3 * N_HEADS))).astype(BF16)
    w_kc = w_kv[:, :2 * gd].astype(BF16)
    w_kr = (w_kv[:, 2 * gd:].reshape(d, 4, N_GROUPS, HEAD_DIM)
            .transpose(0, 2, 1, 3).reshape(d, 4 * gd).astype(BF16))
    const = lambda *_: (0, 0)
    row = lambda i, j: (i, j, 0)
    one = pl.Buffered(1)
    return pl.pallas_call(
        _proj_kernel,
        grid=(b, s // tm),
        in_specs=[
            pl.BlockSpec((1, tm, d), row),
            pl.BlockSpec((1, d), const),
            pl.BlockSpec((1, d), const),
            pl.BlockSpec((d, hd), const, pipeline_mode=one),
            pl.BlockSpec((d, LANES), const, pipeline_mode=one),
            pl.BlockSpec((d, 2 * gd), const, pipeline_mode=one),
            pl.BlockSpec((d, 4 * gd), const, pipeline_mode=one),
        ],
        out_specs=[
            pl.BlockSpec((1, tm, hd), row),
            pl.BlockSpec((1, tm, LANES), row),
            pl.BlockSpec((1, 2 * gd // LANES, tm, LANES), lambda i, j: (i, 0, j, 0)),
            pl.BlockSpec((1, tm, 4 * gd), row),
        ],
        out_shape=[
            jax.ShapeDtypeStruct((b, s, hd), BF16),
            jax.ShapeDtypeStruct((b, s, LANES), F32),
            jax.ShapeDtypeStruct((b, 2 * gd // LANES, s, LANES), F32),
            jax.ShapeDtypeStruct((b, s, 4 * gd), BF16),
        ],
        compiler_params=pltpu.CompilerParams(
            dimension_semantics=("arbitrary", "arbitrary"), vmem_limit_bytes=VMEM_LIMIT),
        name="proj",
    )(x, g_mix.reshape(1, d), g_kv.reshape(1, d), w_q, w_g, w_kc, w_kr)


def _gelu_tanh(x):
    return 0.5 * x * (1.0 + jnp.tanh(math.sqrt(2.0 / math.pi) * (x + 0.044715 * (x * x * x))))


def _compress_kernel(k_ref, v_ref, pe_ref, w1_ref, b1_ref, w2_ref, o_ref):
    n_chunks = int(o_ref.shape[2])
    outs = [None, None]
    for j, src in enumerate((k_ref, v_ref)):
        first = None
        second = None
        for l in range(CMP_STRIDE):
            x = src.at[0, 0][pl.ds(l, n_chunks, stride=CMP_STRIDE), :]
            pa = _dot((x + pe_ref[j, l:l + 1, :]).astype(BF16), w1_ref[j, l])
            pb = _dot((x + pe_ref[j, CMP_STRIDE + l:CMP_STRIDE + l + 1, :]).astype(BF16),
                      w1_ref[j, CMP_STRIDE + l])
            first = pa if first is None else first + pa
            second = pb if second is None else second + pb
        pre = first + pltpu.roll(second, n_chunks - 1, 0) + b1_ref[j]
        hid = _gelu_tanh(pre).astype(BF16)
        for e in range(2):
            part = _dot(hid[:, e * CMP_HIDDEN:(e + 1) * CMP_HIDDEN], w2_ref[j])
            outs[e] = part if outs[e] is None else outs[e] + part
    for e in range(2):
        o_ref[0, e] = outs[e].astype(BF16)


def _compress(kvc, cmp_pe, cmp_w1, cmp_b1, cmp_w2):
    b, n_slabs, s, _ = kvc.shape
    n_chunks = s // CMP_STRIDE
    pairs = N_GROUPS // 2
    assert n_slabs == 2 * pairs and 2 * HEAD_DIM == LANES
    pe = jnp.concatenate([cmp_pe, cmp_pe], axis=-1)
    w1 = cmp_w1.reshape(2, CMP_LEN, HEAD_DIM, CMP_HIDDEN)
    zero = jnp.zeros_like(w1)
    w1 = jnp.concatenate([jnp.concatenate([w1, zero], axis=-1),
                          jnp.concatenate([zero, w1], axis=-1)], axis=2).astype(BF16)
    b1 = jnp.concatenate([cmp_b1, cmp_b1], axis=-1).reshape(2, 1, 2 * CMP_HIDDEN)
    w2 = jnp.stack([jnp.pad(cmp_w2[0], ((0, 0), (0, HEAD_DIM))),
                    jnp.pad(cmp_w2[1], ((0, 0), (HEAD_DIM, 0)))]).astype(BF16)
    const = lambda *_: (0, 0, 0)
    return pl.pallas_call(
        _compress_kernel,
        grid=(b, pairs),
        in_specs=[
            pl.BlockSpec((1, 1, s, LANES), lambda i, p: (i, p, 0, 0)),
            pl.BlockSpec((1, 1, s, LANES), lambda i, p: (i, pairs + p, 0, 0)),
            pl.BlockSpec((2, CMP_LEN, LANES), const),
            pl.BlockSpec((2, CMP_LEN, LANES, 2 * CMP_HIDDEN), lambda *_: (0, 0, 0, 0)),
            pl.BlockSpec((2, 1, 2 * CMP_HIDDEN), const),
            pl.BlockSpec((2, CMP_HIDDEN, LANES), const),
        ],
        out_specs=pl.BlockSpec((1, 2, n_chunks, LANES), lambda i, p: (i, p, 0, 0)),
        out_shape=jax.ShapeDtypeStruct((b, N_GROUPS, n_chunks, LANES), BF16),
        compiler_params=pltpu.CompilerParams(
            dimension_semantics=("arbitrary", "arbitrary"), vmem_limit_bytes=VMEM_LIMIT),
        name="compress",
    )(kvc, kvc, pe, w1, b1, w2)


TILE_PREV2, TILE_PREV, TILE_DIAG = 0, 1, 2
N_KINDS = 3
CMP_WIN = 32
LOG2E = 1.0 / math.log(2.0)


def _rel_bucket_np(dist):
    max_exact = N_BUCKETS // 2
    d = np.maximum(dist, 0)
    df = np.maximum(d, 1).astype(np.float32)
    large = max_exact + (np.log(df / max_exact) / np.float32(math.log(MAX_DISTANCE / max_exact))
                         * (N_BUCKETS - max_exact)).astype(np.int32)
    return np.where(d < max_exact, d, np.minimum(large, N_BUCKETS - 1))


def _toeplitz(w):
    t = w.shape[-1] // 2
    lead = w.shape[:-1]
    a = jnp.broadcast_to(w[..., None, :], lead + (t, 2 * t)).reshape(lead + (2 * t * t,))
    return a[..., :t * (2 * t - 1)].reshape(lead + (t, 2 * t - 1))[..., :t]


def _attn_tables(rel_bias, s):
    nq = s // TQ
    lead = CMP_WIN // 4
    far_dist = min(CMP_STRIDE * (lead + 1) - (CMP_LEN - 1), TQ + 1)
    assert _rel_bucket_np(np.arange(far_dist, s + TQ)).min() == N_BUCKETS - 1
    assert TQ - 1 - CMP_STRIDE * (CMP_WIN - lead) - (CMP_LEN - 1) < 0
    assert WINDOW == 2 * TQ and TQ % SLC_LEN == 0 and TQ % CMP_STRIDE == 0
    relb = (rel_bias.astype(F32) - rel_bias[N_BUCKETS - 1].astype(F32)).T
    f = jnp.take(relb, jnp.asarray(_rel_bucket_np(np.arange(2 * TQ))), axis=1)
    neg = jnp.full((N_HEADS, TQ), NEG, F32)
    w_diag = jnp.concatenate([f[:, 0:1], neg, jnp.flip(f[:, 1:TQ], axis=1)], axis=1)
    w_prev = jnp.concatenate([jnp.flip(f[:, 1:TQ + 1], axis=1), jnp.zeros((N_HEADS, 1), F32),
                              jnp.flip(f[:, TQ + 1:2 * TQ], axis=1)], axis=1)
    ti = np.arange(TQ)[:, None]
    ki = np.arange(TQ)[None, :]
    prev2 = jnp.broadcast_to(jnp.asarray(np.where(ki > ti, 0.0, NEG), F32), (N_HEADS, TQ, TQ))
    b3 = jnp.stack([prev2, _toeplitz(w_prev), _toeplitz(w_diag)], axis=1)
    b3 = (b3.reshape(N_GROUPS, HPG, N_KINDS, TQ, TQ).transpose(0, 2, 1, 3, 4)
          .reshape(N_GROUPS, N_KINDS, HPG * TQ, TQ))
    n_cmp = s // CMP_STRIDE
    dist = ti - CMP_STRIDE * (np.arange(CMP_WIN)[None, :] - lead) - (CMP_LEN - 1)
    pw = jnp.take(relb, jnp.asarray(_rel_bucket_np(dist)), axis=1)
    pw = jnp.where(jnp.asarray(dist >= 0), pw, NEG)
    full = jnp.concatenate([jnp.zeros((N_HEADS, TQ, n_cmp - lead), F32), pw,
                            jnp.full((N_HEADS, TQ, n_cmp), NEG, F32)], axis=-1)
    per_tile = TQ // CMP_STRIDE
    bc = jnp.stack([full[..., n_cmp - per_tile * t:2 * n_cmp - per_tile * t] for t in range(nq)])
    bc = bc.reshape(nq, N_GROUPS, HPG * TQ, n_cmp)
    return b3, bc


def _attn_consts(s):
    n_slc = s // SLC_LEN
    n_cmp = s // CMP_STRIDE
    key = np.arange(s)[:, None]
    lane = np.arange(LANES)[None, :]
    ceneg = np.where((key // SLC_LEN == lane) & (lane < n_slc), NEG, 0.0)
    j = np.arange(n_slc)[:, None]
    i = np.arange(n_cmp)[None, :]
    ov = ((i * CMP_STRIDE < j * SLC_LEN + SLC_LEN) & (i * CMP_STRIDE + CMP_LEN > j * SLC_LEN)
          & (i < n_cmp - 1))
    place = np.eye(n_slc, LANES)
    return (jnp.asarray(ceneg, BF16), jnp.asarray(np.concatenate([ov] * 3, axis=1), BF16),
            jnp.asarray(place, BF16))


BATCH_PER_STEP = 2
STAGGER, DONE = "stagger", "done"


def _attn_kernel(q_ref, gate_ref, kv_ref, kvc_ref, b3_ref, bc_ref, ceneg_ref, ov_ref, place_ref,
                 o_ref, *scratch, qb):
    streams = [
        _attn_stream(q_ref.at[pl.ds(e, 1)], gate_ref.at[pl.ds(e, 1)], kv_ref.at[pl.ds(e, 1)],
                     kvc_ref.at[pl.ds(e, 1)], b3_ref, bc_ref, ceneg_ref, ov_ref, place_ref,
                     o_ref.at[pl.ds(e, 1)], *[s.at[e] for s in scratch], qb=qb)
        for e in range(BATCH_PER_STEP)]
    active = []
    for stream in streams:
        active.append(stream)
        while next(stream, STAGGER) is not STAGGER:
            for other in active[:-1]:
                next(other, None)
    while active:
        for stream in list(active):
            if next(stream, DONE) is DONE:
                active.remove(stream)


def _attn_stream(q_ref, gate_ref, kv_ref, kvc_ref, b3_ref, bc_ref, ceneg_ref, ov_ref, place_ref,
                 o_ref, q4_ref, sc_ref, nsc_ref, macc_ref, mb_ref, oacc_ref, comb_ref, *, qb):
    n_slc = ov_ref.shape[0]
    rows = HPG * TQ
    lane = lax.broadcasted_iota(jnp.int32, (TQ, LANES), 1)
    low = lane < HEAD_DIM

    for pair in range(HPG // 2):
        qp = q_ref[0, :, pair * LANES:(pair + 1) * LANES].astype(F32)
        q4_ref[(2 * pair) * TQ:(2 * pair + 1) * TQ, 0:LANES] = jnp.where(low, qp, 0.0).astype(BF16)
        q4_ref[(2 * pair + 1) * TQ:(2 * pair + 2) * TQ, 0:LANES] = (
            jnp.where(low, pltpu.roll(qp, HEAD_DIM, 1), 0.0).astype(BF16))
    q4 = q4_ref[:, 0:LANES]

    def key_rows(kt, col0, n=1):
        return kv_ref[0, kt * TQ:(kt + n) * TQ, col0:col0 + LANES]

    def lane_tiles(x):
        return [x[:, i:i + LANES] for i in range(0, x.shape[1], LANES)]

    def tile_max(x):
        return functools.reduce(jnp.maximum, lane_tiles(x))

    def exp2_rel(sv, mb):
        return jnp.concatenate([jnp.exp2(t - mb) for t in lane_tiles(sv)], axis=1).astype(BF16)

    win_tiles = [(kt, kind) for kt, kind in
                 ((qb - 2, TILE_PREV2), (qb - 1, TILE_PREV), (qb, TILE_DIAG)) if kt >= 0]
    near_tiles = win_tiles[-2:]
    n_far = max(qb - 1, 0)
    far_groups = [(t, min(2, n_far - t)) for t in range(0, n_far, 2)]

    def split3(x):
        hi = x.astype(BF16)
        r1 = x - hi.astype(F32)
        mid = r1.astype(BF16)
        lo = (r1 - mid.astype(F32)).astype(BF16)
        return jnp.concatenate([hi, mid, lo], axis=1)

    def ones_v(tile):
        return jnp.concatenate([tile, jnp.ones_like(tile)], axis=1)

    def weighted(o, hh, br):
        r = slice(hh * TQ, (hh + 1) * TQ)
        l = o[r, LANES:2 * LANES]
        if br == 0:
            l = jnp.where(l > 0.0, l, 1.0)
        gate = gate_ref[0, 0, :, 3 * hh + br:3 * hh + br + 1]
        return o[r, 0:LANES] * (gate / l)

    tiles_w = [key_rows(kt, LANES) for kt, _ in win_tiles]
    s_w = []

    def win_scores():
        i = len(s_w)
        if i < len(win_tiles):
            s_w.append((_split_rows(_dot_nt, q4, tiles_w[i]) + b3_ref[0, win_tiles[i][1]]) * LOG2E)

    kvc = kvc_ref[0, 0]
    bias_c = bc_ref[0, 0]
    s_c = _split_rows(_dot_nt, q4, kvc) + bias_c
    yield
    win_scores()
    yield
    m_c = jnp.maximum(jnp.max(s_c, axis=1, keepdims=True), 0.5 * NEG)
    e_c = jnp.exp(s_c - m_c)
    o_c = _split_rows(_dot, e_c.astype(BF16), ones_v(kvc))
    yield
    win_scores()
    yield
    l_c = o_c[:, LANES:2 * LANES]
    p_c = e_c / jnp.where(l_c > 0.0, l_c, 1.0)
    psum = p_c[0:TQ]
    for hh in range(1, HPG):
        psum = psum + p_c[hh * TQ:(hh + 1) * TQ]

    imp_t = _dot_nt(ov_ref[...], split3(psum))
    yield
    win_scores()
    yield
    rhs_w = ones_v(jnp.concatenate(tiles_w, axis=0))
    o_w = []
    for hh in range(HPG):
        sw_h = jnp.concatenate([t[hh * TQ:(hh + 1) * TQ] for t in s_w], axis=1)
        m_h = jnp.max(tile_max(sw_h), axis=1, keepdims=True)
        o_w.append(_dot(exp2_rel(sw_h, jnp.broadcast_to(m_h, (TQ, LANES))), rhs_w))
    o_w = jnp.concatenate(o_w, axis=0)
    yield

    jrow = lax.broadcasted_iota(jnp.int32, (n_slc, TQ), 0)
    tpos = qb * TQ + lax.broadcasted_iota(jnp.int32, (n_slc, TQ), 1)
    cur = lax.shift_right_logical(tpos, int(math.log2(SLC_LEN)))
    forced = (jrow == 0) | (jrow == cur) | (jrow == cur - 1)
    score = jnp.where(forced, 3e38, jnp.where(jrow <= cur, imp_t, -1.0))
    rank = jnp.zeros((n_slc, TQ), F32)
    for k in range(n_slc):
        sk = score[k:k + 1, :]
        beats = (sk > score) | ((sk == score) & (jrow > k))
        rank = rank + beats.astype(F32)
    notsel_t = (rank >= float(N_SEL)).astype(BF16)
    q_mask = lax.dot_general(notsel_t, place_ref[...], (((0,), (0,)), ((), ())),
                             preferred_element_type=F32).astype(BF16)
    yield
    for hh in range(HPG):
        q4_ref[hh * TQ:(hh + 1) * TQ, LANES:2 * LANES] = q_mask

    for hh in range(HPG):
        comb_ref[hh * TQ:(hh + 1) * TQ, :] = weighted(o_c, hh, 0) + weighted(o_w, hh, 2)

    kt_n, n_near = near_tiles[0][0], len(near_tiles)
    ks_n = slice(kt_n * TQ, (kt_n + n_near) * TQ)
    bias_n = jnp.concatenate([b3_ref[0, kind] for _, kind in near_tiles], axis=1)
    rhs_n = jnp.concatenate([key_rows(kt_n, 0, n_near), ceneg_ref[ks_n, :]], axis=1)
    dot_n = _dot_nt if n_near == 2 else functools.partial(_split_rows, _dot_nt)
    s_n = (dot_n(q4_ref[...], rhs_n) + bias_n) * LOG2E
    yield
    nsc_ref[...] = s_n
    macc_ref[len(far_groups)] = tile_max(s_n)

    for i, (kt, n) in enumerate(far_groups):
        ks = slice(kt * TQ, (kt + n) * TQ)
        rhs = jnp.concatenate([key_rows(kt, 0, n), ceneg_ref[ks, :]], axis=1)
        dot_f = _dot_nt if n == 2 else functools.partial(_split_rows, _dot_nt)
        sv = dot_f(q4_ref[...], rhs) * LOG2E
        yield
        sc_ref[i, :, 0:n * TQ] = sv
        macc_ref[i] = tile_max(sv)

    yield STAGGER
    m_el = macc_ref[0]
    for i in range(1, len(far_groups) + 1):
        m_el = jnp.maximum(m_el, macc_ref[i])
    m_s = jnp.max(m_el, axis=1, keepdims=True)
    mb_ref[...] = jnp.broadcast_to(m_s, (rows, LANES))
    oacc_ref[...] = jnp.zeros((rows, 2 * LANES), F32)

    for i, (kt, n) in enumerate(far_groups):
        rhs = ones_v(key_rows(kt, 0, n))
        for r in (slice(0, rows // 2), slice(rows // 2, rows)):
            oacc_ref[r, :] += _dot(exp2_rel(sc_ref[i, r, 0:n * TQ], mb_ref[r, :]), rhs)
            yield

    rhs_v = ones_v(key_rows(kt_n, 0, n_near))
    o_s = jnp.concatenate(
        [oacc_ref[r, :] + _dot(exp2_rel(nsc_ref[r, :], mb_ref[r, :]), rhs_v)
         for r in (slice(hh * TQ, (hh + 1) * TQ) for hh in range(HPG))], axis=0)
    yield

    for hp in range(HPG // 2):
        even, odd = [comb_ref[hh * TQ:(hh + 1) * TQ, :] + weighted(o_s, hh, 1)
                     for hh in (2 * hp, 2 * hp + 1)]
        o_ref[0, :, hp * LANES:(hp + 1) * LANES] = (
            jnp.where(low, pltpu.roll(even, HEAD_DIM, 1), odd).astype(BF16))


def _attention(q, gates, kvr, kvcmp, rel_bias):
    b, s, hd = q.shape
    nq = s // TQ
    n_slc = s // SLC_LEN
    n_cmp = s // CMP_STRIDE
    rows = HPG * TQ
    b3, bc = _attn_tables(rel_bias, s)
    ceneg, ov, place = _attn_consts(s)
    gw = 3 * HPG
    gates_g = (gates[:, :, :3 * N_HEADS].reshape(b, s, N_GROUPS, gw).transpose(0, 2, 1, 3))
    gl = HPG * HEAD_DIM
    out = q
    for qb in range(nq):
        n_groups = (max(qb - 1, 0) + 1) // 2
        n_near = min(qb + 1, 2)
        n_keys = (qb + 1) * TQ
        nb = BATCH_PER_STEP
        in_specs = [
            pl.BlockSpec((nb, TQ, gl), lambda g, i, qb=qb: (i, qb, g)),
            pl.BlockSpec((nb, 1, TQ, gw), lambda g, i, qb=qb: (i, g, qb, 0)),
            pl.BlockSpec((nb, n_keys, 4 * HEAD_DIM), lambda g, i: (i, 0, g)),
            pl.BlockSpec((nb, 1, n_cmp, LANES), lambda g, i: (i, g, 0, 0)),
            pl.BlockSpec((1, N_KINDS, rows, TQ), lambda g, i: (g, 0, 0, 0)),
            pl.BlockSpec((1, 1, rows, n_cmp), lambda g, i, qb=qb: (qb, g, 0, 0)),
            pl.BlockSpec((n_keys, LANES), lambda *_: (0, 0)),
            pl.BlockSpec((n_slc, 3 * n_cmp), lambda *_: (0, 0)),
            pl.BlockSpec((n_slc, LANES), lambda *_: (0, 0)),
        ]
        args = [out, gates_g, kvr, kvcmp, b3, bc, ceneg, ov, place]
        out = pl.pallas_call(
            functools.partial(_attn_kernel, qb=qb),
            grid=(N_GROUPS, b // nb),
            in_specs=in_specs,
            out_specs=pl.BlockSpec((nb, TQ, gl), lambda g, i, qb=qb: (i, qb, g)),
            out_shape=jax.ShapeDtypeStruct((b, s, N_HEADS * HEAD_DIM), BF16),
            scratch_shapes=[
                pltpu.VMEM((nb, rows, 2 * LANES), BF16),
                pltpu.VMEM((nb, max(n_groups, 1), rows, 2 * TQ), F32),
                pltpu.VMEM((nb, rows, n_near * TQ), F32),
                pltpu.VMEM((nb, n_groups + 1, rows, LANES), F32),
                pltpu.VMEM((nb, rows, LANES), F32),
                pltpu.VMEM((nb, rows, 2 * LANES), F32),
                pltpu.VMEM((nb, rows, LANES), F32),
            ],
            input_output_aliases={0: 0},
            compiler_params=pltpu.CompilerParams(
                dimension_semantics=("arbitrary", "arbitrary"), vmem_limit_bytes=VMEM_LIMIT),
            name=f"nsa_attn{qb}",
        )(*args)
    return out


def kernel(x, mix_norm, a_w_in, a_conv, a_w_out, ffn_norm, ffn_up, ffn_conv, ffn_down,
           kv_norm, w_kv, cmp_pe, cmp_w1, cmp_b1, cmp_w2, b_w_qg, b_w_o, rel_bias, final_norm):
    b, s, d = x.shape
    assert d == D_MODEL and s % 512 == 0 and mix_norm.shape[0] == 2 and b % BATCH_PER_STEP == 0
    x = _mixer(x, mix_norm[0], a_w_in[0], a_conv[0], a_w_out[0])
    x = _ffn(x, ffn_norm[0], ffn_up[0], ffn_conv[0], ffn_down[0])
    q, gates, kvc, kvr = _proj(x, mix_norm[1], kv_norm, b_w_qg[0], w_kv)
    kvcmp = _compress(kvc, cmp_pe, cmp_w1, cmp_b1, cmp_w2)
    attn = _attention(q, gates, kvr, kvcmp, rel_bias)
    return _ffn(x, ffn_norm[1], ffn_up[1], ffn_conv[1], ffn_down[1],
                attn=attn, w_o=b_w_o[0].astype(BF16), final_g=final_norm)
```
